```python
import math
import jax, jax.numpy as jnp
from jax import lax
import numpy as np

D_MODEL = 1024
BATCH = 2
SEQ = 8192
DEPTH = 2

N_EVEN = (DEPTH + 1) // 2
N_ODD = DEPTH // 2

D_A = D_MODEL // 2
D_B = D_MODEL // 2
CONV_A = 31
CONV_B = 3
D_IN_EVEN = 2 * D_A + 3 * D_B

D_RNN = D_MODEL
RNN_BLOCK = 256
RNN_HEADS = D_RNN // RNN_BLOCK
CONV_C = 4
LRU_C = 8.0

N_EXPERTS = 16
N_EXPERT_GROUPS = 4
EXPERTS_PER_GROUP = N_EXPERTS // N_EXPERT_GROUPS
TOP_K = 2
D_EXPERT = D_MODEL
MOE_BLOCK = 128

LN_EPS = 1e-5
ALPHA = (2.0 * DEPTH) ** 0.25
BETA = (8.0 * DEPTH) ** -0.25

kernel_name = "hybrid_conformer_shortconv_rglru_grouped_moe_deepnorm"


def layer_norm(x, g, b):
    xf = x.astype(jnp.float32)
    mu = jnp.mean(xf, axis=-1, keepdims=True)
    var = jnp.mean(jnp.square(xf - mu), axis=-1, keepdims=True)
    y = (xf - mu) * lax.rsqrt(var + LN_EPS)
    return (y * g.astype(jnp.float32) + b.astype(jnp.float32)).astype(x.dtype)


def causal_depthwise_conv(x, w):
    k, c = w.shape
    return lax.conv_general_dilated(
        x, w[:, None, :].astype(x.dtype), window_strides=(1,), padding=[(k - 1, 0)],
        dimension_numbers=("NWC", "WIO", "NWC"), feature_group_count=c)


def conv_mixer(x, w_in, a_dw, a_dw_b, a_ln_g, a_ln_b, b_dw, w_out):
    u = jnp.einsum("bsd,de->bse", x, w_in)
    a_val, a_gate, b_b, b_c, b_x = jnp.split(
        u, [D_A, 2 * D_A, 2 * D_A + D_B, 2 * D_A + 2 * D_B], axis=-1)
    a = a_val * jax.nn.sigmoid(a_gate)
    a = causal_depthwise_conv(a, a_dw) + a_dw_b
    a = jax.nn.silu(layer_norm(a, a_ln_g, a_ln_b))
    bb = b_b * causal_depthwise_conv(b_c * b_x, b_dw)
    return jnp.einsum("bse,ed->bsd", jnp.concatenate([a, bb], axis=-1), w_out)


def _linear_recurrence_combine(c1, c2):
    a1, b1 = c1
    a2, b2 = c2
    return a1 * a2, a2 * b1 + b2


def rglru_mixer(x, w_in, c_dw, c_dw_b, w_gate_a, b_gate_a, w_gate_x, b_gate_x, lam, w_out):
    bsz, seq, _ = x.shape
    u = jnp.einsum("bsd,de->bse", x, w_in)
    y_branch, xr = jnp.split(u, 2, axis=-1)
    xr = causal_depthwise_conv(xr, c_dw) + c_dw_b
    xh = xr.reshape(bsz, seq, RNN_HEADS, RNN_BLOCK)
    r = jax.nn.sigmoid(jnp.einsum("bshi,hij->bshj", xh, w_gate_a) + b_gate_a)
    i = jax.nn.sigmoid(jnp.einsum("bshi,hij->bshj", xh, w_gate_x) + b_gate_x)
    r = r.reshape(bsz, seq, D_RNN).astype(jnp.float32)
    i = i.reshape(bsz, seq, D_RNN)
    log_a = -LRU_C * r * jax.nn.softplus(-lam.astype(jnp.float32))
    a = jnp.exp(log_a)
    mult = jnp.sqrt(-jnp.expm1(2.0 * log_a))
    b = mult * (i * xr).astype(jnp.float32)
    _, h = lax.associative_scan(_linear_recurrence_combine, (a, b), axis=1)
    out = jax.nn.gelu(y_branch) * h.astype(x.dtype)
    return jnp.einsum("bse,ed->bsd", out, w_out)


def route(xf, w_router, b_router):
    t = xf.shape[0]
    logits = (xf @ w_router + b_router).astype(jnp.float32)
    probs = jax.nn.softmax(logits, axis=-1)
    pg = probs.reshape(t, N_EXPERT_GROUPS, EXPERTS_PER_GROUP)
    group_score = lax.top_k(pg, TOP_K)[0].sum(-1)
    g_sel = jnp.argmax(group_score, axis=-1)
    p_in = jnp.take_along_axis(pg, g_sel[:, None, None], axis=1)[:, 0]
    top_p, top_i = lax.top_k(p_in, TOP_K)
    expert_idx = g_sel[:, None] * EXPERTS_PER_GROUP + top_i
    gates = top_p / jnp.sum(top_p, axis=-1, keepdims=True)
    return expert_idx, gates


def moe(x, w_router, b_router, w1, w3, w2):
    bsz, seq, d = x.shape
    t = bsz * seq
    xf = x.reshape(t, d)
    e_idx, gates = route(xf, w_router, b_router)
    n_assign = t * TOP_K
    e_flat = e_idx.reshape(n_assign)
    tok_flat = jnp.repeat(jnp.arange(t, dtype=jnp.int32), TOP_K)
    g_flat = gates.reshape(n_assign)
    order = jnp.argsort(e_flat)
    e_s, tok_s, g_s = e_flat[order], tok_flat[order], g_flat[order]
    counts = jnp.bincount(e_flat, length=N_EXPERTS)
    padded = (counts + MOE_BLOCK - 1) // MOE_BLOCK * MOE_BLOCK
    start = jnp.cumsum(counts) - counts
    pend = jnp.cumsum(padded)
    pstart = pend - padded
    dest = pstart[e_s] + jnp.arange(n_assign, dtype=jnp.int32) - start[e_s]
    n_blocks = -(-n_assign // MOE_BLOCK) + N_EXPERTS
    n_pad = n_blocks * MOE_BLOCK
    x_pad = jnp.zeros((n_pad, d), x.dtype).at[dest].set(xf[tok_s])
    block_expert = jnp.minimum(
        jnp.searchsorted(pend, jnp.arange(n_blocks, dtype=pend.dtype) * MOE_BLOCK, side="right"),
        N_EXPERTS - 1)

    def expert_block(args):
        xb, e = args
        hb = jax.nn.silu(xb @ w1[e]) * (xb @ w3[e])
        return hb @ w2[e]

    y_pad = lax.map(expert_block, (x_pad.reshape(n_blocks, MOE_BLOCK, d), block_expert))
    y = y_pad.reshape(n_pad, d)[dest] * g_s[:, None].astype(x.dtype)
    out = jnp.zeros((t, d), x.dtype).at[tok_s].add(y)
    return out.reshape(bsz, seq, d)


def setup_inputs(seed: int = 0) -> dict:
    key = jax.random.key(seed)
    ks = jax.random.split(key, 26)
    nrm = lambda k, shape, s: jax.random.normal(k, shape, jnp.float32) * s
    a0 = jax.random.uniform(ks[19], (N_ODD, D_RNN), jnp.float32, 0.9, 0.999)
    s0 = a0 ** (1.0 / LRU_C)
    lam = jnp.log(s0) - jnp.log1p(-s0)
    return {
        "x": nrm(ks[0], (BATCH, SEQ, D_MODEL), 1.0),
        "ln1_g": 1.0 + nrm(ks[1], (DEPTH, D_MODEL), 0.02),
        "ln1_b": nrm(ks[2], (DEPTH, D_MODEL), 0.02),
        "ln2_g": 1.0 + nrm(ks[3], (DEPTH, D_MODEL), 0.02),
        "ln2_b": nrm(ks[4], (DEPTH, D_MODEL), 0.02),
        "even_w_in": nrm(ks[5], (N_EVEN, D_MODEL, D_IN_EVEN), D_MODEL ** -0.5),
        "even_a_dw": nrm(ks[6], (N_EVEN, CONV_A, D_A), CONV_A ** -0.5),
        "even_a_dw_b": nrm(ks[7], (N_EVEN, D_A), 0.02),
        "even_a_ln_g": 1.0 + nrm(ks[8], (N_EVEN, D_A), 0.02),
        "even_a_ln_b": nrm(ks[9], (N_EVEN, D_A), 0.02),
        "even_b_dw": nrm(ks[10], (N_EVEN, CONV_B, D_B), CONV_B ** -0.5),
        "even_w_out": nrm(ks[11], (N_EVEN, D_A + D_B, D_MODEL), BETA * (D_A + D_B) ** -0.5),
        "odd_w_in": nrm(ks[12], (N_ODD, D_MODEL, 2 * D_RNN), D_MODEL ** -0.5),
        "odd_c_dw": nrm(ks[13], (N_ODD, CONV_C, D_RNN), CONV_C ** -0.5),
        "odd_c_dw_b": nrm(ks[14], (N_ODD, D_RNN), 0.02),
        "odd_w_gate_a": nrm(ks[15], (N_ODD, RNN_HEADS, RNN_BLOCK, RNN_BLOCK), RNN_BLOCK ** -0.5),
        "odd_b_gate_a": nrm(ks[16], (N_ODD, RNN_HEADS, RNN_BLOCK), 0.02),
        "odd_w_gate_x": nrm(ks[17], (N_ODD, RNN_HEADS, RNN_BLOCK, RNN_BLOCK), RNN_BLOCK ** -0.5),
        "odd_b_gate_x": nrm(ks[18], (N_ODD, RNN_HEADS, RNN_BLOCK), 0.02),
        "odd_lam": lam,
        "odd_w_out": nrm(ks[20], (N_ODD, D_RNN, D_MODEL), BETA * D_RNN ** -0.5),
        "w_router": nrm(ks[21], (D_MODEL, N_EXPERTS), D_MODEL ** -0.5),
        "b_router": nrm(ks[22], (N_EXPERTS,), 0.01),
        "moe_w1": nrm(ks[23], (DEPTH, N_EXPERTS, D_MODEL, D_EXPERT), D_MODEL ** -0.5),
        "moe_w3": nrm(ks[24], (DEPTH, N_EXPERTS, D_MODEL, D_EXPERT), D_MODEL ** -0.5),
        "moe_w2": nrm(ks[25], (DEPTH, N_EXPERTS, D_EXPERT, D_MODEL), BETA * D_EXPERT ** -0.5),
    }


def reference(x, ln1_g, ln1_b, ln2_g, ln2_b, even_w_in, even_a_dw, even_a_dw_b, even_a_ln_g,
              even_a_ln_b, even_b_dw, even_w_out, odd_w_in, odd_c_dw, odd_c_dw_b, odd_w_gate_a,
              odd_b_gate_a, odd_w_gate_x, odd_b_gate_x, odd_lam, odd_w_out, w_router, b_router,
              moe_w1, moe_w3, moe_w2):
    for layer in range(DEPTH):
        j = layer // 2
        if layer % 2 == 0:
            m = conv_mixer(x, even_w_in[j], even_a_dw[j], even_a_dw_b[j], even_a_ln_g[j],
                           even_a_ln_b[j], even_b_dw[j], even_w_out[j])
        else:
            m = rglru_mixer(x, odd_w_in[j], odd_c_dw[j], odd_c_dw_b[j], odd_w_gate_a[j],
                            odd_b_gate_a[j], odd_w_gate_x[j], odd_b_gate_x[j], odd_lam[j],
                            odd_w_out[j])
        x = layer_norm(ALPHA * x + m, ln1_g[layer], ln1_b[layer])
        f = moe(x, w_router, b_router, moe_w1[layer], moe_w3[layer], moe_w2[layer])
        x = layer_norm(ALPHA * x + f, ln2_g[layer], ln2_b[layer])
    return x
```

```python
import functools

import jax
import jax.numpy as jnp
from jax import lax
from jax.experimental import pallas as pl
from jax.experimental.pallas import tpu as pltpu

F32 = jnp.float32
BF16 = jnp.bfloat16
I32 = jnp.int32
LN_EPS = 1e-5
LRU_C = 8.0
SUBLANES = 8
LANES = 128
HALO_A = 32
HALO_B = 8
EXPERTS_PER_GROUP = 4
VMEM_LIMIT = 56 * 1024 * 1024


def _layer_norm(v, g, b):
    mu = jnp.mean(v, axis=-1, keepdims=True)
    c = v - mu
    var = jnp.mean(c * c, axis=-1, keepdims=True)
    return c * lax.rsqrt(var + LN_EPS) * g + b


def _row(v):
    return v.reshape(1, -1).astype(F32)


def _const_spec(shape):
    return pl.BlockSpec(shape, lambda *_: (0,) * len(shape))


def _mix0_kernel(x_ref, win_ref, adw_ref, adwb_ref, alng_ref, alnb_ref, bdw_ref, wout_ref,
                 lng_ref, lnb_ref, o_ref, abuf, ash, cbuf, bgate, cat, *, alpha, ts, d_a, d_b, rc):
    s = pl.program_id(1)

    @pl.when(s == 0)
    def _():
        abuf[0:HALO_A, :] = jnp.zeros((HALO_A, d_a), F32)
        cbuf[0:HALO_B, :] = jnp.zeros((HALO_B, d_b), F32)

    x = x_ref[0]
    xb = x.astype(BF16)
    ua = jnp.dot(xb, win_ref[:, 0:2 * d_a], preferred_element_type=F32)
    abuf[HALO_A:HALO_A + ts, :] = ua[:, :d_a] * jax.nn.sigmoid(ua[:, d_a:])
    ub = jnp.dot(xb, win_ref[:, 2 * d_a:2 * d_a + 3 * d_b], preferred_element_type=F32)
    bgate[...] = ub[:, :d_b]
    cbuf[HALO_B:HALO_B + ts, :] = ub[:, d_b:2 * d_b] * ub[:, 2 * d_b:]

    n_sh = ts + HALO_A - SUBLANES
    for j in range(1, SUBLANES):
        ash[j - 1, :, :] = abuf[j:j + n_sh, :]

    n_a = adw_ref.shape[0]
    n_b = bdw_ref.shape[0]
    for c in range(ts // rc):
        r0 = c * rc
        acc = adwb_ref[...]
        for k in range(n_a):
            q, j = divmod(HALO_A - (n_a - 1) + k, SUBLANES)
            lo = r0 + q * SUBLANES
            src = abuf[lo:lo + rc, :] if j == 0 else ash[j - 1, lo:lo + rc, :]
            acc = acc + adw_ref[k:k + 1, :] * src
        a = _layer_norm(acc, alng_ref[...], alnb_ref[...])
        cat[r0:r0 + rc, 0:d_a] = (a * jax.nn.sigmoid(a)).astype(BF16)
        bb = jnp.zeros((rc, d_b), F32)
        for k in range(n_b):
            lo = r0 + HALO_B - (n_b - 1) + k
            bb = bb + bdw_ref[k:k + 1, :] * cbuf[lo:lo + rc, :]
        cat[r0:r0 + rc, d_a:d_a + d_b] = (bgate[r0:r0 + rc, :] * bb).astype(BF16)

    abuf[0:HALO_A, :] = abuf[ts:ts + HALO_A, :]
    cbuf[0:HALO_B, :] = cbuf[ts:ts + HALO_B, :]

    m = jnp.dot(cat[...], wout_ref[...], preferred_element_type=F32)
    o_ref[0] = _layer_norm(alpha * x + m, lng_ref[...], lnb_ref[...])


def _mix0(x, w_in, a_dw, a_dw_b, a_ln_g, a_ln_b, b_dw, w_out, ln_g, ln_b, *, alpha, ts=512, rc=32):
    bsz, seq, d = x.shape
    d_a = a_dw.shape[-1]
    d_b = b_dw.shape[-1]
    assert a_dw.shape[0] - 1 <= HALO_A and b_dw.shape[0] - 1 <= HALO_B
    ts = min(ts, seq)
    rc = min(rc, ts)
    assert seq % ts == 0 and ts % rc == 0
    kern = functools.partial(_mix0_kernel, alpha=alpha, ts=ts, d_a=d_a, d_b=d_b, rc=rc)
    return pl.pallas_call(
        kern,
        grid=(bsz, seq // ts),
        in_specs=[
            pl.BlockSpec((1, ts, d), lambda b, s: (b, s, 0)),
            _const_spec(w_in.shape), _const_spec(a_dw.shape), _const_spec((1, d_a)), _const_spec((1, d_a)),
            _const_spec((1, d_a)), _const_spec(b_dw.shape), _const_spec(w_out.shape),
            _const_spec((1, d)), _const_spec((1, d)),
        ],
        out_specs=pl.BlockSpec((1, ts, d), lambda b, s: (b, s, 0)),
        out_shape=jax.ShapeDtypeStruct(x.shape, F32),
        scratch_shapes=[
            pltpu.VMEM((HALO_A + ts, d_a), F32),
            pltpu.VMEM((SUBLANES - 1, HALO_A + ts - SUBLANES, d_a), F32),
            pltpu.VMEM((HALO_B + ts, d_b), F32),
            pltpu.VMEM((ts, d_b), F32),
            pltpu.VMEM((ts, d_a + d_b), BF16),
        ],
        compiler_params=pltpu.CompilerParams(
            dimension_semantics=("arbitrary", "arbitrary"), vmem_limit_bytes=VMEM_LIMIT),
        name="mix0",
    )(x, w_in.astype(BF16), a_dw.astype(F32), _row(a_dw_b), _row(a_ln_g), _row(a_ln_b),
      b_dw.astype(F32), w_out.astype(BF16), _row(ln_g), _row(ln_b))


def _mix1_kernel(x_ref, win_ref, cdw_ref, cdwb_ref, wga_ref, bga_ref, wgx_ref, bgx_ref, lam_ref, wout_ref,
                 lng_ref, lnb_ref, o_ref, cbuf, ybr, abuf, bbuf, hcar, *, alpha, ts, d_rnn, rc):
    s = pl.program_id(1)

    @pl.when(s == 0)
    def _():
        cbuf[0:HALO_B, :] = jnp.zeros((HALO_B, d_rnn), F32)
        hcar[...] = jnp.zeros((1, d_rnn), F32)

    x = x_ref[0]
    xb = x.astype(BF16)
    ybr[...] = jnp.dot(xb, win_ref[:, 0:d_rnn], preferred_element_type=F32)
    cbuf[HALO_B:HALO_B + ts, :] = jnp.dot(xb, win_ref[:, d_rnn:2 * d_rnn], preferred_element_type=F32)

    n_heads, blk = wga_ref.shape[0], wga_ref.shape[1]
    n_c = cdw_ref.shape[0]
    neg_c_sp = -LRU_C * jax.nn.softplus(-lam_ref[...])
    for c in range(ts // rc):
        r0 = c * rc
        xr = cdwb_ref[...]
        for k in range(n_c):
            lo = r0 + HALO_B - (n_c - 1) + k
            xr = xr + cdw_ref[k:k + 1, :] * cbuf[lo:lo + rc, :]
        xrb = xr.astype(BF16)
        for h in range(n_heads):
            cs = slice(h * blk, (h + 1) * blk)
            r = jax.nn.sigmoid(jnp.dot(xrb[:, cs], wga_ref[h], preferred_element_type=F32) + bga_ref[:, cs])
            i = jax.nn.sigmoid(jnp.dot(xrb[:, cs], wgx_ref[h], preferred_element_type=F32) + bgx_ref[:, cs])
            log_a = neg_c_sp[:, cs] * r
            abuf[r0:r0 + rc, cs] = jnp.exp(log_a)
            th = jnp.tanh(log_a)
            bbuf[r0:r0 + rc, cs] = jnp.sqrt(-2.0 * th / (1.0 - th)) * (i * xr[:, cs])

    cbuf[0:HALO_B, :] = cbuf[ts:ts + HALO_B, :]

    rowid = lax.broadcasted_iota(I32, (SUBLANES, d_rnn), 0)

    def group(g, h_prev):
        r = pl.multiple_of(g * SUBLANES, SUBLANES)
        a = abuf[pl.ds(r, SUBLANES), :]
        b = bbuf[pl.ds(r, SUBLANES), :]
        step = 1
        while step < SUBLANES:
            keep = rowid >= step
            b = jnp.where(keep, a * pltpu.roll(b, step, axis=0) + b, b)
            a = jnp.where(keep, a * pltpu.roll(a, step, axis=0), a)
            step *= 2
        h = a * h_prev + b
        bbuf[pl.ds(r, SUBLANES), :] = h
        return h[SUBLANES - 1:SUBLANES, :]

    hcar[...] = lax.fori_loop(0, ts // SUBLANES, group, hcar[...], unroll=2)

    out = (jax.nn.gelu(ybr[...]) * bbuf[...]).astype(BF16)
    m = jnp.dot(out, wout_ref[...], preferred_element_type=F32)
    o_ref[0] = _layer_norm(alpha * x + m, lng_ref[...], lnb_ref[...])


def _mix1(x, w_in, c_dw, c_dw_b, w_gate_a, b_gate_a, w_gate_x, b_gate_x, lam, w_out, ln_g, ln_b,
          *, alpha, ts=512, rc=64):
    bsz, seq, d = x.shape
    d_rnn = c_dw.shape[-1]
    assert c_dw.shape[0] - 1 <= HALO_B
    ts = min(ts, seq)
    rc = min(rc, ts)
    assert seq % ts == 0 and ts % rc == 0 and ts % SUBLANES == 0
    kern = functools.partial(_mix1_kernel, alpha=alpha, ts=ts, d_rnn=d_rnn, rc=rc)
    return pl.pallas_call(
        kern,
        grid=(bsz, seq // ts),
        in_specs=[
            pl.BlockSpec((1, ts, d), lambda b, s: (b, s, 0)),
            _const_spec(w_in.shape), _const_spec(c_dw.shape), _const_spec((1, d_rnn)),
            _const_spec(w_gate_a.shape), _const_spec((1, d_rnn)),
            _const_spec(w_gate_x.shape), _const_spec((1, d_rnn)),
            _const_spec((1, d_rnn)), _const_spec(w_out.shape), _const_spec((1, d)), _const_spec((1, d)),
        ],
        out_specs=pl.BlockSpec((1, ts, d), lambda b, s: (b, s, 0)),
        out_shape=jax.ShapeDtypeStruct(x.shape, F32),
        scratch_shapes=[
            pltpu.VMEM((HALO_B + ts, d_rnn), F32),
            pltpu.VMEM((ts, d_rnn), F32),
            pltpu.VMEM((ts, d_rnn), F32),
            pltpu.VMEM((ts, d_rnn), F32),
            pltpu.VMEM((1, d_rnn), F32),
        ],
        compiler_params=pltpu.CompilerParams(
            dimension_semantics=("arbitrary", "arbitrary"), vmem_limit_bytes=VMEM_LIMIT),
        name="mix1",
    )(x, w_in.astype(BF16), c_dw.astype(F32), _row(c_dw_b), w_gate_a.astype(BF16), _row(b_gate_a),
      w_gate_x.astype(BF16), _row(b_gate_x), _row(lam), w_out.astype(BF16), _row(ln_g), _row(ln_b))


def _first_argmax(vals):
    best, idx = vals[0], jnp.zeros(vals[0].shape, I32)
    for j in range(1, len(vals)):
        upd = vals[j] > best
        best = jnp.where(upd, vals[j], best)
        idx = jnp.where(upd, j, idx)
    return best, idx


def _route_kernel(x_ref, wrt_ref, br_ref, idx_ref, gate_ref, cnt_ref, tri, base, *, tt, n_exp):
    step = pl.program_id(0)

    @pl.when(step == 0)
    def _():
        src = lax.broadcasted_iota(I32, (tt, tt), 0)
        dst = lax.broadcasted_iota(I32, (tt, tt), 1)
        tri[...] = jnp.where(src <= dst, 1.0, 0.0).astype(BF16)
        base[...] = jnp.zeros(base.shape, F32)

    logits = lax.dot_general(wrt_ref[...], x_ref[...], (((1,), (1,)), ((), ())),
                             precision=lax.Precision.HIGHEST, preferred_element_type=F32)
    logits = logits + br_ref[:, 0:1]
    e = jnp.exp(logits - jnp.max(logits, axis=0, keepdims=True))
    p = e / jnp.sum(e, axis=0, keepdims=True)
    rows = [p[j:j + 1, :] for j in range(n_exp)]

    n_grp = n_exp // EXPERTS_PER_GROUP
    scores = []
    for g in range(n_grp):
        a, b, c, d = rows[EXPERTS_PER_GROUP * g:EXPERTS_PER_GROUP * (g + 1)]
        scores.append(jnp.maximum(jnp.maximum(a, b) + jnp.maximum(c, d), jnp.maximum(a + b, c + d)))
    _, g_sel = _first_argmax(scores)

    p_in = []
    for j in range(EXPERTS_PER_GROUP):
        v = rows[j]
        for g in range(1, n_grp):
            v = jnp.where(g_sel == g, rows[EXPERTS_PER_GROUP * g + j], v)
        p_in.append(v)
    v1, i1 = _first_argmax(p_in)
    v2, i2 = _first_argmax([jnp.where(i1 == j, -1.0, p_in[j]) for j in range(EXPERTS_PER_GROUP)])
    den = v1 + v2
    e1 = g_sel * EXPERTS_PER_GROUP + i1
    e2 = g_sel * EXPERTS_PER_GROUP + i2

    eid = lax.broadcasted_iota(I32, (n_exp, tt), 0)
    hit1 = eid == e1
    hit2 = eid == e2
    member = jnp.where(hit1 | hit2, 1.0, 0.0)
    incl = jnp.dot(member.astype(BF16), tri[...], preferred_element_type=F32)
    before = base[:, 0:1] + incl - member
    r1 = jnp.sum(jnp.where(hit1, before, 0.0), axis=0, keepdims=True)
    r2 = jnp.sum(jnp.where(hit2, before, 0.0), axis=0, keepdims=True)
    base[...] = base[...] + jnp.sum(member, axis=1, keepdims=True)

    idx_ref[0:1, :] = e1
    idx_ref[1:2, :] = e2
    idx_ref[2:3, :] = r1.astype(I32)
    idx_ref[3:4, :] = r2.astype(I32)
    gate_ref[0:1, :] = v1 / den
    gate_ref[1:2, :] = v2 / den
    cnt_ref[...] = base[...].astype(I32)


def _route(xf, w_router, b_router, *, tt=1024):
    t, d = xf.shape
    n_exp = w_router.shape[1]
    tt = min(tt, t)
    assert t % tt == 0 and n_exp % EXPERTS_PER_GROUP == 0
    kern = functools.partial(_route_kernel, tt=tt, n_exp=n_exp)
    return pl.pallas_call(
        kern,
        grid=(t // tt,),
        in_specs=[
            pl.BlockSpec((tt, d), lambda i: (i, 0)),
            _const_spec((n_exp, d)), _const_spec((n_exp, LANES)),
        ],
        out_specs=[
            pl.BlockSpec((4, tt), lambda i: (0, i)),
            pl.BlockSpec((2, tt), lambda i: (0, i)),
            _const_spec((n_exp, LANES)),
        ],
        out_shape=[
            jax.ShapeDtypeStruct((4, t), I32),
            jax.ShapeDtypeStruct((2, t), F32),
            jax.ShapeDtypeStruct((n_exp, LANES), I32),
        ],
        scratch_shapes=[pltpu.VMEM((tt, tt), BF16), pltpu.VMEM((n_exp, LANES), F32)],
        compiler_params=pltpu.CompilerParams(
            dimension_semantics=("arbitrary",), vmem_limit_bytes=VMEM_LIMIT),
        name="route",
    )(xf, w_router.T.astype(F32), jnp.broadcast_to(b_router.astype(F32)[:, None], (n_exp, LANES)))


def _dispatch_kernel(pstart_ref, idx_ref, x_hbm, init_hbm, xs_hbm, sem, *, td):
    del init_hbm
    base = pl.program_id(0) * td

    def issue(t, carry):
        d0 = pstart_ref[idx_ref[0, t]] + idx_ref[2, t]
        d1 = pstart_ref[idx_ref[1, t]] + idx_ref[3, t]
        pltpu.make_async_copy(x_hbm.at[pl.ds(base + t, 1)], xs_hbm.at[pl.ds(d0, 1)], sem).start()
        pltpu.make_async_copy(x_hbm.at[pl.ds(base + t, 1)], xs_hbm.at[pl.ds(d1, 1)], sem).start()
        return carry

    lax.fori_loop(0, td, issue, 0, unroll=8)
    pltpu.make_async_copy(x_hbm.at[pl.ds(0, 2 * td)], xs_hbm.at[pl.ds(0, 2 * td)], sem).wait()


def _dispatch(xf, idx, pstart, n_pad, *, td=512):
    t, d = xf.shape
    td = min(td, t)
    assert t % td == 0
    grid_spec = pltpu.PrefetchScalarGridSpec(
        num_scalar_prefetch=1,
        grid=(t // td,),
        in_specs=[
            pl.BlockSpec((4, td), lambda i, ps: (0, i), memory_space=pltpu.SMEM),
            pl.BlockSpec(memory_space=pl.ANY),
            pl.BlockSpec(memory_space=pl.ANY),
        ],
        out_specs=pl.BlockSpec(memory_space=pl.ANY),
        scratch_shapes=[pltpu.SemaphoreType.DMA(())],
    )
    return pl.pallas_call(
        functools.partial(_dispatch_kernel, td=td),
        grid_spec=grid_spec,
        out_shape=jax.ShapeDtypeStruct((n_pad, d), xf.dtype),
        input_output_aliases={3: 0},
        compiler_params=pltpu.CompilerParams(dimension_semantics=("arbitrary",)),
        name="dispatch",
    )(pstart, idx, xf, jnp.zeros((n_pad, d), xf.dtype))


def _experts_kernel(blk_e_ref, n_used_ref, xs_ref, w1_ref, w3_ref, w2_ref, y_ref):
    del blk_e_ref
    used = pl.program_id(0) < n_used_ref[0]

    @pl.when(used)
    def _():
        xb = xs_ref[...].astype(BF16)
        h1 = jnp.dot(xb, w1_ref[0], preferred_element_type=F32)
        h3 = jnp.dot(xb, w3_ref[0], preferred_element_type=F32)
        hb = (h1 * jax.nn.sigmoid(h1) * h3).astype(BF16)
        y_ref[...] = jnp.dot(hb, w2_ref[0], preferred_element_type=F32)

    @pl.when(jnp.logical_not(used))
    def _():
        y_ref[...] = jnp.zeros(y_ref.shape, F32)


def _experts(xs, blk_e, n_used, w1, w3, w2, *, bm):
    n_pad, d = xs.shape
    d_e = w1.shape[-1]
    row_blk = lambda i, be, nu: (jnp.minimum(i, nu[0] - 1), 0)
    grid_spec = pltpu.PrefetchScalarGridSpec(
        num_scalar_prefetch=2,
        grid=(n_pad // bm,),
        in_specs=[
            pl.BlockSpec((bm, d), row_blk),
            pl.BlockSpec((1, d, d_e), lambda i, be, nu: (be[i], 0, 0)),
            pl.BlockSpec((1, d, d_e), lambda i, be, nu: (be[i], 0, 0)),
            pl.BlockSpec((1, d_e, d), lambda i, be, nu: (be[i], 0, 0)),
        ],
        out_specs=pl.BlockSpec((bm, d), lambda i, be, nu: (i, 0)),
    )
    return pl.pallas_call(
        _experts_kernel,
        grid_spec=grid_spec,
        out_shape=jax.ShapeDtypeStruct((n_pad, d), F32),
        compiler_params=pltpu.CompilerParams(
            dimension_semantics=("arbitrary",), vmem_limit_bytes=VMEM_LIMIT),
        name="experts",
    )(blk_e, n_used, xs, w1, w3, w2)


def _combine_kernel(pstart_ref, idx_ref, x_ref, gate_ref, lng_ref, lnb_ref, y_hbm, o_ref, ybuf, sem, *, alpha, tc):
    def issue(t, carry):
        d0 = pstart_ref[idx_ref[0, t]] + idx_ref[2, t]
        d1 = pstart_ref[idx_ref[1, t]] + idx_ref[3, t]
        pltpu.make_async_copy(y_hbm.at[pl.ds(d0, 1)], ybuf.at[0, pl.ds(t, 1)], sem).start()
        pltpu.make_async_copy(y_hbm.at[pl.ds(d1, 1)], ybuf.at[1, pl.ds(t, 1)], sem).start()
        return carry

    lax.fori_loop(0, tc, issue, 0, unroll=8)
    pltpu.make_async_copy(y_hbm.at[pl.ds(0, 2 * tc)], ybuf.reshape(2 * tc, ybuf.shape[-1]), sem).wait()
    f = gate_ref[:, 0:1] * ybuf[0] + gate_ref[:, 1:2] * ybuf[1]
    o_ref[...] = _layer_norm(alpha * x_ref[...] + f, lng_ref[...], lnb_ref[...])


def _combine(xf, y, idx, gates_col, pstart, ln_g, ln_b, *, alpha, tc=256):
    t, d = xf.shape
    tc = min(tc, t)
    assert t % tc == 0
    grid_spec = pltpu.PrefetchScalarGridSpec(
        num_scalar_prefetch=1,
        grid=(t // tc,),
        in_specs=[
            pl.BlockSpec((4, tc), lambda i, ps: (0, i), memory_space=pltpu.SMEM),
            pl.BlockSpec((tc, d), lambda i, ps: (i, 0)),
            pl.BlockSpec((tc, 2), lambda i, ps: (i, 0)),
            pl.BlockSpec((1, d), lambda i, ps: (0, 0)),
            pl.BlockSpec((1, d), lambda i, ps: (0, 0)),
            pl.BlockSpec(memory_space=pl.ANY),
        ],
        out_specs=pl.BlockSpec((tc, d), lambda i, ps: (i, 0)),
        scratch_shapes=[pltpu.VMEM((2, tc, d), F32), pltpu.SemaphoreType.DMA(())],
    )
    return pl.pallas_call(
        functools.partial(_combine_kernel, alpha=alpha, tc=tc),
        grid_spec=grid_spec,
        out_shape=jax.ShapeDtypeStruct((t, d), F32),
        compiler_params=pltpu.CompilerParams(
            dimension_semantics=("arbitrary",), vmem_limit_bytes=VMEM_LIMIT),
        name="combine",
    )(pstart, idx, xf, gates_col, _row(ln_g), _row(ln_b), y)


def _moe_ln(x, w_router, b_router, w1, w3, w2, ln_g, ln_b, *, alpha, bm=256):
    bsz, seq, d = x.shape
    t = bsz * seq
    n_exp = w_router.shape[1]
    xf = x.reshape(t, d)
    idx, gates, cnt = _route(xf, w_router, b_router)
    counts = cnt[:, 0]
    padded = (counts + bm - 1) // bm * bm
    pend = jnp.cumsum(padded)
    pstart = (pend - padded).astype(I32)
    n_blk = -(-(2 * t) // bm) + n_exp
    n_used = (pend[-1] // bm).astype(I32)
    blk = jnp.minimum(jnp.arange(n_blk, dtype=I32), n_used - 1)
    blk_e = jnp.minimum(jnp.searchsorted(pend, blk * bm, side="right"), n_exp - 1).astype(I32)
    xs = _dispatch(xf, idx, pstart, n_blk * bm)
    y = _experts(xs, blk_e, n_used.reshape(1), w1.astype(BF16), w3.astype(BF16), w2.astype(BF16), bm=bm)
    out = _combine(xf, y, idx, gates.T, pstart, ln_g, ln_b, alpha=alpha)
    return out.reshape(bsz, seq, d)


def kernel(x, ln1_g, ln1_b, ln2_g, ln2_b, even_w_in, even_a_dw, even_a_dw_b, even_a_ln_g, even_a_ln_b, even_b_dw, even_w_out, odd_w_in, odd_c_dw, odd_c_dw_b, odd_w_gate_a, odd_b_gate_a, odd_w_gate_x, odd_b_gate_x, odd_lam, odd_w_out, w_router, b_router, moe_w1, moe_w3, moe_w2):
    depth = ln1_g.shape[0]
    alpha = (2.0 * depth) ** 0.25
    for layer in range(depth):
        j = layer // 2
        if layer % 2 == 0:
            x = _mix0(x, even_w_in[j], even_a_dw[j], even_a_dw_b[j], even_a_ln_g[j], even_a_ln_b[j],
                      even_b_dw[j], even_w_out[j], ln1_g[layer], ln1_b[layer], alpha=alpha)
        else:
            x = _mix1(x, odd_w_in[j], odd_c_dw[j], odd_c_dw_b[j], odd_w_gate_a[j], odd_b_gate_a[j],
                      odd_w_gate_x[j], odd_b_gate_x[j], odd_lam[j], odd_w_out[j], ln1_g[layer], ln1_b[layer],
                      alpha=alpha)
        x = _moe_ln(x, w_router, b_router, moe_w1[layer], moe_w3[layer], moe_w2[layer],
                    ln2_g[layer], ln2_b[layer], alpha=alpha)
    return x
```

```python
import functools

import jax
import jax.numpy as jnp
from jax import lax
from jax.experimental import pallas as pl
from jax.experimental.pallas import tpu as pltpu

F32 = jnp.float32
BF16 = jnp.bfloat16
I32 = jnp.int32
LN_EPS = 1e-5
LRU_C = 8.0
SUBLANES = 8
LANES = 128
HALO_A = 32
HALO_B = 8
EXPERTS_PER_GROUP = 4
VMEM_LIMIT = 56 * 1024 * 1024


def _layer_norm(v, g, b):
    mu = jnp.mean(v, axis=-1, keepdims=True)
    c = v - mu
    var = jnp.mean(c * c, axis=-1, keepdims=True)
    return c * lax.rsqrt(var + LN_EPS) * g + b


def _row(v):
    return v.reshape(1, -1).astype(F32)


def _const_spec(shape):
    return pl.BlockSpec(shape, lambda *_: (0,) * len(shape))


def _mix0_kernel(x_ref, win_ref, adw_ref, adwb_ref, alng_ref, alnb_ref, bdw_ref, wout_ref,
                 lng_ref, lnb_ref, o_ref, abuf, ash, cbuf, bgate, cat, *, alpha, ts, d_a, d_b, rc):
    s = pl.program_id(1)

    @pl.when(s == 0)
    def _():
        abuf[0:HALO_A, :] = jnp.zeros((HALO_A, d_a), F32)
        cbuf[0:HALO_B, :] = jnp.zeros((HALO_B, d_b), F32)

    x = x_ref[0]
    xb = x.astype(BF16)
    ua = jnp.dot(xb, win_ref[:, 0:2 * d_a], preferred_element_type=F32)
    abuf[HALO_A:HALO_A + ts, :] = ua[:, :d_a] * jax.nn.sigmoid(ua[:, d_a:])
    ub = jnp.dot(xb, win_ref[:, 2 * d_a:2 * d_a + 3 * d_b], preferred_element_type=F32)
    bgate[...] = ub[:, :d_b]
    cbuf[HALO_B:HALO_B + ts, :] = ub[:, d_b:2 * d_b] * ub[:, 2 * d_b:]

    n_sh = ts + HALO_A - SUBLANES
    for j in range(1, SUBLANES):
        ash[j - 1, :, :] = abuf[j:j + n_sh, :]

    n_a = adw_ref.shape[0]
    n_b = bdw_ref.shape[0]
    for c in range(ts // rc):
        r0 = c * rc
        acc = adwb_ref[...]
        for k in range(n_a):
            q, j = divmod(HALO_A - (n_a - 1) + k, SUBLANES)
            lo = r0 + q * SUBLANES
            src = abuf[lo:lo + rc, :] if j == 0 else ash[j - 1, lo:lo + rc, :]
            acc = acc + adw_ref[k:k + 1, :] * src
        a = _layer_norm(acc, alng_ref[...], alnb_ref[...])
        cat[r0:r0 + rc, 0:d_a] = (a * jax.nn.sigmoid(a)).astype(BF16)
        bb = jnp.zeros((rc, d_b), F32)
        for k in range(n_b):
            lo = r0 + HALO_B - (n_b - 1) + k
            bb = bb + bdw_ref[k:k + 1, :] * cbuf[lo:lo + rc, :]
        cat[r0:r0 + rc, d_a:d_a + d_b] = (bgate[r0:r0 + rc, :] * bb).astype(BF16)

    abuf[0:HALO_A, :] = abuf[ts:ts + HALO_A, :]
    cbuf[0:HALO_B, :] = cbuf[ts:ts + HALO_B, :]

    m = jnp.dot(cat[...], wout_ref[...], preferred_element_type=F32)
    o_ref[0] = _layer_norm(alpha * x + m, lng_ref[...], lnb_ref[...])


def _mix0(x, w_in, a_dw, a_dw_b, a_ln_g, a_ln_b, b_dw, w_out, ln_g, ln_b, *, alpha, ts=512, rc=32):
    bsz, seq, d = x.shape
    d_a = a_dw.shape[-1]
    d_b = b_dw.shape[-1]
    assert a_dw.shape[0] - 1 <= HALO_A and b_dw.shape[0] - 1 <= HALO_B
    ts = min(ts, seq)
    rc = min(rc, ts)
    assert seq % ts == 0 and ts % rc == 0
    kern = functools.partial(_mix0_kernel, alpha=alpha, ts=ts, d_a=d_a, d_b=d_b, rc=rc)
    return pl.pallas_call(
        kern,
        grid=(bsz, seq // ts),
        in_specs=[
            pl.BlockSpec((1, ts, d), lambda b, s: (b, s, 0)),
            _const_spec(w_in.shape), _const_spec(a_dw.shape), _const_spec((1, d_a)), _const_spec((1, d_a)),
            _const_spec((1, d_a)), _const_spec(b_dw.shape), _const_spec(w_out.shape),
            _const_spec((1, d)), _const_spec((1, d)),
        ],
        out_specs=pl.BlockSpec((1, ts, d), lambda b, s: (b, s, 0)),
        out_shape=jax.ShapeDtypeStruct(x.shape, F32),
        scratch_shapes=[
            pltpu.VMEM((HALO_A + ts, d_a), F32),
            pltpu.VMEM((SUBLANES - 1, HALO_A + ts - SUBLANES, d_a), F32),
            pltpu.VMEM((HALO_B + ts, d_b), F32),
            pltpu.VMEM((ts, d_b), F32),
            pltpu.VMEM((ts, d_a + d_b), BF16),
        ],
        compiler_params=pltpu.CompilerParams(
            dimension_semantics=("arbitrary", "arbitrary"), vmem_limit_bytes=VMEM_LIMIT),
        name="mix0",
    )(x, w_in.astype(BF16), a_dw.astype(F32), _row(a_dw_b), _row(a_ln_g), _row(a_ln_b),
      b_dw.astype(F32), w_out.astype(BF16), _row(ln_g), _row(ln_b))


def _mix1_kernel(x_ref, win_ref, cdw_ref, cdwb_ref, wga_ref, bga_ref, wgx_ref, bgx_ref, lam_ref, wout_ref,
                 lng_ref, lnb_ref, o_ref, cbuf, ybr, abuf, bbuf, hcar, *, alpha, ts, d_rnn, rc):
    s = pl.program_id(1)

    @pl.when(s == 0)
    def _():
        cbuf[0:HALO_B, :] = jnp.zeros((HALO_B, d_rnn), F32)
        hcar[...] = jnp.zeros((1, d_rnn), F32)

    x = x_ref[0]
    xb = x.astype(BF16)
    ybr[...] = jnp.dot(xb, win_ref[:, 0:d_rnn], preferred_element_type=F32)
    cbuf[HALO_B:HALO_B + ts, :] = jnp.dot(xb, win_ref[:, d_rnn:2 * d_rnn], preferred_element_type=F32)

    n_heads, blk = wga_ref.shape[0], wga_ref.shape[1]
    n_c = cdw_ref.shape[0]
    neg_c_sp = -LRU_C * jax.nn.softplus(-lam_ref[...])
    for c in range(ts // rc):
        r0 = c * rc
        xr = cdwb_ref[...]
        for k in range(n_c):
            lo = r0 + HALO_B - (n_c - 1) + k
            xr = xr + cdw_ref[k:k + 1, :] * cbuf[lo:lo + rc, :]
        xrb = xr.astype(BF16)
        for h in range(n_heads):
            cs = slice(h * blk, (h + 1) * blk)
            r = jax.nn.sigmoid(jnp.dot(xrb[:, cs], wga_ref[h], preferred_element_type=F32) + bga_ref[:, cs])
            i = jax.nn.sigmoid(jnp.dot(xrb[:, cs], wgx_ref[h], preferred_element_type=F32) + bgx_ref[:, cs])
            log_a = neg_c_sp[:, cs] * r
            abuf[r0:r0 + rc, cs] = jnp.exp(log_a)
            th = jnp.tanh(log_a)
            bbuf[r0:r0 + rc, cs] = jnp.sqrt(-2.0 * th / (1.0 - th)) * (i * xr[:, cs])

    cbuf[0:HALO_B, :] = cbuf[ts:ts + HALO_B, :]

    rowid = lax.broadcasted_iota(I32, (SUBLANES, d_rnn), 0)

    def group(g, h_prev):
        r = pl.multiple_of(g * SUBLANES, SUBLANES)
        a = abuf[pl.ds(r, SUBLANES), :]
        b = bbuf[pl.ds(r, SUBLANES), :]
        step = 1
        while step < SUBLANES:
            keep = rowid >= step
            b = jnp.where(keep, a * pltpu.roll(b, step, axis=0) + b, b)
            a = jnp.where(keep, a * pltpu.roll(a, step, axis=0), a)
            step *= 2
        h = a * h_prev + b
        bbuf[pl.ds(r, SUBLANES), :] = h
        return h[SUBLANES - 1:SUBLANES, :]

    hcar[...] = lax.fori_loop(0, ts // SUBLANES, group, hcar[...], unroll=2)

    out = (jax.nn.gelu(ybr[...]) * bbuf[...]).astype(BF16)
    m = jnp.dot(out, wout_ref[...], preferred_element_type=F32)
    o_ref[0] = _layer_norm(alpha * x + m, lng_ref[...], lnb_ref[...])


def _mix1(x, w_in, c_dw, c_dw_b, w_gate_a, b_gate_a, w_gate_x, b_gate_x, lam, w_out, ln_g, ln_b,
          *, alpha, ts=512, rc=64):
    bsz, seq, d = x.shape
    d_rnn = c_dw.shape[-1]
    assert c_dw.shape[0] - 1 <= HALO_B
    ts = min(ts, seq)
    rc = min(rc, ts)
    assert seq % ts == 0 and ts % rc == 0 and ts % SUBLANES == 0
    kern = functools.partial(_mix1_kernel, alpha=alpha, ts=ts, d_rnn=d_rnn, rc=rc)
    return pl.pallas_call(
        kern,
        grid=(bsz, seq // ts),
        in_specs=[
            pl.BlockSpec((1, ts, d), lambda b, s: (b, s, 0)),
            _const_spec(w_in.shape), _const_spec(c_dw.shape), _const_spec((1, d_rnn)),
            _const_spec(w_gate_a.shape), _const_spec((1, d_rnn)),
            _const_spec(w_gate_x.shape), _const_spec((1, d_rnn)),
            _const_spec((1, d_rnn)), _const_spec(w_out.shape), _const_spec((1, d)), _const_spec((1, d)),
        ],
        out_specs=pl.BlockSpec((1, ts, d), lambda b, s: (b, s, 0)),
        out_shape=jax.ShapeDtypeStruct(x.shape, F32),
        scratch_shapes=[
            pltpu.VMEM((HALO_B + ts, d_rnn), F32),
            pltpu.VMEM((ts, d_rnn), F32),
            pltpu.VMEM((ts, d_rnn), F32),
            pltpu.VMEM((ts, d_rnn), F32),
            pltpu.VMEM((1, d_rnn), F32),
        ],
        compiler_params=pltpu.CompilerParams(
            dimension_semantics=("arbitrary", "arbitrary"), vmem_limit_bytes=VMEM_LIMIT),
        name="mix1",
    )(x, w_in.astype(BF16), c_dw.astype(F32), _row(c_dw_b), w_gate_a.astype(BF16), _row(b_gate_a),
      w_gate_x.astype(BF16), _row(b_gate_x), _row(lam), w_out.astype(BF16), _row(ln_g), _row(ln_b))


def _first_argmax(vals):
    best, idx = vals[0], jnp.zeros(vals[0].shape, I32)
    for j in range(1, len(vals)):
        upd = vals[j] > best
        best = jnp.where(upd, vals[j], best)
        idx = jnp.where(upd, j, idx)
    return best, idx


def _route_kernel(x_ref, wrt_ref, br_ref, slot_ref, gate_ref, tab_ref, tri, base, *, tt, n_exp):
    step = pl.program_id(0)

    @pl.when(step == 0)
    def _():
        src = lax.broadcasted_iota(I32, (tt, tt), 0)
        dst = lax.broadcasted_iota(I32, (tt, tt), 1)
        tri[...] = jnp.where(src <= dst, 1.0, 0.0).astype(BF16)
        base[...] = jnp.zeros(base.shape, F32)

    logits = lax.dot_general(wrt_ref[...], x_ref[...], (((1,), (1,)), ((), ())),
                             precision=lax.Precision.HIGHEST, preferred_element_type=F32)
    logits = logits + br_ref[:, 0:1]
    e = jnp.exp(logits - jnp.max(logits, axis=0, keepdims=True))
    p = e / jnp.sum(e, axis=0, keepdims=True)
    rows = [p[j:j + 1, :] for j in range(n_exp)]

    n_grp = n_exp // EXPERTS_PER_GROUP
    scores = []
    for g in range(n_grp):
        a, b, c, d = rows[EXPERTS_PER_GROUP * g:EXPERTS_PER_GROUP * (g + 1)]
        scores.append(jnp.maximum(jnp.maximum(a, b) + jnp.maximum(c, d), jnp.maximum(a + b, c + d)))
    _, g_sel = _first_argmax(scores)

    p_in = []
    for j in range(EXPERTS_PER_GROUP):
        v = rows[j]
        for g in range(1, n_grp):
            v = jnp.where(g_sel == g, rows[EXPERTS_PER_GROUP * g + j], v)
        p_in.append(v)
    v1, i1 = _first_argmax(p_in)
    v2, i2 = _first_argmax([jnp.where(i1 == j, -1.0, p_in[j]) for j in range(EXPERTS_PER_GROUP)])
    den = v1 + v2
    e1 = g_sel * EXPERTS_PER_GROUP + i1
    e2 = g_sel * EXPERTS_PER_GROUP + i2

    eid = lax.broadcasted_iota(I32, (n_exp, tt), 0)
    hit1 = eid == e1
    hit2 = eid == e2
    member = jnp.where(hit1 | hit2, 1.0, 0.0)
    incl = jnp.dot(member.astype(BF16), tri[...], preferred_element_type=F32)
    count = jnp.sum(member, axis=1, keepdims=True)
    lower = (lax.broadcasted_iota(I32, (n_exp, n_exp), 1) < lax.broadcasted_iota(I32, (n_exp, n_exp), 0))
    offset = jnp.dot(jnp.where(lower, 1.0, 0.0), jnp.broadcast_to(count, (n_exp, LANES)),
                     precision=lax.Precision.HIGHEST, preferred_element_type=F32)[:, 0:1]
    place = offset + incl - member
    s1 = jnp.sum(jnp.where(hit1, place, 0.0), axis=0, keepdims=True)
    s2 = jnp.sum(jnp.where(hit2, place, 0.0), axis=0, keepdims=True)

    slot_ref[0:1, :] = s1.astype(I32)
    slot_ref[1:2, :] = s2.astype(I32)
    gate_ref[0:1, :] = v1 / den
    gate_ref[1:2, :] = v2 / den
    lane = lax.broadcasted_iota(I32, (n_exp, LANES), 1)
    tab_ref[0] = jnp.where(lane == 0, base[...], jnp.where(lane == 1, count, 0.0)).astype(I32)
    base[...] = base[...] + count


def _route(xf, w_router, b_router, *, tt):
    t, d = xf.shape
    n_exp = w_router.shape[1]
    assert t % tt == 0 and n_exp % EXPERTS_PER_GROUP == 0
    kern = functools.partial(_route_kernel, tt=tt, n_exp=n_exp)
    return pl.pallas_call(
        kern,
        grid=(t // tt,),
        in_specs=[
            pl.BlockSpec((tt, d), lambda i: (i, 0)),
            _const_spec((n_exp, d)), _const_spec((n_exp, LANES)),
        ],
        out_specs=[
            pl.BlockSpec((2, tt), lambda i: (0, i)),
            pl.BlockSpec((2, tt), lambda i: (0, i)),
            pl.BlockSpec((1, n_exp, LANES), lambda i: (i, 0, 0)),
        ],
        out_shape=[
            jax.ShapeDtypeStruct((2, t), I32),
            jax.ShapeDtypeStruct((2, t), F32),
            jax.ShapeDtypeStruct((t // tt, n_exp, LANES), I32),
        ],
        scratch_shapes=[pltpu.VMEM((tt, tt), BF16), pltpu.VMEM((n_exp, LANES), F32)],
        compiler_params=pltpu.CompilerParams(
            dimension_semantics=("arbitrary",), vmem_limit_bytes=VMEM_LIMIT),
        name="route",
    )(xf, w_router.T.astype(F32), jnp.broadcast_to(b_router.astype(F32)[:, None], (n_exp, LANES)))


def _segment_copies(src_of, dst_of, length, max_len, sem, *, wait=False):
    done = 0
    for bit in reversed(range(max_len.bit_length())):
        n = 1 << bit
        piece = length & n

        @pl.when(piece != 0)
        def _(done=done, n=n):
            copy = pltpu.make_async_copy(src_of(done, n), dst_of(done, n), sem)
            copy.wait() if wait else copy.start()

        done = done + piece


def _token_rows(ref, lo, n):
    return ref.at[pl.ds(pl.multiple_of(lo * SUBLANES, SUBLANES), n * SUBLANES)]


def _store_token_major(ref, val):
    for j in range(SUBLANES):
        ref[pl.ds(j, val.shape[0], stride=SUBLANES), :] = val[:, j * LANES:(j + 1) * LANES]


def _load_token_major(ref, rows):
    return jnp.concatenate([ref[pl.ds(j, rows, stride=SUBLANES), :] for j in range(SUBLANES)], axis=1)


def _dispatch_kernel(seg_dst_ref, seg_len_ref, pad_dst_ref, pad_len_ref, n_used_ref, x_ref, slot_ref, xs_hbm,
                     grouped, zeros, sems, pad_sem, *, tm, n_exp, bm):
    i = pl.program_id(0)
    n_tiles = pl.num_programs(0)
    buf = lax.rem(i, 2)

    def wait_tile(b):
        pltpu.make_async_copy(grouped.at[b], _token_rows(xs_hbm, 0, 2 * tm), sems.at[b]).wait()

    def pad_copies(wait):
        for e in range(n_exp):
            _segment_copies(lambda lo, n: _token_rows(zeros, lo, n),
                            lambda lo, n, e=e: _token_rows(xs_hbm, pad_dst_ref[e] + lo, n),
                            pad_len_ref[e], bm - 1, pad_sem, wait=wait)
        n_blk = xs_hbm.shape[0] // (bm * SUBLANES)
        for blk in range(n_blk - n_exp, n_blk):
            @pl.when(blk >= n_used_ref[0])
            def _(blk=blk):
                copy = pltpu.make_async_copy(zeros, _token_rows(xs_hbm, blk * bm, bm), pad_sem)
                copy.wait() if wait else copy.start()

    @pl.when(i == 0)
    def _():
        zeros[...] = jnp.zeros(zeros.shape, F32)
        pad_copies(wait=False)

    @pl.when(i >= 2)
    def _():
        wait_tile(buf)

    row = lax.broadcasted_iota(I32, (2 * tm, tm), 0)
    onehot = jnp.where((row == slot_ref[0:1, :]) | (row == slot_ref[1:2, :]), 1.0, 0.0).astype(BF16)
    _store_token_major(grouped.at[buf], jnp.dot(onehot, x_ref[...].astype(BF16), preferred_element_type=F32))

    start = 0
    for e in range(n_exp):
        length = seg_len_ref[i * n_exp + e]
        dst = seg_dst_ref[i * n_exp + e]
        _segment_copies(lambda lo, n, start=start: _token_rows(grouped.at[buf], start + lo, n),
                        lambda lo, n, dst=dst: _token_rows(xs_hbm, dst + lo, n),
                        length, tm, sems.at[buf])
        start = start + length

    @pl.when(i == n_tiles - 1)
    def _():
        wait_tile(buf)

        @pl.when(n_tiles > 1)
        def _():
            wait_tile(1 - buf)

        pad_copies(wait=True)


def _dispatch(xf, slots, seg_dst, seg_len, pad_dst, pad_len, n_used, n_pad, *, tm, bm):
    t, d = xf.shape
    assert d == SUBLANES * LANES
    n_exp = pad_dst.shape[0]
    grid_spec = pltpu.PrefetchScalarGridSpec(
        num_scalar_prefetch=5,
        grid=(t // tm,),
        in_specs=[
            pl.BlockSpec((tm, d), lambda i, *_: (i, 0)),
            pl.BlockSpec((2, tm), lambda i, *_: (0, i)),
        ],
        out_specs=pl.BlockSpec(memory_space=pl.ANY),
        scratch_shapes=[
            pltpu.VMEM((2, 2 * tm * SUBLANES, LANES), F32),
            pltpu.VMEM((bm * SUBLANES, LANES), F32),
            pltpu.SemaphoreType.DMA((2,)),
            pltpu.SemaphoreType.DMA(()),
        ],
    )
    return pl.pallas_call(
        functools.partial(_dispatch_kernel, tm=tm, n_exp=n_exp, bm=bm),
        grid_spec=grid_spec,
        out_shape=jax.ShapeDtypeStruct((n_pad * SUBLANES, LANES), F32),
        compiler_params=pltpu.CompilerParams(
            dimension_semantics=("arbitrary",), vmem_limit_bytes=VMEM_LIMIT),
        name="dispatch",
    )(seg_dst, seg_len, pad_dst, pad_len, n_used, xf, slots)


def _experts_kernel(blk_e_ref, n_used_ref, xs_ref, w1_ref, w3_ref, w2_ref, y_ref, w1b, w3b, w2b, *, bm):
    i = pl.program_id(0)
    used = i < n_used_ref[0]

    @pl.when((i == 0) | (blk_e_ref[i] != blk_e_ref[jnp.maximum(i - 1, 0)]))
    def _():
        w1b[...] = w1_ref[0, 0].astype(BF16)
        w3b[...] = w3_ref[0, 0].astype(BF16)
        w2b[...] = w2_ref[0, 0].astype(BF16)

    @pl.when(used)
    def _():
        xb = _load_token_major(xs_ref, bm).astype(BF16)
        h1 = jnp.dot(xb, w1b[...], preferred_element_type=F32)
        h3 = jnp.dot(xb, w3b[...], preferred_element_type=F32)
        hb = (h1 * jax.nn.sigmoid(h1) * h3).astype(BF16)
        _store_token_major(y_ref, jnp.dot(hb, w2b[...], preferred_element_type=F32))

    @pl.when(jnp.logical_not(used))
    def _():
        y_ref[...] = jnp.zeros(y_ref.shape, F32)


def _experts(xs, blk_e, n_used, w1, w3, w2, layer, *, bm):
    d, d_e = w1.shape[-2:]
    assert d == SUBLANES * LANES
    n_blk = xs.shape[0] // (bm * SUBLANES)
    row_blk = lambda i, be, nu: (jnp.minimum(i, nu[0] - 1), 0)
    w_blk = lambda i, be, nu: (layer, be[i], 0, 0)
    grid_spec = pltpu.PrefetchScalarGridSpec(
        num_scalar_prefetch=2,
        grid=(n_blk,),
        in_specs=[
            pl.BlockSpec((bm * SUBLANES, LANES), row_blk),
            pl.BlockSpec((1, 1, d, d_e), w_blk),
            pl.BlockSpec((1, 1, d, d_e), w_blk),
            pl.BlockSpec((1, 1, d_e, d), w_blk),
        ],
        out_specs=pl.BlockSpec((bm * SUBLANES, LANES), lambda i, be, nu: (i, 0)),
        scratch_shapes=[pltpu.VMEM((d, d_e), BF16), pltpu.VMEM((d, d_e), BF16), pltpu.VMEM((d_e, d), BF16)],
    )
    return pl.pallas_call(
        functools.partial(_experts_kernel, bm=bm),
        grid_spec=grid_spec,
        out_shape=jax.ShapeDtypeStruct(xs.shape, F32),
        compiler_params=pltpu.CompilerParams(
            dimension_semantics=("arbitrary",), vmem_limit_bytes=VMEM_LIMIT),
        name="experts",
    )(blk_e, n_used, xs, w1, w3, w2)


def _combine_kernel(seg_src_ref, seg_len_ref, x_ref, slot_ref, gate_ref, lng_ref, lnb_ref, y_hbm, o_ref,
                    grouped, sems, *, alpha, tm, n_exp):
    i = pl.program_id(0)
    n_tiles = pl.num_programs(0)
    buf = lax.rem(i, 2)

    def fetch(tile, b):
        start = 0
        for e in range(n_exp):
            length = seg_len_ref[tile * n_exp + e]
            src = seg_src_ref[tile * n_exp + e]
            _segment_copies(lambda lo, n, src=src: _token_rows(y_hbm, src + lo, n),
                            lambda lo, n, start=start: _token_rows(grouped.at[b], start + lo, n),
                            length, tm, sems.at[b])
            start = start + length

    @pl.when(i == 0)
    def _():
        fetch(0, 0)

    @pl.when(i + 1 < n_tiles)
    def _():
        fetch(i + 1, 1 - buf)

    pltpu.make_async_copy(_token_rows(y_hbm, 0, 2 * tm), grouped.at[buf], sems.at[buf]).wait()

    row = lax.broadcasted_iota(I32, (2 * tm, tm), 0)
    g = (jnp.where(row == slot_ref[0:1, :], gate_ref[0:1, :], 0.0)
         + jnp.where(row == slot_ref[1:2, :], gate_ref[1:2, :], 0.0))
    g_hi = g.astype(BF16)
    g_lo = (g - g_hi.astype(F32)).astype(BF16)
    yb = _load_token_major(grouped.at[buf], 2 * tm).astype(BF16)
    over_rows = (((0,), (0,)), ((), ()))
    f = (lax.dot_general(g_hi, yb, over_rows, preferred_element_type=F32)
         + lax.dot_general(g_lo, yb, over_rows, preferred_element_type=F32))
    o_ref[...] = _layer_norm(alpha * x_ref[...] + f, lng_ref[...], lnb_ref[...])


def _combine(xf, y, slots, gates, seg_src, seg_len, ln_g, ln_b, *, alpha, tm, n_exp):
    t, d = xf.shape
    grid_spec = pltpu.PrefetchScalarGridSpec(
        num_scalar_prefetch=2,
        grid=(t // tm,),
        in_specs=[
            pl.BlockSpec((tm, d), lambda i, *_: (i, 0)),
            pl.BlockSpec((2, tm), lambda i, *_: (0, i)),
            pl.BlockSpec((2, tm), lambda i, *_: (0, i)),
            pl.BlockSpec((1, d), lambda i, *_: (0, 0)),
            pl.BlockSpec((1, d), lambda i, *_: (0, 0)),
            pl.BlockSpec(memory_space=pl.ANY),
        ],
        out_specs=pl.BlockSpec((tm, d), lambda i, *_: (i, 0)),
        scratch_shapes=[pltpu.VMEM((2, 2 * tm * SUBLANES, LANES), F32), pltpu.SemaphoreType.DMA((2,))],
    )
    return pl.pallas_call(
        functools.partial(_combine_kernel, alpha=alpha, tm=tm, n_exp=n_exp),
        grid_spec=grid_spec,
        out_shape=jax.ShapeDtypeStruct((t, d), F32),
        compiler_params=pltpu.CompilerParams(
            dimension_semantics=("arbitrary",), vmem_limit_bytes=VMEM_LIMIT),
        name="combine",
    )(seg_src, seg_len, xf, slots, gates, _row(ln_g), _row(ln_b), y)


def _moe_ln(x, w_router, b_router, w1, w3, w2, layer, ln_g, ln_b, *, alpha, tm=256, bm=256):
    bsz, seq, d = x.shape
    t = bsz * seq
    n_exp = w_router.shape[1]
    tm = min(tm, t)
    assert t % tm == 0
    xf = x.reshape(t, d)
    slots, gates, tab = _route(xf, w_router, b_router, tt=tm)
    before, in_tile = tab[:, :, 0], tab[:, :, 1]
    counts = before[-1] + in_tile[-1]
    padded = (counts + bm - 1) // bm * bm
    pend = jnp.cumsum(padded)
    pstart = pend - padded
    n_blk = -(-(2 * t) // bm) + n_exp
    n_used = (pend[-1] // bm).reshape(1).astype(I32)
    blk = jnp.minimum(jnp.arange(n_blk, dtype=I32), n_used - 1)
    blk_e = jnp.minimum(jnp.sum(pend[None, :] <= (blk * bm)[:, None], axis=1), n_exp - 1).astype(I32)
    seg_pos = (pstart[None, :] + before).reshape(-1).astype(I32)
    seg_len = in_tile.reshape(-1).astype(I32)
    xs = _dispatch(xf, slots, seg_pos, seg_len, (pstart + counts).astype(I32), (padded - counts).astype(I32),
                   n_used, n_blk * bm, tm=tm, bm=bm)
    y = _experts(xs, blk_e, n_used, w1, w3, w2, layer, bm=bm)
    out = _combine(xf, y, slots, gates, seg_pos, seg_len, ln_g, ln_b, alpha=alpha, tm=tm, n_exp=n_exp)
    return out.reshape(bsz, seq, d)


def kernel(x, ln1_g, ln1_b, ln2_g, ln2_b, even_w_in, even_a_dw, even_a_dw_b, even_a_ln_g, even_a_ln_b, even_b_dw, even_w_out, odd_w_in, odd_c_dw, odd_c_dw_b, odd_w_gate_a, odd_b_gate_a, odd_w_gate_x, odd_b_gate_x, odd_lam, odd_w_out, w_router, b_router, moe_w1, moe_w3, moe_w2):
    depth = ln1_g.shape[0]
    alpha = (2.0 * depth) ** 0.25
    for layer in range(depth):
        j = layer // 2
        if layer % 2 == 0:
            x = _mix0(x, even_w_in[j], even_a_dw[j], even_a_dw_b[j], even_a_ln_g[j], even_a_ln_b[j],
                      even_b_dw[j], even_w_out[j], ln1_g[layer], ln1_b[layer], alpha=alpha)
        else:
            x = _mix1(x, odd_w_in[j], odd_c_dw[j], odd_c_dw_b[j], odd_w_gate_a[j], odd_b_gate_a[j],
                      odd_w_gate_x[j], odd_b_gate_x[j], odd_lam[j], odd_w_out[j], ln1_g[layer], ln1_b[layer],
                      alpha=alpha)
        x = _moe_ln(x, w_router, b_router, moe_w1, moe_w3, moe_w2, layer,
                    ln2_g[layer], ln2_b[layer], alpha=alpha)
    return x
```

```python
import functools

import jax
import jax.numpy as jnp
from jax import lax
from jax.experimental import pallas as pl
from jax.experimental.pallas import tpu as pltpu

F32 = jnp.float32
BF16 = jnp.bfloat16
I32 = jnp.int32
LN_EPS = 1e-5
LRU_C = 8.0
SUBLANES = 8
LANES = 128
HALO_A = 32
HALO_B = 8
EXPERTS_PER_GROUP = 4
VMEM_LIMIT = 56 * 1024 * 1024


def _layer_norm(v, g, b):
    mu = jnp.mean(v, axis=-1, keepdims=True)
    c = v - mu
    var = jnp.mean(c * c, axis=-1, keepdims=True)
    return c * lax.rsqrt(var + LN_EPS) * g + b


def _row(v):
    return v.reshape(1, -1).astype(F32)


def _const_spec(shape):
    return pl.BlockSpec(shape, lambda *_: (0,) * len(shape))


def _mix0_kernel(x_ref, win_ref, adw_ref, adwb_ref, alng_ref, alnb_ref, bdw_ref, wout_ref, lng_ref, lnb_ref,
                 wrt_ref, br_ref, o_ref, slot_ref, gate_ref, tab_ref,
                 abuf, ash, cbuf, bgate, cat, tri, base, *, alpha, ts, tm, d_a, d_b, rc):
    first = (pl.program_id(0) == 0) & (pl.program_id(1) == 0)
    n_a = adw_ref.shape[0]
    n_b = bdw_ref.shape[0]

    @pl.when(pl.program_id(1) == 0)
    def _():
        abuf[0:HALO_A, :] = jnp.zeros((HALO_A, d_a), F32)
        cbuf[0:HALO_B, :] = jnp.zeros((HALO_B, d_b), F32)

    x = x_ref[0]
    xb = x.astype(BF16)
    ua = jnp.dot(xb, win_ref[:, 0:2 * d_a], preferred_element_type=F32)
    abuf[HALO_A:HALO_A + ts, :] = ua[:, :d_a] * jax.nn.sigmoid(ua[:, d_a:])
    ub = jnp.dot(xb, win_ref[:, 2 * d_a:2 * d_a + 3 * d_b], preferred_element_type=F32)
    bgate[...] = ub[:, :d_b]
    cbuf[HALO_B:HALO_B + ts, :] = ub[:, d_b:2 * d_b] * ub[:, 2 * d_b:]

    n_sh = ts + HALO_A - SUBLANES
    for j in range(1, SUBLANES):
        ash[j - 1, :, :] = abuf[j:j + n_sh, :]

    for c in range(ts // rc):
        r0 = c * rc
        acc = adwb_ref[...]
        for k in range(n_a):
            q, j = divmod(HALO_A - (n_a - 1) + k, SUBLANES)
            lo = r0 + q * SUBLANES
            src = abuf[lo:lo + rc, :] if j == 0 else ash[j - 1, lo:lo + rc, :]
            acc = acc + adw_ref[k:k + 1, :] * src
        a = _layer_norm(acc, alng_ref[...], alnb_ref[...])
        cat[r0:r0 + rc, 0:d_a] = (a * jax.nn.sigmoid(a)).astype(BF16)
        bb = jnp.zeros((rc, d_b), F32)
        for k in range(n_b):
            lo = r0 + HALO_B - (n_b - 1) + k
            bb = bb + bdw_ref[k:k + 1, :] * cbuf[lo:lo + rc, :]
        cat[r0:r0 + rc, d_a:d_a + d_b] = (bgate[r0:r0 + rc, :] * bb).astype(BF16)

    abuf[0:HALO_A, :] = abuf[ts:ts + HALO_A, :]
    cbuf[0:HALO_B, :] = cbuf[ts:ts + HALO_B, :]

    m = jnp.dot(cat[...], wout_ref[...], preferred_element_type=F32)
    _ln_and_route(alpha * x + m, lng_ref, lnb_ref, wrt_ref, br_ref, o_ref, slot_ref, gate_ref, tab_ref,
                  tri, base, first, tm=tm)


def _mix0(x, w_in, a_dw, a_dw_b, a_ln_g, a_ln_b, b_dw, w_out, ln_g, ln_b, w_router, b_router,
          *, alpha, ts=512, tm=256, rc=32):
    bsz, seq, d = x.shape
    d_a = a_dw.shape[-1]
    d_b = b_dw.shape[-1]
    assert a_dw.shape[0] - 1 <= HALO_A and b_dw.shape[0] - 1 <= HALO_B
    ts = min(ts, seq)
    rc = min(rc, ts)
    assert seq % ts == 0 and ts % rc == 0 and rc % (2 * SUBLANES) == 0
    r_in, r_args, r_out_specs, r_out_shape, r_scratch, tm = _router_io(w_router, b_router, bsz, seq, ts, tm)
    kern = functools.partial(_mix0_kernel, alpha=alpha, ts=ts, tm=tm, d_a=d_a, d_b=d_b, rc=rc)
    return pl.pallas_call(
        kern,
        grid=(bsz, seq // ts),
        in_specs=[
            pl.BlockSpec((1, ts, d), lambda b, s: (b, s, 0)),
            _const_spec(w_in.shape), _const_spec(a_dw.shape), _const_spec((1, d_a)), _const_spec((1, d_a)),
            _const_spec((1, d_a)), _const_spec(b_dw.shape), _const_spec(w_out.shape),
            _const_spec((1, d)), _const_spec((1, d)),
        ] + r_in,
        out_specs=[pl.BlockSpec((1, ts, d), lambda b, s: (b, s, 0))] + r_out_specs,
        out_shape=[jax.ShapeDtypeStruct(x.shape, F32)] + r_out_shape,
        scratch_shapes=[
            pltpu.VMEM((HALO_A + ts, d_a), F32),
            pltpu.VMEM((SUBLANES - 1, HALO_A + ts - SUBLANES, d_a), F32),
            pltpu.VMEM((HALO_B + ts, d_b), F32),
            pltpu.VMEM((ts, d_b), F32),
            pltpu.VMEM((ts, d_a + d_b), BF16),
        ] + r_scratch,
        compiler_params=pltpu.CompilerParams(
            dimension_semantics=("arbitrary", "arbitrary"), vmem_limit_bytes=VMEM_LIMIT),
        name="mix0",
    )(x, w_in.astype(BF16), a_dw.astype(F32), _row(a_dw_b), _row(a_ln_g), _row(a_ln_b),
      b_dw.astype(F32), w_out.astype(BF16), _row(ln_g), _row(ln_b), *r_args)


def _mix1_kernel(x_ref, win_ref, cdw_ref, cdwb_ref, wga_ref, bga_ref, wgx_ref, bgx_ref, lam_ref, wout_ref,
                 lng_ref, lnb_ref, wrt_ref, br_ref, o_ref, slot_ref, gate_ref, tab_ref,
                 cbuf, ybr, abuf, bbuf, hcar, tri, base, *, alpha, ts, tm, d_rnn, rc):
    first = (pl.program_id(0) == 0) & (pl.program_id(1) == 0)

    @pl.when(pl.program_id(1) == 0)
    def _():
        cbuf[0:HALO_B, :] = jnp.zeros((HALO_B, d_rnn), F32)
        hcar[...] = jnp.zeros((1, d_rnn), F32)

    x = x_ref[0]
    xb = x.astype(BF16)
    ybr[...] = jnp.dot(xb, win_ref[:, 0:d_rnn], preferred_element_type=F32)
    cbuf[HALO_B:HALO_B + ts, :] = jnp.dot(xb, win_ref[:, d_rnn:2 * d_rnn], preferred_element_type=F32)

    n_heads, blk = wga_ref.shape[0], wga_ref.shape[1]
    n_c = cdw_ref.shape[0]
    neg_c_sp = -LRU_C * jax.nn.softplus(-lam_ref[...])
    for c in range(ts // rc):
        r0 = c * rc
        xr = cdwb_ref[...]
        for k in range(n_c):
            lo = r0 + HALO_B - (n_c - 1) + k
            xr = xr + cdw_ref[k:k + 1, :] * cbuf[lo:lo + rc, :]
        xrb = xr.astype(BF16)
        for h in range(n_heads):
            cs = slice(h * blk, (h + 1) * blk)
            r = jax.nn.sigmoid(jnp.dot(xrb[:, cs], wga_ref[h], preferred_element_type=F32) + bga_ref[:, cs])
            i = jax.nn.sigmoid(jnp.dot(xrb[:, cs], wgx_ref[h], preferred_element_type=F32) + bgx_ref[:, cs])
            log_a = neg_c_sp[:, cs] * r
            abuf[r0:r0 + rc, cs] = jnp.exp(log_a)
            th = jnp.tanh(log_a)
            bbuf[r0:r0 + rc, cs] = jnp.sqrt(-2.0 * th / (1.0 - th)) * (i * xr[:, cs])

    cbuf[0:HALO_B, :] = cbuf[ts:ts + HALO_B, :]

    rowid = lax.broadcasted_iota(I32, (SUBLANES, d_rnn), 0)

    def group(g, h_prev):
        r = pl.multiple_of(g * SUBLANES, SUBLANES)
        a = abuf[pl.ds(r, SUBLANES), :]
        b = bbuf[pl.ds(r, SUBLANES), :]
        step = 1
        while step < SUBLANES:
            keep = rowid >= step
            b = jnp.where(keep, a * pltpu.roll(b, step, axis=0) + b, b)
            a = jnp.where(keep, a * pltpu.roll(a, step, axis=0), a)
            step *= 2
        h = a * h_prev + b
        bbuf[pl.ds(r, SUBLANES), :] = h
        return h[SUBLANES - 1:SUBLANES, :]

    hcar[...] = lax.fori_loop(0, ts // SUBLANES, group, hcar[...], unroll=2)

    out = (jax.nn.gelu(ybr[...]) * bbuf[...]).astype(BF16)
    m = jnp.dot(out, wout_ref[...], preferred_element_type=F32)
    _ln_and_route(alpha * x + m, lng_ref, lnb_ref, wrt_ref, br_ref, o_ref, slot_ref, gate_ref, tab_ref,
                  tri, base, first, tm=tm)


def _mix1(x, w_in, c_dw, c_dw_b, w_gate_a, b_gate_a, w_gate_x, b_gate_x, lam, w_out, ln_g, ln_b,
          w_router, b_router, *, alpha, ts=512, tm=256, rc=64):
    bsz, seq, d = x.shape
    d_rnn = c_dw.shape[-1]
    assert c_dw.shape[0] - 1 <= HALO_B
    ts = min(ts, seq)
    rc = min(rc, ts)
    assert seq % ts == 0 and ts % rc == 0 and ts % SUBLANES == 0
    r_in, r_args, r_out_specs, r_out_shape, r_scratch, tm = _router_io(w_router, b_router, bsz, seq, ts, tm)
    kern = functools.partial(_mix1_kernel, alpha=alpha, ts=ts, tm=tm, d_rnn=d_rnn, rc=rc)
    return pl.pallas_call(
        kern,
        grid=(bsz, seq // ts),
        in_specs=[
            pl.BlockSpec((1, ts, d), lambda b, s: (b, s, 0)),
            _const_spec(w_in.shape), _const_spec(c_dw.shape), _const_spec((1, d_rnn)),
            _const_spec(w_gate_a.shape), _const_spec((1, d_rnn)),
            _const_spec(w_gate_x.shape), _const_spec((1, d_rnn)),
            _const_spec((1, d_rnn)), _const_spec(w_out.shape), _const_spec((1, d)), _const_spec((1, d)),
        ] + r_in,
        out_specs=[pl.BlockSpec((1, ts, d), lambda b, s: (b, s, 0))] + r_out_specs,
        out_shape=[jax.ShapeDtypeStruct(x.shape, F32)] + r_out_shape,
        scratch_shapes=[
            pltpu.VMEM((HALO_B + ts, d_rnn), F32),
            pltpu.VMEM((ts, d_rnn), F32),
            pltpu.VMEM((ts, d_rnn), F32),
            pltpu.VMEM((ts, d_rnn), F32),
            pltpu.VMEM((1, d_rnn), F32),
        ] + r_scratch,
        compiler_params=pltpu.CompilerParams(
            dimension_semantics=("arbitrary", "arbitrary"), vmem_limit_bytes=VMEM_LIMIT),
        name="mix1",
    )(x, w_in.astype(BF16), c_dw.astype(F32), _row(c_dw_b), w_gate_a.astype(BF16), _row(b_gate_a),
      w_gate_x.astype(BF16), _row(b_gate_x), _row(lam), w_out.astype(BF16), _row(ln_g), _row(ln_b), *r_args)


def _first_argmax(vals):
    best, idx = vals[0], jnp.zeros(vals[0].shape, I32)
    for j in range(1, len(vals)):
        upd = vals[j] > best
        best = jnp.where(upd, vals[j], best)
        idx = jnp.where(upd, j, idx)
    return best, idx


def _route_tile(x, wrt_ref, br_ref, tri, base, tm):
    n_exp = wrt_ref.shape[0]
    tt = x.shape[0]
    w = wrt_ref[...]
    w_hi = w.astype(BF16)
    w_lo = (w - w_hi.astype(F32)).astype(BF16)
    x_hi = x.astype(BF16)
    x_lo = (x - x_hi.astype(F32)).astype(BF16)
    over_d = (((1,), (1,)), ((), ()))
    logits = (lax.dot_general(w_hi, x_hi, over_d, preferred_element_type=F32)
              + lax.dot_general(w_hi, x_lo, over_d, preferred_element_type=F32)
              + lax.dot_general(w_lo, x_hi, over_d, preferred_element_type=F32))
    logits = logits + br_ref[:, 0:1]
    e = jnp.exp(logits - jnp.max(logits, axis=0, keepdims=True))
    p = e / jnp.sum(e, axis=0, keepdims=True)
    rows = [p[j:j + 1, :] for j in range(n_exp)]

    n_grp = n_exp // EXPERTS_PER_GROUP
    scores = []
    for g in range(n_grp):
        a, b, c, d = rows[EXPERTS_PER_GROUP * g:EXPERTS_PER_GROUP * (g + 1)]
        scores.append(jnp.maximum(jnp.maximum(a, b) + jnp.maximum(c, d), jnp.maximum(a + b, c + d)))
    _, g_sel = _first_argmax(scores)

    p_in = []
    for j in range(EXPERTS_PER_GROUP):
        v = rows[j]
        for g in range(1, n_grp):
            v = jnp.where(g_sel == g, rows[EXPERTS_PER_GROUP * g + j], v)
        p_in.append(v)
    v1, i1 = _first_argmax(p_in)
    v2, i2 = _first_argmax([jnp.where(i1 == j, -1.0, p_in[j]) for j in range(EXPERTS_PER_GROUP)])
    den = v1 + v2
    e1 = g_sel * EXPERTS_PER_GROUP + i1
    e2 = g_sel * EXPERTS_PER_GROUP + i2

    eid = lax.broadcasted_iota(I32, (n_exp, tt), 0)
    hit1 = eid == e1
    hit2 = eid == e2
    member = jnp.where(hit1 | hit2, 1.0, 0.0)
    member_b = member.astype(BF16)
    lower = (lax.broadcasted_iota(I32, (n_exp, n_exp), 1) < lax.broadcasted_iota(I32, (n_exp, n_exp), 0))
    lower = jnp.where(lower, 1.0, 0.0)
    lane = lax.broadcasted_iota(I32, (n_exp, LANES), 1)
    place, tabs = [], []
    for h in range(tt // tm):
        cols = slice(h * tm, (h + 1) * tm)
        incl = jnp.dot(member_b[:, cols], tri[...], preferred_element_type=F32)
        count = jnp.sum(member[:, cols], axis=1, keepdims=True)
        offset = jnp.dot(lower, jnp.broadcast_to(count, (n_exp, LANES)),
                         precision=lax.Precision.HIGHEST, preferred_element_type=F32)[:, 0:1]
        place.append(offset + incl - member[:, cols])
        tabs.append(jnp.where(lane == 0, base, jnp.where(lane == 1, count, 0.0)).astype(I32))
        base = base + count
    place = jnp.concatenate(place, axis=1)
    s1 = jnp.sum(jnp.where(hit1, place, 0.0), axis=0, keepdims=True)
    s2 = jnp.sum(jnp.where(hit2, place, 0.0), axis=0, keepdims=True)
    return (s1.astype(I32), s2.astype(I32)), (v1 / den, v2 / den), tabs, base


def _ln_and_route(v, lng_ref, lnb_ref, wrt_ref, br_ref, o_ref, slot_ref, gate_ref, tab_ref, tri, base, first, *, tm):
    @pl.when(first)
    def _():
        src = lax.broadcasted_iota(I32, (tm, tm), 0)
        dst = lax.broadcasted_iota(I32, (tm, tm), 1)
        tri[...] = jnp.where(src <= dst, 1.0, 0.0).astype(BF16)
        base[...] = jnp.zeros(base.shape, F32)

    x1 = _layer_norm(v, lng_ref[...], lnb_ref[...])
    o_ref[0] = x1
    slots, gates, tabs, new_base = _route_tile(x1, wrt_ref, br_ref, tri, base[...], tm)
    for k in range(2):
        slot_ref[k:k + 1, :] = slots[k]
        gate_ref[k:k + 1, :] = gates[k]
    for h, tab in enumerate(tabs):
        tab_ref[h] = tab
    base[...] = new_base


def _router_io(w_router, b_router, bsz, seq, ts, tm):
    d, n_exp = w_router.shape
    tm = min(tm, ts)
    assert ts % tm == 0 and n_exp % EXPERTS_PER_GROUP == 0
    t = bsz * seq
    per_seq = seq // ts
    in_specs = [_const_spec((n_exp, d)), _const_spec((n_exp, LANES))]
    args = (w_router.T.astype(F32), jnp.broadcast_to(b_router.astype(F32)[:, None], (n_exp, LANES)))
    out_specs = [
        pl.BlockSpec((2, ts), lambda b, s: (0, b * per_seq + s)),
        pl.BlockSpec((2, ts), lambda b, s: (0, b * per_seq + s)),
        pl.BlockSpec((ts // tm, n_exp, LANES), lambda b, s: (b * per_seq + s, 0, 0)),
    ]
    out_shape = [
        jax.ShapeDtypeStruct((2, t), I32),
        jax.ShapeDtypeStruct((2, t), F32),
        jax.ShapeDtypeStruct((t // tm, n_exp, LANES), I32),
    ]
    scratch = [pltpu.VMEM((tm, tm), BF16), pltpu.VMEM((n_exp, LANES), F32)]
    return in_specs, args, out_specs, out_shape, scratch, tm


def _segment_copies(src_of, dst_of, length, max_len, sem, *, wait=False):
    done = 0
    for bit in reversed(range(max_len.bit_length())):
        n = 1 << bit
        piece = length & n

        @pl.when(piece != 0)
        def _(done=done, n=n):
            copy = pltpu.make_async_copy(src_of(done, n), dst_of(done, n), sem)
            copy.wait() if wait else copy.start()

        done = done + piece


def _token_rows(ref, lo, n):
    return ref.at[pl.ds(pl.multiple_of(lo * SUBLANES, SUBLANES), n * SUBLANES)]


def _store_token_major(ref, val):
    for j in range(SUBLANES):
        ref[pl.ds(j, val.shape[0], stride=SUBLANES), :] = val[:, j * LANES:(j + 1) * LANES]


def _load_token_major(ref, rows):
    return jnp.concatenate([ref[pl.ds(j, rows, stride=SUBLANES), :] for j in range(SUBLANES)], axis=1)


def _dispatch_kernel(seg_dst_ref, seg_len_ref, pad_dst_ref, pad_len_ref, n_used_ref, x_ref, slot_ref, xs_hbm,
                     grouped, zeros, sems, pad_sem, *, tm, n_exp, bm):
    i = pl.program_id(0)
    n_tiles = pl.num_programs(0)
    buf = lax.rem(i, 2)

    def wait_tile(b):
        pltpu.make_async_copy(grouped.at[b], _token_rows(xs_hbm, 0, 2 * tm), sems.at[b]).wait()

    def pad_copies(wait):
        for e in range(n_exp):
            _segment_copies(lambda lo, n: _token_rows(zeros, lo, n),
                            lambda lo, n, e=e: _token_rows(xs_hbm, pad_dst_ref[e] + lo, n),
                            pad_len_ref[e], bm - 1, pad_sem, wait=wait)
        n_blk = xs_hbm.shape[0] // (bm * SUBLANES)
        for blk in range(n_blk - n_exp, n_blk):
            @pl.when(blk >= n_used_ref[0])
            def _(blk=blk):
                copy = pltpu.make_async_copy(zeros, _token_rows(xs_hbm, blk * bm, bm), pad_sem)
                copy.wait() if wait else copy.start()

    @pl.when(i == 0)
    def _():
        zeros[...] = jnp.zeros(zeros.shape, F32)
        pad_copies(wait=False)

    @pl.when(i >= 2)
    def _():
        wait_tile(buf)

    row = lax.broadcasted_iota(I32, (2 * tm, tm), 0)
    onehot = jnp.where((row == slot_ref[0:1, :]) | (row == slot_ref[1:2, :]), 1.0, 0.0).astype(BF16)
    _store_token_major(grouped.at[buf], jnp.dot(onehot, x_ref[...].astype(BF16), preferred_element_type=F32))

    start = 0
    for e in range(n_exp):
        length = seg_len_ref[i * n_exp + e]
        dst = seg_dst_ref[i * n_exp + e]
        _segment_copies(lambda lo, n, start=start: _token_rows(grouped.at[buf], start + lo, n),
                        lambda lo, n, dst=dst: _token_rows(xs_hbm, dst + lo, n),
                        length, tm, sems.at[buf])
        start = start + length

    @pl.when(i == n_tiles - 1)
    def _():
        wait_tile(buf)

        @pl.when(n_tiles > 1)
        def _():
            wait_tile(1 - buf)

        pad_copies(wait=True)


def _dispatch(xf, slots, seg_dst, seg_len, pad_dst, pad_len, n_used, n_pad, *, tm, bm):
    t, d = xf.shape
    assert d == SUBLANES * LANES
    n_exp = pad_dst.shape[0]
    grid_spec = pltpu.PrefetchScalarGridSpec(
        num_scalar_prefetch=5,
        grid=(t // tm,),
        in_specs=[
            pl.BlockSpec((tm, d), lambda i, *_: (i, 0)),
            pl.BlockSpec((2, tm), lambda i, *_: (0, i)),
        ],
        out_specs=pl.BlockSpec(memory_space=pl.ANY),
        scratch_shapes=[
            pltpu.VMEM((2, 2 * tm * SUBLANES, LANES), F32),
            pltpu.VMEM((bm * SUBLANES, LANES), F32),
            pltpu.SemaphoreType.DMA((2,)),
            pltpu.SemaphoreType.DMA(()),
        ],
    )
    return pl.pallas_call(
        functools.partial(_dispatch_kernel, tm=tm, n_exp=n_exp, bm=bm),
        grid_spec=grid_spec,
        out_shape=jax.ShapeDtypeStruct((n_pad * SUBLANES, LANES), F32),
        compiler_params=pltpu.CompilerParams(
            dimension_semantics=("arbitrary",), vmem_limit_bytes=VMEM_LIMIT),
        name="dispatch",
    )(seg_dst, seg_len, pad_dst, pad_len, n_used, xf, slots)


def _experts_kernel(first_blk_ref, n_blk_ref, n_used_ref, xs_hbm, w1_ref, w3_ref, w2_ref, y_hbm,
                    w1b, w3b, w2b, xbuf, ybuf, in_sems, out_sems, *, bm, n_exp):
    e = pl.program_id(0)
    first = first_blk_ref[e]
    n = n_blk_ref[e]

    def x_copy(j, slot):
        return pltpu.make_async_copy(_token_rows(xs_hbm, (first + j) * bm, bm), xbuf.at[slot], in_sems.at[slot])

    def y_copy(blk, slot):
        return pltpu.make_async_copy(ybuf.at[slot], _token_rows(y_hbm, blk * bm, bm), out_sems.at[slot])

    @pl.when(n > 0)
    def _():
        x_copy(0, 0).start()

    w1b[...] = w1_ref[0, 0].astype(BF16)
    w3b[...] = w3_ref[0, 0].astype(BF16)
    w2b[...] = w2_ref[0, 0].astype(BF16)

    def block(j, carry):
        slot = lax.rem(j, 2)

        @pl.when(j + 1 < n)
        def _():
            x_copy(j + 1, 1 - slot).start()

        x_copy(j, slot).wait()

        @pl.when(j >= 2)
        def _():
            y_copy(first + j - 2, slot).wait()

        xb = _load_token_major(xbuf.at[slot], bm).astype(BF16)
        h1 = jnp.dot(xb, w1b[...], preferred_element_type=F32)
        h3 = jnp.dot(xb, w3b[...], preferred_element_type=F32)
        hb = (h1 * jax.nn.sigmoid(h1) * h3).astype(BF16)
        _store_token_major(ybuf.at[slot], jnp.dot(hb, w2b[...], preferred_element_type=F32))
        y_copy(first + j, slot).start()
        return carry

    lax.fori_loop(0, n, block, 0)

    for back in (2, 1):
        @pl.when(n >= back)
        def _(back=back):
            y_copy(first + n - back, lax.rem(n - back, 2)).wait()

    @pl.when(e == n_exp - 1)
    def _():
        ybuf[0] = jnp.zeros(ybuf.shape[1:], F32)
        n_all = y_hbm.shape[0] // (bm * SUBLANES)
        for blk in range(n_all - n_exp, n_all):
            @pl.when(blk >= n_used_ref[0])
            def _(blk=blk):
                y_copy(blk, 0).start()
                y_copy(blk, 0).wait()


def _experts(xs, first_blk, n_blk, n_used, w1, w3, w2, layer, *, bm):
    n_exp, d, d_e = w1.shape[-3:]
    assert d == SUBLANES * LANES
    w_blk = lambda e, *_: (layer, e, 0, 0)
    grid_spec = pltpu.PrefetchScalarGridSpec(
        num_scalar_prefetch=3,
        grid=(n_exp,),
        in_specs=[
            pl.BlockSpec(memory_space=pl.ANY),
            pl.BlockSpec((1, 1, d, d_e), w_blk),
            pl.BlockSpec((1, 1, d, d_e), w_blk),
            pl.BlockSpec((1, 1, d_e, d), w_blk),
        ],
        out_specs=pl.BlockSpec(memory_space=pl.ANY),
        scratch_shapes=[
            pltpu.VMEM((d, d_e), BF16), pltpu.VMEM((d, d_e), BF16), pltpu.VMEM((d_e, d), BF16),
            pltpu.VMEM((2, bm * SUBLANES, LANES), F32), pltpu.VMEM((2, bm * SUBLANES, LANES), F32),
            pltpu.SemaphoreType.DMA((2,)), pltpu.SemaphoreType.DMA((2,)),
        ],
    )
    return pl.pallas_call(
        functools.partial(_experts_kernel, bm=bm, n_exp=n_exp),
        grid_spec=grid_spec,
        out_shape=jax.ShapeDtypeStruct(xs.shape, F32),
        compiler_params=pltpu.CompilerParams(
            dimension_semantics=("arbitrary",), vmem_limit_bytes=VMEM_LIMIT),
        name="experts",
    )(first_blk, n_blk, n_used, xs, w1, w3, w2)


def _combine_kernel(seg_src_ref, seg_len_ref, x_ref, slot_ref, gate_ref, lng_ref, lnb_ref, y_hbm, o_ref,
                    grouped, sems, *, alpha, tm, n_exp):
    i = pl.program_id(0)
    n_tiles = pl.num_programs(0)
    buf = lax.rem(i, 2)

    def fetch(tile, b):
        start = 0
        for e in range(n_exp):
            length = seg_len_ref[tile * n_exp + e]
            src = seg_src_ref[tile * n_exp + e]
            _segment_copies(lambda lo, n, src=src: _token_rows(y_hbm, src + lo, n),
                            lambda lo, n, start=start: _token_rows(grouped.at[b], start + lo, n),
                            length, tm, sems.at[b])
            start = start + length

    @pl.when(i == 0)
    def _():
        fetch(0, 0)

    @pl.when(i + 1 < n_tiles)
    def _():
        fetch(i + 1, 1 - buf)

    pltpu.make_async_copy(_token_rows(y_hbm, 0, 2 * tm), grouped.at[buf], sems.at[buf]).wait()

    row = lax.broadcasted_iota(I32, (2 * tm, tm), 0)
    g = (jnp.where(row == slot_ref[0:1, :], gate_ref[0:1, :], 0.0)
         + jnp.where(row == slot_ref[1:2, :], gate_ref[1:2, :], 0.0))
    g_hi = g.astype(BF16)
    g_lo = (g - g_hi.astype(F32)).astype(BF16)
    yb = _load_token_major(grouped.at[buf], 2 * tm).astype(BF16)
    over_rows = (((0,), (0,)), ((), ()))
    f = (lax.dot_general(g_hi, yb, over_rows, preferred_element_type=F32)
         + lax.dot_general(g_lo, yb, over_rows, preferred_element_type=F32))
    o_ref[...] = _layer_norm(alpha * x_ref[...] + f, lng_ref[...], lnb_ref[...])


def _combine(xf, y, slots, gates, seg_src, seg_len, ln_g, ln_b, *, alpha, tm, n_exp):
    t, d = xf.shape
    grid_spec = pltpu.PrefetchScalarGridSpec(
        num_scalar_prefetch=2,
        grid=(t // tm,),
        in_specs=[
            pl.BlockSpec((tm, d), lambda i, *_: (i, 0)),
            pl.BlockSpec((2, tm), lambda i, *_: (0, i)),
            pl.BlockSpec((2, tm), lambda i, *_: (0, i)),
            pl.BlockSpec((1, d), lambda i, *_: (0, 0)),
            pl.BlockSpec((1, d), lambda i, *_: (0, 0)),
            pl.BlockSpec(memory_space=pl.ANY),
        ],
        out_specs=pl.BlockSpec((tm, d), lambda i, *_: (i, 0)),
        scratch_shapes=[pltpu.VMEM((2, 2 * tm * SUBLANES, LANES), F32), pltpu.SemaphoreType.DMA((2,))],
    )
    return pl.pallas_call(
        functools.partial(_combine_kernel, alpha=alpha, tm=tm, n_exp=n_exp),
        grid_spec=grid_spec,
        out_shape=jax.ShapeDtypeStruct((t, d), F32),
        compiler_params=pltpu.CompilerParams(
            dimension_semantics=("arbitrary",), vmem_limit_bytes=VMEM_LIMIT),
        name="combine",
    )(seg_src, seg_len, xf, slots, gates, _row(ln_g), _row(ln_b), y)


def _moe_ln(x, slots, gates, tab, w1, w3, w2, layer, ln_g, ln_b, *, alpha, bm=256):
    bsz, seq, d = x.shape
    t = bsz * seq
    n_exp = tab.shape[1]
    tm = t // tab.shape[0]
    xf = x.reshape(t, d)
    before, in_tile = tab[:, :, 0], tab[:, :, 1]
    counts = before[-1] + in_tile[-1]
    padded = (counts + bm - 1) // bm * bm
    pend = jnp.cumsum(padded)
    pstart = pend - padded
    n_blk = -(-(2 * t) // bm) + n_exp
    n_used = (pend[-1] // bm).reshape(1).astype(I32)
    seg_pos = (pstart[None, :] + before).reshape(-1).astype(I32)
    seg_len = in_tile.reshape(-1).astype(I32)
    xs = _dispatch(xf, slots, seg_pos, seg_len, (pstart + counts).astype(I32), (padded - counts).astype(I32),
                   n_used, n_blk * bm, tm=tm, bm=bm)
    y = _experts(xs, (pstart // bm).astype(I32), (padded // bm).astype(I32), n_used, w1, w3, w2, layer, bm=bm)
    out = _combine(xf, y, slots, gates, seg_pos, seg_len, ln_g, ln_b, alpha=alpha, tm=tm, n_exp=n_exp)
    return out.reshape(bsz, seq, d)


def kernel(x, ln1_g, ln1_b, ln2_g, ln2_b, even_w_in, even_a_dw, even_a_dw_b, even_a_ln_g, even_a_ln_b, even_b_dw, even_w_out, odd_w_in, odd_c_dw, odd_c_dw_b, odd_w_gate_a, odd_b_gate_a, odd_w_gate_x, odd_b_gate_x, odd_lam, odd_w_out, w_router, b_router, moe_w1, moe_w3, moe_w2):
    depth = ln1_g.shape[0]
    alpha = (2.0 * depth) ** 0.25
    for layer in range(depth):
        j = layer // 2
        if layer % 2 == 0:
            x, slots, gates, tab = _mix0(
                x, even_w_in[j], even_a_dw[j], even_a_dw_b[j], even_a_ln_g[j], even_a_ln_b[j],
                even_b_dw[j], even_w_out[j], ln1_g[layer], ln1_b[layer], w_router, b_router, alpha=alpha)
        else:
            x, slots, gates, tab = _mix1(
                x, odd_w_in[j], odd_c_dw[j], odd_c_dw_b[j], odd_w_gate_a[j], odd_b_gate_a[j],
                odd_w_gate_x[j], odd_b_gate_x[j], odd_lam[j], odd_w_out[j], ln1_g[layer], ln1_b[layer],
                w_router, b_router, alpha=alpha)
        x = _moe_ln(x, slots, gates, tab, moe_w1, moe_w3, moe_w2, layer,
                    ln2_g[layer], ln2_b[layer], alpha=alpha)
    return x
```

```python
import functools

import jax
import jax.numpy as jnp
from jax import lax
from jax.experimental import pallas as pl
from jax.experimental.pallas import tpu as pltpu

F32 = jnp.float32
BF16 = jnp.bfloat16
I32 = jnp.int32
LN_EPS = 1e-5
LRU_C = 8.0
SUBLANES = 8
LANES = 128
HALO_A = 32
HALO_B = 8
EXPERTS_PER_GROUP = 4
BLOCK_DMA_PRIORITY = 1
VMEM_LIMIT = 56 * 1024 * 1024


def _layer_norm(v, g, b):
    mu = jnp.mean(v, axis=-1, keepdims=True)
    c = v - mu
    var = jnp.mean(c * c, axis=-1, keepdims=True)
    return c * lax.rsqrt(var + LN_EPS) * g + b


def _row(v):
    return v.reshape(1, -1).astype(F32)


def _const_spec(shape):
    return pl.BlockSpec(shape, lambda *_: (0,) * len(shape))


def _mix0_kernel(x_ref, win_ref, adw_ref, adwb_ref, alng_ref, alnb_ref, bdw_ref, wout_ref, lng_ref, lnb_ref,
                 wrt_ref, br_ref, o_ref, slot_ref, gate_ref, tab_ref, *scratch, alpha, ts, tm, d_a, d_b, rc, parts):
    tri, base = scratch[-2:]
    bufs = [scratch[5 * p:5 * p + 5] for p in range(parts)]
    first = (pl.program_id(0) == 0) & (pl.program_id(1) == 0)
    n_a = adw_ref.shape[0]
    n_b = bdw_ref.shape[0]
    hs = ts // parts

    @pl.when(pl.program_id(1) == 0)
    def _():
        bufs[0][0][0:HALO_A, :] = jnp.zeros((HALO_A, d_a), F32)
        bufs[0][2][0:HALO_B, :] = jnp.zeros((HALO_B, d_b), F32)

    x = x_ref[0]
    xb = x.astype(BF16)
    for p, (abuf, ash, cbuf, bgate, cat) in enumerate(bufs):
        rows = slice(p * hs, (p + 1) * hs)
        ua = jnp.dot(xb[rows, :], win_ref[:, 0:2 * d_a], preferred_element_type=F32)
        abuf[HALO_A:HALO_A + hs, :] = ua[:, :d_a] * jax.nn.sigmoid(ua[:, d_a:])
        ub = jnp.dot(xb[rows, :], win_ref[:, 2 * d_a:2 * d_a + 3 * d_b], preferred_element_type=F32)
        bgate[...] = ub[:, :d_b]
        cbuf[HALO_B:HALO_B + hs, :] = ub[:, d_b:2 * d_b] * ub[:, 2 * d_b:]
        if p + 1 < parts:
            bufs[p + 1][0][0:HALO_A, :] = abuf[hs:hs + HALO_A, :]
            bufs[p + 1][2][0:HALO_B, :] = cbuf[hs:hs + HALO_B, :]

    ms = []
    for abuf, ash, cbuf, bgate, cat in bufs:
        n_sh = hs + HALO_A - SUBLANES
        for j in range(1, SUBLANES):
            ash[j - 1, :, :] = abuf[j:j + n_sh, :]
        for c in range(hs // rc):
            r0 = c * rc
            acc = adwb_ref[...]
            for k in range(n_a):
                q, j = divmod(HALO_A - (n_a - 1) + k, SUBLANES)
                lo = r0 + q * SUBLANES
                src = abuf[lo:lo + rc, :] if j == 0 else ash[j - 1, lo:lo + rc, :]
                acc = acc + adw_ref[k:k + 1, :] * src
            a = _layer_norm(acc, alng_ref[...], alnb_ref[...])
            cat[r0:r0 + rc, 0:d_a] = (a * jax.nn.sigmoid(a)).astype(BF16)
            bb = jnp.zeros((rc, d_b), F32)
            for k in range(n_b):
                lo = r0 + HALO_B - (n_b - 1) + k
                bb = bb + bdw_ref[k:k + 1, :] * cbuf[lo:lo + rc, :]
            cat[r0:r0 + rc, d_a:d_a + d_b] = (bgate[r0:r0 + rc, :] * bb).astype(BF16)
        ms.append(jnp.dot(cat[...], wout_ref[...], preferred_element_type=F32))

    bufs[0][0][0:HALO_A, :] = bufs[-1][0][hs:hs + HALO_A, :]
    bufs[0][2][0:HALO_B, :] = bufs[-1][2][hs:hs + HALO_B, :]

    m = jnp.concatenate(ms, axis=0)
    _ln_and_route(alpha * x + m, lng_ref, lnb_ref, wrt_ref, br_ref, o_ref, slot_ref, gate_ref, tab_ref,
                  tri, base, first, tm=tm)


def _mix0(x, w_in, a_dw, a_dw_b, a_ln_g, a_ln_b, b_dw, w_out, ln_g, ln_b, w_router, b_router,
          *, alpha, ts=512, tm=256, rc=32, parts=2):
    bsz, seq, d = x.shape
    d_a = a_dw.shape[-1]
    d_b = b_dw.shape[-1]
    assert a_dw.shape[0] - 1 <= HALO_A and b_dw.shape[0] - 1 <= HALO_B
    ts = min(ts, seq)
    hs = ts // parts
    rc = min(rc, hs)
    assert seq % ts == 0 and ts % parts == 0 and hs % rc == 0 and rc % (2 * SUBLANES) == 0
    r_in, r_args, r_out_specs, r_out_shape, r_scratch, tm = _router_io(w_router, b_router, bsz, seq, ts, tm)
    kern = functools.partial(_mix0_kernel, alpha=alpha, ts=ts, tm=tm, d_a=d_a, d_b=d_b, rc=rc, parts=parts)
    part_scratch = [
        pltpu.VMEM((HALO_A + hs, d_a), F32),
        pltpu.VMEM((SUBLANES - 1, HALO_A + hs - SUBLANES, d_a), F32),
        pltpu.VMEM((HALO_B + hs, d_b), F32),
        pltpu.VMEM((hs, d_b), F32),
        pltpu.VMEM((hs, d_a + d_b), BF16),
    ]
    return pl.pallas_call(
        kern,
        grid=(bsz, seq // ts),
        in_specs=[
            pl.BlockSpec((1, ts, d), lambda b, s: (b, s, 0)),
            _const_spec(w_in.shape), _const_spec(a_dw.shape), _const_spec((1, d_a)), _const_spec((1, d_a)),
            _const_spec((1, d_a)), _const_spec(b_dw.shape), _const_spec(w_out.shape),
            _const_spec((1, d)), _const_spec((1, d)),
        ] + r_in,
        out_specs=[pl.BlockSpec((1, ts, d), lambda b, s: (b, s, 0))] + r_out_specs,
        out_shape=[jax.ShapeDtypeStruct(x.shape, F32)] + r_out_shape,
        scratch_shapes=part_scratch * parts + r_scratch,
        compiler_params=pltpu.CompilerParams(
            dimension_semantics=("arbitrary", "arbitrary"), vmem_limit_bytes=VMEM_LIMIT),
        name="mix0",
    )(x, w_in.astype(BF16), a_dw.astype(F32), _row(a_dw_b), _row(a_ln_g), _row(a_ln_b),
      b_dw.astype(F32), w_out.astype(BF16), _row(ln_g), _row(ln_b), *r_args)


def _mix1_kernel(x_ref, win_ref, cdw_ref, cdwb_ref, wga_ref, bga_ref, wgx_ref, bgx_ref, lam_ref, wout_ref,
                 lng_ref, lnb_ref, wrt_ref, br_ref, o_ref, slot_ref, gate_ref, tab_ref,
                 cbuf, ybr, abuf, bbuf, hcar, tri, base, *, alpha, ts, tm, d_rnn, rc):
    first = (pl.program_id(0) == 0) & (pl.program_id(1) == 0)

    @pl.when(pl.program_id(1) == 0)
    def _():
        cbuf[0:HALO_B, :] = jnp.zeros((HALO_B, d_rnn), F32)
        hcar[...] = jnp.zeros((1, d_rnn), F32)

    x = x_ref[0]
    xb = x.astype(BF16)
    ybr[...] = jnp.dot(xb, win_ref[:, 0:d_rnn], preferred_element_type=F32)
    cbuf[HALO_B:HALO_B + ts, :] = jnp.dot(xb, win_ref[:, d_rnn:2 * d_rnn], preferred_element_type=F32)

    n_heads, blk = wga_ref.shape[0], wga_ref.shape[1]
    n_c = cdw_ref.shape[0]
    neg_c_sp = -LRU_C * jax.nn.softplus(-lam_ref[...])
    for c in range(ts // rc):
        r0 = c * rc
        xr = cdwb_ref[...]
        for k in range(n_c):
            lo = r0 + HALO_B - (n_c - 1) + k
            xr = xr + cdw_ref[k:k + 1, :] * cbuf[lo:lo + rc, :]
        xrb = xr.astype(BF16)
        for h in range(n_heads):
            cs = slice(h * blk, (h + 1) * blk)
            r = jax.nn.sigmoid(jnp.dot(xrb[:, cs], wga_ref[h], preferred_element_type=F32) + bga_ref[:, cs])
            i = jax.nn.sigmoid(jnp.dot(xrb[:, cs], wgx_ref[h], preferred_element_type=F32) + bgx_ref[:, cs])
            log_a = neg_c_sp[:, cs] * r
            abuf[r0:r0 + rc, cs] = jnp.exp(log_a)
            th = jnp.tanh(log_a)
            bbuf[r0:r0 + rc, cs] = jnp.sqrt(-2.0 * th / (1.0 - th)) * (i * xr[:, cs])

    cbuf[0:HALO_B, :] = cbuf[ts:ts + HALO_B, :]

    rowid = lax.broadcasted_iota(I32, (SUBLANES, d_rnn), 0)

    def group(g, h_prev):
        r = pl.multiple_of(g * SUBLANES, SUBLANES)
        a = abuf[pl.ds(r, SUBLANES), :]
        b = bbuf[pl.ds(r, SUBLANES), :]
        step = 1
        while step < SUBLANES:
            keep = rowid >= step
            b = jnp.where(keep, a * pltpu.roll(b, step, axis=0) + b, b)
            a = jnp.where(keep, a * pltpu.roll(a, step, axis=0), a)
            step *= 2
        h = a * h_prev + b
        bbuf[pl.ds(r, SUBLANES), :] = h
        return h[SUBLANES - 1:SUBLANES, :]

    hcar[...] = lax.fori_loop(0, ts // SUBLANES, group, hcar[...], unroll=2)

    out = (jax.nn.gelu(ybr[...]) * bbuf[...]).astype(BF16)
    m = jnp.dot(out, wout_ref[...], preferred_element_type=F32)
    _ln_and_route(alpha * x + m, lng_ref, lnb_ref, wrt_ref, br_ref, o_ref, slot_ref, gate_ref, tab_ref,
                  tri, base, first, tm=tm)


def _mix1(x, w_in, c_dw, c_dw_b, w_gate_a, b_gate_a, w_gate_x, b_gate_x, lam, w_out, ln_g, ln_b,
          w_router, b_router, *, alpha, ts=512, tm=256, rc=64):
    bsz, seq, d = x.shape
    d_rnn = c_dw.shape[-1]
    assert c_dw.shape[0] - 1 <= HALO_B
    ts = min(ts, seq)
    rc = min(rc, ts)
    assert seq % ts == 0 and ts % rc == 0 and ts % SUBLANES == 0
    r_in, r_args, r_out_specs, r_out_shape, r_scratch, tm = _router_io(w_router, b_router, bsz, seq, ts, tm)
    kern = functools.partial(_mix1_kernel, alpha=alpha, ts=ts, tm=tm, d_rnn=d_rnn, rc=rc)
    return pl.pallas_call(
        kern,
        grid=(bsz, seq // ts),
        in_specs=[
            pl.BlockSpec((1, ts, d), lambda b, s: (b, s, 0)),
            _const_spec(w_in.shape), _const_spec(c_dw.shape), _const_spec((1, d_rnn)),
            _const_spec(w_gate_a.shape), _const_spec((1, d_rnn)),
            _const_spec(w_gate_x.shape), _const_spec((1, d_rnn)),
            _const_spec((1, d_rnn)), _const_spec(w_out.shape), _const_spec((1, d)), _const_spec((1, d)),
        ] + r_in,
        out_specs=[pl.BlockSpec((1, ts, d), lambda b, s: (b, s, 0))] + r_out_specs,
        out_shape=[jax.ShapeDtypeStruct(x.shape, F32)] + r_out_shape,
        scratch_shapes=[
            pltpu.VMEM((HALO_B + ts, d_rnn), F32),
            pltpu.VMEM((ts, d_rnn), F32),
            pltpu.VMEM((ts, d_rnn), F32),
            pltpu.VMEM((ts, d_rnn), F32),
            pltpu.VMEM((1, d_rnn), F32),
        ] + r_scratch,
        compiler_params=pltpu.CompilerParams(
            dimension_semantics=("arbitrary", "arbitrary"), vmem_limit_bytes=VMEM_LIMIT),
        name="mix1",
    )(x, w_in.astype(BF16), c_dw.astype(F32), _row(c_dw_b), w_gate_a.astype(BF16), _row(b_gate_a),
      w_gate_x.astype(BF16), _row(b_gate_x), _row(lam), w_out.astype(BF16), _row(ln_g), _row(ln_b), *r_args)


def _first_argmax(vals):
    best, idx = vals[0], jnp.zeros(vals[0].shape, I32)
    for j in range(1, len(vals)):
        upd = vals[j] > best
        best = jnp.where(upd, vals[j], best)
        idx = jnp.where(upd, j, idx)
    return best, idx


def _route_tile(x, wrt_ref, br_ref, tri, base, tm):
    n_exp = wrt_ref.shape[0]
    tt = x.shape[0]
    w = wrt_ref[...]
    w_hi = w.astype(BF16)
    w_lo = (w - w_hi.astype(F32)).astype(BF16)
    x_hi = x.astype(BF16)
    x_lo = (x - x_hi.astype(F32)).astype(BF16)
    over_d = (((1,), (1,)), ((), ()))
    logits = (lax.dot_general(w_hi, x_hi, over_d, preferred_element_type=F32)
              + lax.dot_general(w_hi, x_lo, over_d, preferred_element_type=F32)
              + lax.dot_general(w_lo, x_hi, over_d, preferred_element_type=F32))
    logits = logits + br_ref[:, 0:1]
    e = jnp.exp(logits - jnp.max(logits, axis=0, keepdims=True))
    p = e / jnp.sum(e, axis=0, keepdims=True)
    rows = [p[j:j + 1, :] for j in range(n_exp)]

    n_grp = n_exp // EXPERTS_PER_GROUP
    scores = []
    for g in range(n_grp):
        a, b, c, d = rows[EXPERTS_PER_GROUP * g:EXPERTS_PER_GROUP * (g + 1)]
        scores.append(jnp.maximum(jnp.maximum(a, b) + jnp.maximum(c, d), jnp.maximum(a + b, c + d)))
    _, g_sel = _first_argmax(scores)

    p_in = []
    for j in range(EXPERTS_PER_GROUP):
        v = rows[j]
        for g in range(1, n_grp):
            v = jnp.where(g_sel == g, rows[EXPERTS_PER_GROUP * g + j], v)
        p_in.append(v)
    v1, i1 = _first_argmax(p_in)
    v2, i2 = _first_argmax([jnp.where(i1 == j, -1.0, p_in[j]) for j in range(EXPERTS_PER_GROUP)])
    den = v1 + v2
    e1 = g_sel * EXPERTS_PER_GROUP + i1
    e2 = g_sel * EXPERTS_PER_GROUP + i2

    eid = lax.broadcasted_iota(I32, (n_exp, tt), 0)
    hit1 = eid == e1
    hit2 = eid == e2
    member = jnp.where(hit1 | hit2, 1.0, 0.0)
    member_b = member.astype(BF16)
    lower = (lax.broadcasted_iota(I32, (n_exp, n_exp), 1) < lax.broadcasted_iota(I32, (n_exp, n_exp), 0))
    lower = jnp.where(lower, 1.0, 0.0)
    lane = lax.broadcasted_iota(I32, (n_exp, LANES), 1)
    place, tabs = [], []
    for h in range(tt // tm):
        cols = slice(h * tm, (h + 1) * tm)
        incl = jnp.dot(member_b[:, cols], tri[...], preferred_element_type=F32)
        count = jnp.sum(member[:, cols], axis=1, keepdims=True)
        offset = jnp.dot(lower, jnp.broadcast_to(count, (n_exp, LANES)),
                         precision=lax.Precision.HIGHEST, preferred_element_type=F32)[:, 0:1]
        place.append(offset + incl - member[:, cols])
        tabs.append(jnp.where(lane == 0, base, jnp.where(lane == 1, count, 0.0)).astype(I32))
        base = base + count
    place = jnp.concatenate(place, axis=1)
    s1 = jnp.sum(jnp.where(hit1, place, 0.0), axis=0, keepdims=True)
    s2 = jnp.sum(jnp.where(hit2, place, 0.0), axis=0, keepdims=True)
    return (s1.astype(I32), s2.astype(I32)), (v1 / den, v2 / den), tabs, base


def _ln_and_route(v, lng_ref, lnb_ref, wrt_ref, br_ref, o_ref, slot_ref, gate_ref, tab_ref, tri, base, first, *, tm):
    @pl.when(first)
    def _():
        src = lax.broadcasted_iota(I32, (tm, tm), 0)
        dst = lax.broadcasted_iota(I32, (tm, tm), 1)
        tri[...] = jnp.where(src <= dst, 1.0, 0.0).astype(BF16)
        base[...] = jnp.zeros(base.shape, F32)

    x1 = _layer_norm(v, lng_ref[...], lnb_ref[...])
    o_ref[0] = x1
    slots, gates, tabs, new_base = _route_tile(x1, wrt_ref, br_ref, tri, base[...], tm)
    for k in range(2):
        slot_ref[k:k + 1, :] = slots[k]
        gate_ref[k:k + 1, :] = gates[k]
    for h, tab in enumerate(tabs):
        tab_ref[h] = tab
    base[...] = new_base


def _router_io(w_router, b_router, bsz, seq, ts, tm):
    d, n_exp = w_router.shape
    tm = min(tm, ts)
    assert ts % tm == 0 and n_exp % EXPERTS_PER_GROUP == 0
    t = bsz * seq
    per_seq = seq // ts
    in_specs = [_const_spec((n_exp, d)), _const_spec((n_exp, LANES))]
    args = (w_router.T.astype(F32), jnp.broadcast_to(b_router.astype(F32)[:, None], (n_exp, LANES)))
    out_specs = [
        pl.BlockSpec((2, ts), lambda b, s: (0, b * per_seq + s)),
        pl.BlockSpec((2, ts), lambda b, s: (0, b * per_seq + s)),
        pl.BlockSpec((ts // tm, n_exp, LANES), lambda b, s: (b * per_seq + s, 0, 0)),
    ]
    out_shape = [
        jax.ShapeDtypeStruct((2, t), I32),
        jax.ShapeDtypeStruct((2, t), F32),
        jax.ShapeDtypeStruct((t // tm, n_exp, LANES), I32),
    ]
    scratch = [pltpu.VMEM((tm, tm), BF16), pltpu.VMEM((n_exp, LANES), F32)]
    return in_specs, args, out_specs, out_shape, scratch, tm


def _segment_copies(src_of, dst_of, length, max_len, sem, *, wait=False):
    done = 0
    for bit in reversed(range(max_len.bit_length())):
        n = 1 << bit
        piece = length & n

        @pl.when(piece != 0)
        def _(done=done, n=n):
            copy = pltpu.make_async_copy(src_of(done, n), dst_of(done, n), sem)
            copy.wait() if wait else copy.start()

        done = done + piece


def _token_rows(ref, lo, n):
    return ref.at[pl.ds(pl.multiple_of(lo * SUBLANES, SUBLANES), n * SUBLANES)]


def _store_token_major(ref, val):
    for j in range(SUBLANES):
        ref[pl.ds(j, val.shape[0], stride=SUBLANES), :] = val[:, j * LANES:(j + 1) * LANES]


def _load_token_major(ref, rows):
    return jnp.concatenate([ref[pl.ds(j, rows, stride=SUBLANES), :] for j in range(SUBLANES)], axis=1)


def _dispatch_kernel(seg_dst_ref, seg_len_ref, pad_dst_ref, pad_len_ref, n_used_ref, x_ref, slot_ref, xs_hbm,
                     grouped, zeros, sems, pad_sem, *, tm, n_exp, bm):
    i = pl.program_id(0)
    n_tiles = pl.num_programs(0)
    buf = lax.rem(i, 2)

    def wait_tile(b):
        pltpu.make_async_copy(grouped.at[b], _token_rows(xs_hbm, 0, 2 * tm), sems.at[b]).wait()

    def pad_copies(wait):
        for e in range(n_exp):
            _segment_copies(lambda lo, n: _token_rows(zeros, lo, n),
                            lambda lo, n, e=e: _token_rows(xs_hbm, pad_dst_ref[e] + lo, n),
                            pad_len_ref[e], bm - 1, pad_sem, wait=wait)
        n_blk = xs_hbm.shape[0] // (bm * SUBLANES)
        for blk in range(n_blk - n_exp, n_blk):
            @pl.when(blk >= n_used_ref[0])
            def _(blk=blk):
                copy = pltpu.make_async_copy(zeros, _token_rows(xs_hbm, blk * bm, bm), pad_sem)
                copy.wait() if wait else copy.start()

    @pl.when(i == 0)
    def _():
        zeros[...] = jnp.zeros(zeros.shape, F32)
        pad_copies(wait=False)

    @pl.when(i >= 2)
    def _():
        wait_tile(buf)

    row = lax.broadcasted_iota(I32, (2 * tm, tm), 0)
    onehot = jnp.where((row == slot_ref[0:1, :]) | (row == slot_ref[1:2, :]), 1.0, 0.0).astype(BF16)
    _store_token_major(grouped.at[buf], jnp.dot(onehot, x_ref[...].astype(BF16), preferred_element_type=F32))

    start = 0
    for e in range(n_exp):
        length = seg_len_ref[i * n_exp + e]
        dst = seg_dst_ref[i * n_exp + e]
        _segment_copies(lambda lo, n, start=start: _token_rows(grouped.at[buf], start + lo, n),
                        lambda lo, n, dst=dst: _token_rows(xs_hbm, dst + lo, n),
                        length, tm, sems.at[buf])
        start = start + length

    @pl.when(i == n_tiles - 1)
    def _():
        wait_tile(buf)

        @pl.when(n_tiles > 1)
        def _():
            wait_tile(1 - buf)

        pad_copies(wait=True)


def _dispatch(xf, slots, seg_dst, seg_len, pad_dst, pad_len, n_used, n_pad, *, tm, bm):
    t, d = xf.shape
    assert d == SUBLANES * LANES
    n_exp = pad_dst.shape[0]
    grid_spec = pltpu.PrefetchScalarGridSpec(
        num_scalar_prefetch=5,
        grid=(t // tm,),
        in_specs=[
            pl.BlockSpec((tm, d), lambda i, *_: (i, 0)),
            pl.BlockSpec((2, tm), lambda i, *_: (0, i)),
        ],
        out_specs=pl.BlockSpec(memory_space=pl.ANY),
        scratch_shapes=[
            pltpu.VMEM((2, 2 * tm * SUBLANES, LANES), F32),
            pltpu.VMEM((bm * SUBLANES, LANES), F32),
            pltpu.SemaphoreType.DMA((2,)),
            pltpu.SemaphoreType.DMA(()),
        ],
    )
    return pl.pallas_call(
        functools.partial(_dispatch_kernel, tm=tm, n_exp=n_exp, bm=bm),
        grid_spec=grid_spec,
        out_shape=jax.ShapeDtypeStruct((n_pad * SUBLANES, LANES), F32),
        compiler_params=pltpu.CompilerParams(
            dimension_semantics=("arbitrary",), vmem_limit_bytes=VMEM_LIMIT),
        name="dispatch",
    )(seg_dst, seg_len, pad_dst, pad_len, n_used, xf, slots)


def _experts_kernel(first_blk_ref, n_blk_ref, n_used_ref, xs_hbm, w1_ref, w3_ref, w2_ref, y_hbm,
                    w1b, w3b, w2b, xbuf, ybuf, in_sems, out_sems, *, bm, n_exp):
    e = pl.program_id(0)
    first = first_blk_ref[e]
    n = n_blk_ref[e]

    def x_copy(blk, slot):
        return pltpu.make_async_copy(_token_rows(xs_hbm, blk * bm, bm), xbuf.at[slot], in_sems.at[slot])

    def y_copy(blk, slot):
        return pltpu.make_async_copy(ybuf.at[slot], _token_rows(y_hbm, blk * bm, bm), out_sems.at[slot])

    @pl.when((e == 0) & (n > 0))
    def _():
        x_copy(first, 0).start(BLOCK_DMA_PRIORITY)

    w1b[...] = w1_ref[0, 0].astype(BF16)
    w3b[...] = w3_ref[0, 0].astype(BF16)
    w2b[...] = w2_ref[0, 0].astype(BF16)

    def block(j, carry):
        slot = lax.rem(j, 2)

        @pl.when(j + 1 < n)
        def _():
            x_copy(first + j + 1, 1 - slot).start(BLOCK_DMA_PRIORITY)

        x_copy(first + j, slot).wait()

        @pl.when(j >= 2)
        def _():
            y_copy(first + j - 2, slot).wait()

        xb = _load_token_major(xbuf.at[slot], bm).astype(BF16)
        h1 = jnp.dot(xb, w1b[...], preferred_element_type=F32)
        h3 = jnp.dot(xb, w3b[...], preferred_element_type=F32)
        hb = (h1 * jax.nn.sigmoid(h1) * h3).astype(BF16)
        _store_token_major(ybuf.at[slot], jnp.dot(hb, w2b[...], preferred_element_type=F32))
        y_copy(first + j, slot).start(BLOCK_DMA_PRIORITY)
        return carry

    lax.fori_loop(0, n, block, 0)

    nxt = jnp.minimum(e + 1, n_exp - 1)

    @pl.when((e + 1 < n_exp) & (n_blk_ref[nxt] > 0))
    def _():
        x_copy(first_blk_ref[nxt], 0).start(BLOCK_DMA_PRIORITY)

    for back in (2, 1):
        @pl.when(n >= back)
        def _(back=back):
            y_copy(first + n - back, lax.rem(n - back, 2)).wait()

    @pl.when(e == n_exp - 1)
    def _():
        ybuf[0] = jnp.zeros(ybuf.shape[1:], F32)
        n_all = y_hbm.shape[0] // (bm * SUBLANES)
        for blk in range(n_all - n_exp, n_all):
            @pl.when(blk >= n_used_ref[0])
            def _(blk=blk):
                y_copy(blk, 0).start(BLOCK_DMA_PRIORITY)
                y_copy(blk, 0).wait()


def _experts(xs, first_blk, n_blk, n_used, w1, w3, w2, layer, *, bm):
    n_exp, d, d_e = w1.shape[-3:]
    assert d == SUBLANES * LANES
    w_blk = lambda e, *_: (layer, e, 0, 0)
    grid_spec = pltpu.PrefetchScalarGridSpec(
        num_scalar_prefetch=3,
        grid=(n_exp,),
        in_specs=[
            pl.BlockSpec(memory_space=pl.ANY),
            pl.BlockSpec((1, 1, d, d_e), w_blk),
            pl.BlockSpec((1, 1, d, d_e), w_blk),
            pl.BlockSpec((1, 1, d_e, d), w_blk),
        ],
        out_specs=pl.BlockSpec(memory_space=pl.ANY),
        scratch_shapes=[
            pltpu.VMEM((d, d_e), BF16), pltpu.VMEM((d, d_e), BF16), pltpu.VMEM((d_e, d), BF16),
            pltpu.VMEM((2, bm * SUBLANES, LANES), F32), pltpu.VMEM((2, bm * SUBLANES, LANES), F32),
            pltpu.SemaphoreType.DMA((2,)), pltpu.SemaphoreType.DMA((2,)),
        ],
    )
    return pl.pallas_call(
        functools.partial(_experts_kernel, bm=bm, n_exp=n_exp),
        grid_spec=grid_spec,
        out_shape=jax.ShapeDtypeStruct(xs.shape, F32),
        compiler_params=pltpu.CompilerParams(
            dimension_semantics=("arbitrary",), vmem_limit_bytes=VMEM_LIMIT),
        name="experts",
    )(first_blk, n_blk, n_used, xs, w1, w3, w2)


def _combine_kernel(seg_src_ref, seg_len_ref, x_ref, slot_ref, gate_ref, lng_ref, lnb_ref, y_hbm, o_ref,
                    grouped, sems, *, alpha, tm, n_exp):
    i = pl.program_id(0)
    n_tiles = pl.num_programs(0)
    buf = lax.rem(i, 2)

    def fetch(tile, b):
        start = 0
        for e in range(n_exp):
            length = seg_len_ref[tile * n_exp + e]
            src = seg_src_ref[tile * n_exp + e]
            _segment_copies(lambda lo, n, src=src: _token_rows(y_hbm, src + lo, n),
                            lambda lo, n, start=start: _token_rows(grouped.at[b], start + lo, n),
                            length, tm, sems.at[b])
            start = start + length

    @pl.when(i == 0)
    def _():
        fetch(0, 0)

    @pl.when(i + 1 < n_tiles)
    def _():
        fetch(i + 1, 1 - buf)

    pltpu.make_async_copy(_token_rows(y_hbm, 0, 2 * tm), grouped.at[buf], sems.at[buf]).wait()

    row = lax.broadcasted_iota(I32, (2 * tm, tm), 0)
    g = (jnp.where(row == slot_ref[0:1, :], gate_ref[0:1, :], 0.0)
         + jnp.where(row == slot_ref[1:2, :], gate_ref[1:2, :], 0.0))
    g_hi = g.astype(BF16)
    g_lo = (g - g_hi.astype(F32)).astype(BF16)
    yb = _load_token_major(grouped.at[buf], 2 * tm).astype(BF16)
    over_rows = (((0,), (0,)), ((), ()))
    f = (lax.dot_general(g_hi, yb, over_rows, preferred_element_type=F32)
         + lax.dot_general(g_lo, yb, over_rows, preferred_element_type=F32))
    o_ref[...] = _layer_norm(alpha * x_ref[...] + f, lng_ref[...], lnb_ref[...])


def _combine(xf, y, slots, gates, seg_src, seg_len, ln_g, ln_b, *, alpha, tm, n_exp):
    t, d = xf.shape
    grid_spec = pltpu.PrefetchScalarGridSpec(
        num_scalar_prefetch=2,
        grid=(t // tm,),
        in_specs=[
            pl.BlockSpec((tm, d), lambda i, *_: (i, 0)),
            pl.BlockSpec((2, tm), lambda i, *_: (0, i)),
            pl.BlockSpec((2, tm), lambda i, *_: (0, i)),
            pl.BlockSpec((1, d), lambda i, *_: (0, 0)),
            pl.BlockSpec((1, d), lambda i, *_: (0, 0)),
            pl.BlockSpec(memory_space=pl.ANY),
        ],
        out_specs=pl.BlockSpec((tm, d), lambda i, *_: (i, 0)),
        scratch_shapes=[pltpu.VMEM((2, 2 * tm * SUBLANES, LANES), F32), pltpu.SemaphoreType.DMA((2,))],
    )
    return pl.pallas_call(
        functools.partial(_combine_kernel, alpha=alpha, tm=tm, n_exp=n_exp),
        grid_spec=grid_spec,
        out_shape=jax.ShapeDtypeStruct((t, d), F32),
        compiler_params=pltpu.CompilerParams(
            dimension_semantics=("arbitrary",), vmem_limit_bytes=VMEM_LIMIT),
        name="combine",
    )(seg_src, seg_len, xf, slots, gates, _row(ln_g), _row(ln_b), y)


def _moe_ln(x, slots, gates, tab, w1, w3, w2, layer, ln_g, ln_b, *, alpha, bm=256):
    bsz, seq, d = x.shape
    t = bsz * seq
    n_exp = tab.shape[1]
    tm = t // tab.shape[0]
    xf = x.reshape(t, d)
    before, in_tile = tab[:, :, 0], tab[:, :, 1]
    counts = before[-1] + in_tile[-1]
    padded = (counts + bm - 1) // bm * bm
    pend = jnp.cumsum(padded)
    pstart = pend - padded
    n_blk = -(-(2 * t) // bm) + n_exp
    n_used = (pend[-1] // bm).reshape(1).astype(I32)
    seg_pos = (pstart[None, :] + before).reshape(-1).astype(I32)
    seg_len = in_tile.reshape(-1).astype(I32)
    xs = _dispatch(xf, slots, seg_pos, seg_len, (pstart + counts).astype(I32), (padded - counts).astype(I32),
                   n_used, n_blk * bm, tm=tm, bm=bm)
    y = _experts(xs, (pstart // bm).astype(I32), (padded // bm).astype(I32), n_used, w1, w3, w2, layer, bm=bm)
    out = _combine(xf, y, slots, gates, seg_pos, seg_len, ln_g, ln_b, alpha=alpha, tm=tm, n_exp=n_exp)
    return out.reshape(bsz, seq, d)


def kernel(x, ln1_g, ln1_b, ln2_g, ln2_b, even_w_in, even_a_dw, even_a_dw_b, even_a_ln_g, even_a_ln_b, even_b_dw, even_w_out, odd_w_in, odd_c_dw, odd_c_dw_b, odd_w_gate_a, odd_b_gate_a, odd_w_gate_x, odd_b_gate_x, odd_lam, odd_w_out, w_router, b_router, moe_w1, moe_w3, moe_w2):
    depth = ln1_g.shape[0]
    alpha = (2.0 * depth) ** 0.25
    for layer in range(depth):
        j = layer // 2
        if layer % 2 == 0:
            x, slots, gates, tab = _mix0(
                x, even_w_in[j], even_a_dw[j], even_a_dw_b[j], even_a_ln_g[j], even_a_ln_b[j],
                even_b_dw[j], even_w_out[j], ln1_g[layer], ln1_b[layer], w_router, b_router, alpha=alpha)
        else:
            x, slots, gates, tab = _mix1(
                x, odd_w_in[j], odd_c_dw[j], odd_c_dw_b[j], odd_w_gate_a[j], odd_b_gate_a[j],
                odd_w_gate_x[j], odd_b_gate_x[j], odd_lam[j], odd_w_out[j], ln1_g[layer], ln1_b[layer],
                w_router, b_router, alpha=alpha)
        x = _moe_ln(x, slots, gates, tab, moe_w1, moe_w3, moe_w2, layer,
                    ln2_g[layer], ln2_b[layer], alpha=alpha)
    return x
```

```python
import functools

import jax
import jax.numpy as jnp
from jax import lax
from jax.experimental import pallas as pl
from jax.experimental.pallas import tpu as pltpu

F32 = jnp.float32
BF16 = jnp.bfloat16
I32 = jnp.int32
LN_EPS = 1e-5
LRU_C = 8.0
SUBLANES = 8
LANES = 128
HALO_A = 32
HALO_B = 8
EXPERTS_PER_GROUP = 4
BLOCK_DMA_PRIORITY = 1
EXPERT_IN_SLOTS = 4
VMEM_LIMIT = 56 * 1024 * 1024


def _layer_norm(v, g, b):
    mu = jnp.mean(v, axis=-1, keepdims=True)
    c = v - mu
    var = jnp.mean(c * c, axis=-1, keepdims=True)
    return c * lax.rsqrt(var + LN_EPS) * g + b


def _row(v):
    return v.reshape(1, -1).astype(F32)


def _const_spec(shape):
    return pl.BlockSpec(shape, lambda *_: (0,) * len(shape))


def _mix0_kernel(x_ref, win_ref, adw_ref, adwb_ref, alng_ref, alnb_ref, bdw_ref, wout_ref, lng_ref, lnb_ref,
                 wrt_ref, br_ref, o_ref, slot_ref, gate_ref, tab_ref, *scratch, alpha, ts, tm, d_a, d_b, rc, parts):
    tri, base = scratch[-2:]
    bufs = [scratch[5 * p:5 * p + 5] for p in range(parts)]
    first = (pl.program_id(0) == 0) & (pl.program_id(1) == 0)
    n_a = adw_ref.shape[0]
    n_b = bdw_ref.shape[0]
    hs = ts // parts

    @pl.when(pl.program_id(1) == 0)
    def _():
        bufs[0][0][0:HALO_A, :] = jnp.zeros((HALO_A, d_a), F32)
        bufs[0][2][0:HALO_B, :] = jnp.zeros((HALO_B, d_b), F32)

    x = x_ref[0]
    xb = x.astype(BF16)
    cw = 2 * LANES
    dot = functools.partial(jnp.dot, preferred_element_type=F32)

    def proj_units(p):
        abuf, ash, cbuf, bgate, cat = bufs[p]
        xp = xb[p * hs:(p + 1) * hs, :]
        units = []
        for c0 in range(0, d_a, cw):
            def unit(c0=c0):
                val = dot(xp, win_ref[:, c0:c0 + cw])
                gate = dot(xp, win_ref[:, d_a + c0:d_a + c0 + cw])
                abuf[HALO_A:HALO_A + hs, c0:c0 + cw] = val * jax.nn.sigmoid(gate)
            units.append(unit)
        for c0 in range(0, d_b, cw):
            def unit(c0=c0, o=2 * d_a):
                bgate[:, c0:c0 + cw] = dot(xp, win_ref[:, o + c0:o + c0 + cw])
                cbuf[HALO_B:HALO_B + hs, c0:c0 + cw] = (dot(xp, win_ref[:, o + d_b + c0:o + d_b + c0 + cw])
                                                        * dot(xp, win_ref[:, o + 2 * d_b + c0:o + 2 * d_b + c0 + cw]))
            units.append(unit)
        return units

    def conv_units(p):
        abuf, ash, cbuf, bgate, cat = bufs[p]

        def shifted():
            n_sh = hs + HALO_A - SUBLANES
            for j in range(1, SUBLANES):
                ash[j - 1, :, :] = abuf[j:j + n_sh, :]
        units = [shifted]
        for c in range(hs // rc):
            def unit(r0=c * rc):
                acc = adwb_ref[...]
                for k in range(n_a):
                    q, j = divmod(HALO_A - (n_a - 1) + k, SUBLANES)
                    lo = r0 + q * SUBLANES
                    src = abuf[lo:lo + rc, :] if j == 0 else ash[j - 1, lo:lo + rc, :]
                    acc = acc + adw_ref[k:k + 1, :] * src
                a = _layer_norm(acc, alng_ref[...], alnb_ref[...])
                cat[r0:r0 + rc, 0:d_a] = (a * jax.nn.sigmoid(a)).astype(BF16)
                bb = jnp.zeros((rc, d_b), F32)
                for k in range(n_b):
                    lo = r0 + HALO_B - (n_b - 1) + k
                    bb = bb + bdw_ref[k:k + 1, :] * cbuf[lo:lo + rc, :]
                cat[r0:r0 + rc, d_a:d_a + d_b] = (bgate[r0:r0 + rc, :] * bb).astype(BF16)
            units.append(unit)
        return units

    m_cols = [[] for _ in range(parts)]

    def out_units(p):
        cat = bufs[p][4]
        units = []
        for c0 in range(0, wout_ref.shape[1], cw):
            def unit(c0=c0):
                m_cols[p].append(dot(cat[...], wout_ref[:, c0:c0 + cw]))
            units.append(unit)
        return units

    def run_interleaved(*groups):
        order = sorted(((i + 0.5) / len(g), gi, i) for gi, g in enumerate(groups) for i in range(len(g)))
        for _, gi, i in order:
            groups[gi][i]()

    for s in range(parts + 2):
        groups = []
        if s < parts:
            groups.append(proj_units(s))
        if 0 <= s - 1 < parts:
            groups.append(conv_units(s - 1))
        if 0 <= s - 2 < parts:
            groups.append(out_units(s - 2))
        run_interleaved(*groups)
        if s + 1 < parts:
            bufs[s + 1][0][0:HALO_A, :] = bufs[s][0][hs:hs + HALO_A, :]
            bufs[s + 1][2][0:HALO_B, :] = bufs[s][2][hs:hs + HALO_B, :]

    bufs[0][0][0:HALO_A, :] = bufs[-1][0][hs:hs + HALO_A, :]
    bufs[0][2][0:HALO_B, :] = bufs[-1][2][hs:hs + HALO_B, :]

    m = jnp.concatenate([jnp.concatenate(cols, axis=1) for cols in m_cols], axis=0)
    _ln_and_route(alpha * x + m, lng_ref, lnb_ref, wrt_ref, br_ref, o_ref, slot_ref, gate_ref, tab_ref,
                  tri, base, first, tm=tm)


def _mix0(x, w_in, a_dw, a_dw_b, a_ln_g, a_ln_b, b_dw, w_out, ln_g, ln_b, w_router, b_router,
          *, alpha, ts=512, tm=256, rc=32, parts=2):
    bsz, seq, d = x.shape
    d_a = a_dw.shape[-1]
    d_b = b_dw.shape[-1]
    assert a_dw.shape[0] - 1 <= HALO_A and b_dw.shape[0] - 1 <= HALO_B
    ts = min(ts, seq)
    hs = ts // parts
    rc = min(rc, hs)
    assert seq % ts == 0 and ts % parts == 0 and hs % rc == 0 and rc % (2 * SUBLANES) == 0
    r_in, r_args, r_out_specs, r_out_shape, r_scratch, tm = _router_io(w_router, b_router, bsz, seq, ts, tm)
    kern = functools.partial(_mix0_kernel, alpha=alpha, ts=ts, tm=tm, d_a=d_a, d_b=d_b, rc=rc, parts=parts)
    part_scratch = [
        pltpu.VMEM((HALO_A + hs, d_a), F32),
        pltpu.VMEM((SUBLANES - 1, HALO_A + hs - SUBLANES, d_a), F32),
        pltpu.VMEM((HALO_B + hs, d_b), F32),
        pltpu.VMEM((hs, d_b), F32),
        pltpu.VMEM((hs, d_a + d_b), BF16),
    ]
    return pl.pallas_call(
        kern,
        grid=(bsz, seq // ts),
        in_specs=[
            pl.BlockSpec((1, ts, d), lambda b, s: (b, s, 0)),
            _const_spec(w_in.shape), _const_spec(a_dw.shape), _const_spec((1, d_a)), _const_spec((1, d_a)),
            _const_spec((1, d_a)), _const_spec(b_dw.shape), _const_spec(w_out.shape),
            _const_spec((1, d)), _const_spec((1, d)),
        ] + r_in,
        out_specs=[pl.BlockSpec((1, ts, d), lambda b, s: (b, s, 0))] + r_out_specs,
        out_shape=[jax.ShapeDtypeStruct(x.shape, F32)] + r_out_shape,
        scratch_shapes=part_scratch * parts + r_scratch,
        compiler_params=pltpu.CompilerParams(
            dimension_semantics=("arbitrary", "arbitrary"), vmem_limit_bytes=VMEM_LIMIT),
        name="mix0",
    )(x, w_in.astype(BF16), a_dw.astype(F32), _row(a_dw_b), _row(a_ln_g), _row(a_ln_b),
      b_dw.astype(F32), w_out.astype(BF16), _row(ln_g), _row(ln_b), *r_args)


def _mix1_kernel(x_ref, win_ref, cdw_ref, cdwb_ref, wga_ref, bga_ref, wgx_ref, bgx_ref, lam_ref, wout_ref,
                 lng_ref, lnb_ref, wrt_ref, br_ref, o_ref, slot_ref, gate_ref, tab_ref,
                 cbuf, ybr, abuf, bbuf, hcar, tri, base, *, alpha, ts, tm, d_rnn, rc):
    first = (pl.program_id(0) == 0) & (pl.program_id(1) == 0)

    @pl.when(pl.program_id(1) == 0)
    def _():
        cbuf[0:HALO_B, :] = jnp.zeros((HALO_B, d_rnn), F32)
        hcar[...] = jnp.zeros((1, d_rnn), F32)

    x = x_ref[0]
    xb = x.astype(BF16)
    ybr[...] = jnp.dot(xb, win_ref[:, 0:d_rnn], preferred_element_type=F32)
    cbuf[HALO_B:HALO_B + ts, :] = jnp.dot(xb, win_ref[:, d_rnn:2 * d_rnn], preferred_element_type=F32)

    n_heads, blk = wga_ref.shape[0], wga_ref.shape[1]
    n_c = cdw_ref.shape[0]
    neg_c_sp = -LRU_C * jax.nn.softplus(-lam_ref[...])
    for c in range(ts // rc):
        r0 = c * rc
        xr = cdwb_ref[...]
        for k in range(n_c):
            lo = r0 + HALO_B - (n_c - 1) + k
            xr = xr + cdw_ref[k:k + 1, :] * cbuf[lo:lo + rc, :]
        xrb = xr.astype(BF16)
        for h in range(n_heads):
            cs = slice(h * blk, (h + 1) * blk)
            r = jax.nn.sigmoid(jnp.dot(xrb[:, cs], wga_ref[h], preferred_element_type=F32) + bga_ref[:, cs])
            i = jax.nn.sigmoid(jnp.dot(xrb[:, cs], wgx_ref[h], preferred_element_type=F32) + bgx_ref[:, cs])
            log_a = neg_c_sp[:, cs] * r
            abuf[r0:r0 + rc, cs] = jnp.exp(log_a)
            th = jnp.tanh(log_a)
            bbuf[r0:r0 + rc, cs] = jnp.sqrt(-2.0 * th / (1.0 - th)) * (i * xr[:, cs])

    cbuf[0:HALO_B, :] = cbuf[ts:ts + HALO_B, :]

    rowid = lax.broadcasted_iota(I32, (SUBLANES, d_rnn), 0)

    def group(g, h_prev):
        r = pl.multiple_of(g * SUBLANES, SUBLANES)
        a = abuf[pl.ds(r, SUBLANES), :]
        b = bbuf[pl.ds(r, SUBLANES), :]
        step = 1
        while step < SUBLANES:
            keep = rowid >= step
            b = jnp.where(keep, a * pltpu.roll(b, step, axis=0) + b, b)
            a = jnp.where(keep, a * pltpu.roll(a, step, axis=0), a)
            step *= 2
        h = a * h_prev + b
        bbuf[pl.ds(r, SUBLANES), :] = h
        return h[SUBLANES - 1:SUBLANES, :]

    hcar[...] = lax.fori_loop(0, ts // SUBLANES, group, hcar[...], unroll=2)

    out = (jax.nn.gelu(ybr[...]) * bbuf[...]).astype(BF16)
    m = jnp.dot(out, wout_ref[...], preferred_element_type=F32)
    _ln_and_route(alpha * x + m, lng_ref, lnb_ref, wrt_ref, br_ref, o_ref, slot_ref, gate_ref, tab_ref,
                  tri, base, first, tm=tm)


def _mix1(x, w_in, c_dw, c_dw_b, w_gate_a, b_gate_a, w_gate_x, b_gate_x, lam, w_out, ln_g, ln_b,
          w_router, b_router, *, alpha, ts=512, tm=256, rc=64):
    bsz, seq, d = x.shape
    d_rnn = c_dw.shape[-1]
    assert c_dw.shape[0] - 1 <= HALO_B
    ts = min(ts, seq)
    rc = min(rc, ts)
    assert seq % ts == 0 and ts % rc == 0 and ts % SUBLANES == 0
    r_in, r_args, r_out_specs, r_out_shape, r_scratch, tm = _router_io(w_router, b_router, bsz, seq, ts, tm)
    kern = functools.partial(_mix1_kernel, alpha=alpha, ts=ts, tm=tm, d_rnn=d_rnn, rc=rc)
    return pl.pallas_call(
        kern,
        grid=(bsz, seq // ts),
        in_specs=[
            pl.BlockSpec((1, ts, d), lambda b, s: (b, s, 0)),
            _const_spec(w_in.shape), _const_spec(c_dw.shape), _const_spec((1, d_rnn)),
            _const_spec(w_gate_a.shape), _const_spec((1, d_rnn)),
            _const_spec(w_gate_x.shape), _const_spec((1, d_rnn)),
            _const_spec((1, d_rnn)), _const_spec(w_out.shape), _const_spec((1, d)), _const_spec((1, d)),
        ] + r_in,
        out_specs=[pl.BlockSpec((1, ts, d), lambda b, s: (b, s, 0))] + r_out_specs,
        out_shape=[jax.ShapeDtypeStruct(x.shape, F32)] + r_out_shape,
        scratch_shapes=[
            pltpu.VMEM((HALO_B + ts, d_rnn), F32),
            pltpu.VMEM((ts, d_rnn), F32),
            pltpu.VMEM((ts, d_rnn), F32),
            pltpu.VMEM((ts, d_rnn), F32),
            pltpu.VMEM((1, d_rnn), F32),
        ] + r_scratch,
        compiler_params=pltpu.CompilerParams(
            dimension_semantics=("arbitrary", "arbitrary"), vmem_limit_bytes=VMEM_LIMIT),
        name="mix1",
    )(x, w_in.astype(BF16), c_dw.astype(F32), _row(c_dw_b), w_gate_a.astype(BF16), _row(b_gate_a),
      w_gate_x.astype(BF16), _row(b_gate_x), _row(lam), w_out.astype(BF16), _row(ln_g), _row(ln_b), *r_args)


def _first_argmax(vals):
    best, idx = vals[0], jnp.zeros(vals[0].shape, I32)
    for j in range(1, len(vals)):
        upd = vals[j] > best
        best = jnp.where(upd, vals[j], best)
        idx = jnp.where(upd, j, idx)
    return best, idx


def _route_tile(x, wrt_ref, br_ref, tri, base, tm):
    n_exp = wrt_ref.shape[0]
    tt = x.shape[0]
    w = wrt_ref[...]
    w_hi = w.astype(BF16)
    w_lo = (w - w_hi.astype(F32)).astype(BF16)
    x_hi = x.astype(BF16)
    x_lo = (x - x_hi.astype(F32)).astype(BF16)
    over_d = (((1,), (1,)), ((), ()))
    logits = (lax.dot_general(w_hi, x_hi, over_d, preferred_element_type=F32)
              + lax.dot_general(w_hi, x_lo, over_d, preferred_element_type=F32)
              + lax.dot_general(w_lo, x_hi, over_d, preferred_element_type=F32))
    logits = logits + br_ref[:, 0:1]
    e = jnp.exp(logits - jnp.max(logits, axis=0, keepdims=True))
    p = e / jnp.sum(e, axis=0, keepdims=True)
    rows = [p[j:j + 1, :] for j in range(n_exp)]

    n_grp = n_exp // EXPERTS_PER_GROUP
    scores = []
    for g in range(n_grp):
        a, b, c, d = rows[EXPERTS_PER_GROUP * g:EXPERTS_PER_GROUP * (g + 1)]
        scores.append(jnp.maximum(jnp.maximum(a, b) + jnp.maximum(c, d), jnp.maximum(a + b, c + d)))
    _, g_sel = _first_argmax(scores)

    p_in = []
    for j in range(EXPERTS_PER_GROUP):
        v = rows[j]
        for g in range(1, n_grp):
            v = jnp.where(g_sel == g, rows[EXPERTS_PER_GROUP * g + j], v)
        p_in.append(v)
    v1, i1 = _first_argmax(p_in)
    v2, i2 = _first_argmax([jnp.where(i1 == j, -1.0, p_in[j]) for j in range(EXPERTS_PER_GROUP)])
    den = v1 + v2
    e1 = g_sel * EXPERTS_PER_GROUP + i1
    e2 = g_sel * EXPERTS_PER_GROUP + i2

    eid = lax.broadcasted_iota(I32, (n_exp, tt), 0)
    hit1 = eid == e1
    hit2 = eid == e2
    member = jnp.where(hit1 | hit2, 1.0, 0.0)
    member_b = member.astype(BF16)
    lower = (lax.broadcasted_iota(I32, (n_exp, n_exp), 1) < lax.broadcasted_iota(I32, (n_exp, n_exp), 0))
    lower = jnp.where(lower, 1.0, 0.0).astype(BF16)
    lane = lax.broadcasted_iota(I32, (n_exp, LANES), 1)
    place, tabs = [], []
    for h in range(tt // tm):
        cols = slice(h * tm, (h + 1) * tm)
        incl = jnp.dot(member_b[:, cols], tri[...], preferred_element_type=F32)
        count = jnp.sum(member[:, cols], axis=1, keepdims=True)
        offset = jnp.dot(lower, jnp.broadcast_to(count, (n_exp, LANES)).astype(BF16),
                         preferred_element_type=F32)[:, 0:1]
        place.append(offset + incl - member[:, cols])
        tabs.append(jnp.where(lane == 0, base, jnp.where(lane == 1, count, 0.0)).astype(I32))
        base = base + count
    place = jnp.concatenate(place, axis=1)
    s1 = jnp.sum(jnp.where(hit1, place, 0.0), axis=0, keepdims=True)
    s2 = jnp.sum(jnp.where(hit2, place, 0.0), axis=0, keepdims=True)
    return (s1.astype(I32), s2.astype(I32)), (v1 / den, v2 / den), tabs, base


def _ln_and_route(v, lng_ref, lnb_ref, wrt_ref, br_ref, o_ref, slot_ref, gate_ref, tab_ref, tri, base, first, *, tm):
    @pl.when(first)
    def _():
        src = lax.broadcasted_iota(I32, (tm, tm), 0)
        dst = lax.broadcasted_iota(I32, (tm, tm), 1)
        tri[...] = jnp.where(src <= dst, 1.0, 0.0).astype(BF16)
        base[...] = jnp.zeros(base.shape, F32)

    x1 = _layer_norm(v, lng_ref[...], lnb_ref[...])
    o_ref[0] = x1
    slots, gates, tabs, new_base = _route_tile(x1, wrt_ref, br_ref, tri, base[...], tm)
    for k in range(2):
        slot_ref[k:k + 1, :] = slots[k]
        gate_ref[k:k + 1, :] = gates[k]
    for h, tab in enumerate(tabs):
        tab_ref[h] = tab
    base[...] = new_base


def _router_io(w_router, b_router, bsz, seq, ts, tm):
    d, n_exp = w_router.shape
    tm = min(tm, ts)
    assert ts % tm == 0 and n_exp % EXPERTS_PER_GROUP == 0
    t = bsz * seq
    per_seq = seq // ts
    in_specs = [_const_spec((n_exp, d)), _const_spec((n_exp, LANES))]
    args = (w_router.T.astype(F32), jnp.broadcast_to(b_router.astype(F32)[:, None], (n_exp, LANES)))
    out_specs = [
        pl.BlockSpec((2, ts), lambda b, s: (0, b * per_seq + s)),
        pl.BlockSpec((2, ts), lambda b, s: (0, b * per_seq + s)),
        pl.BlockSpec((ts // tm, n_exp, LANES), lambda b, s: (b * per_seq + s, 0, 0)),
    ]
    out_shape = [
        jax.ShapeDtypeStruct((2, t), I32),
        jax.ShapeDtypeStruct((2, t), F32),
        jax.ShapeDtypeStruct((t // tm, n_exp, LANES), I32),
    ]
    scratch = [pltpu.VMEM((tm, tm), BF16), pltpu.VMEM((n_exp, LANES), F32)]
    return in_specs, args, out_specs, out_shape, scratch, tm


def _segment_copies(src_of, dst_of, length, max_len, sem, *, wait=False):
    done = 0
    for bit in reversed(range(max_len.bit_length())):
        n = 1 << bit
        piece = length & n

        @pl.when(piece != 0)
        def _(done=done, n=n):
            copy = pltpu.make_async_copy(src_of(done, n), dst_of(done, n), sem)
            copy.wait() if wait else copy.start()

        done = done + piece


def _token_rows(ref, lo, n):
    return ref.at[pl.ds(pl.multiple_of(lo * SUBLANES, SUBLANES), n * SUBLANES)]


def _store_token_major(ref, val):
    for j in range(SUBLANES):
        ref[pl.ds(j, val.shape[0], stride=SUBLANES), :] = val[:, j * LANES:(j + 1) * LANES]


def _load_token_major(ref, rows):
    return jnp.concatenate([ref[pl.ds(j, rows, stride=SUBLANES), :] for j in range(SUBLANES)], axis=1)


def _dispatch_kernel(seg_dst_ref, seg_len_ref, pad_dst_ref, pad_len_ref, n_used_ref, x_ref, slot_ref, xs_hbm,
                     grouped, zeros, sems, pad_sem, *, tm, n_exp, bm):
    i = pl.program_id(0)
    n_tiles = pl.num_programs(0)
    buf = lax.rem(i, 2)

    def wait_tile(b):
        pltpu.make_async_copy(grouped.at[b], _token_rows(xs_hbm, 0, 2 * tm), sems.at[b]).wait()

    def pad_copies(wait):
        for e in range(n_exp):
            _segment_copies(lambda lo, n: _token_rows(zeros, lo, n),
                            lambda lo, n, e=e: _token_rows(xs_hbm, pad_dst_ref[e] + lo, n),
                            pad_len_ref[e], bm - 1, pad_sem, wait=wait)
        n_blk = xs_hbm.shape[0] // (bm * SUBLANES)
        for blk in range(n_blk - n_exp, n_blk):
            @pl.when(blk >= n_used_ref[0])
            def _(blk=blk):
                copy = pltpu.make_async_copy(zeros, _token_rows(xs_hbm, blk * bm, bm), pad_sem)
                copy.wait() if wait else copy.start()

    @pl.when(i == 0)
    def _():
        zeros[...] = jnp.zeros(zeros.shape, F32)
        pad_copies(wait=False)

    @pl.when(i >= 2)
    def _():
        wait_tile(buf)

    row = lax.broadcasted_iota(I32, (2 * tm, tm), 0)
    onehot = jnp.where((row == slot_ref[0:1, :]) | (row == slot_ref[1:2, :]), 1.0, 0.0).astype(BF16)
    _store_token_major(grouped.at[buf], jnp.dot(onehot, x_ref[...].astype(BF16), preferred_element_type=F32))

    start = 0
    for e in range(n_exp):
        length = seg_len_ref[i * n_exp + e]
        dst = seg_dst_ref[i * n_exp + e]
        _segment_copies(lambda lo, n, start=start: _token_rows(grouped.at[buf], start + lo, n),
                        lambda lo, n, dst=dst: _token_rows(xs_hbm, dst + lo, n),
                        length, tm, sems.at[buf])
        start = start + length

    @pl.when(i == n_tiles - 1)
    def _():
        wait_tile(buf)

        @pl.when(n_tiles > 1)
        def _():
            wait_tile(1 - buf)

        pad_copies(wait=True)


def _dispatch(xf, slots, seg_dst, seg_len, pad_dst, pad_len, n_used, n_pad, *, tm, bm):
    t, d = xf.shape
    assert d == SUBLANES * LANES
    n_exp = pad_dst.shape[0]
    grid_spec = pltpu.PrefetchScalarGridSpec(
        num_scalar_prefetch=5,
        grid=(t // tm,),
        in_specs=[
            pl.BlockSpec((tm, d), lambda i, *_: (i, 0)),
            pl.BlockSpec((2, tm), lambda i, *_: (0, i)),
        ],
        out_specs=pl.BlockSpec(memory_space=pl.ANY),
        scratch_shapes=[
            pltpu.VMEM((2, 2 * tm * SUBLANES, LANES), F32),
            pltpu.VMEM((bm * SUBLANES, LANES), F32),
            pltpu.SemaphoreType.DMA((2,)),
            pltpu.SemaphoreType.DMA(()),
        ],
    )
    return pl.pallas_call(
        functools.partial(_dispatch_kernel, tm=tm, n_exp=n_exp, bm=bm),
        grid_spec=grid_spec,
        out_shape=jax.ShapeDtypeStruct((n_pad * SUBLANES, LANES), F32),
        compiler_params=pltpu.CompilerParams(
            dimension_semantics=("arbitrary",), vmem_limit_bytes=VMEM_LIMIT),
        name="dispatch",
    )(seg_dst, seg_len, pad_dst, pad_len, n_used, xf, slots)


def _experts_kernel(first_blk_ref, n_blk_ref, n_used_ref, xs_hbm, w1_ref, w3_ref, w2_ref, y_hbm,
                    w1b, w3b, w2b, xbuf, ybuf, in_sems, out_sems, *, bm, n_exp):
    e = pl.program_id(0)
    first = first_blk_ref[e]
    n = n_blk_ref[e]

    n_x = xbuf.shape[0]
    ahead = n_x - 1

    def x_copy(blk, slot):
        return pltpu.make_async_copy(_token_rows(xs_hbm, blk * bm, bm), xbuf.at[slot], in_sems.at[slot])

    def y_copy(blk, slot):
        return pltpu.make_async_copy(ybuf.at[slot], _token_rows(y_hbm, blk * bm, bm), out_sems.at[slot])

    def fetch_head(first_blk, count):
        for k in range(ahead):
            @pl.when(k < count)
            def _(k=k):
                x_copy(first_blk + k, k).start(BLOCK_DMA_PRIORITY)

    @pl.when(e == 0)
    def _():
        fetch_head(first, n)

    w1b[...] = w1_ref[0, 0].astype(BF16)
    w3b[...] = w3_ref[0, 0].astype(BF16)
    w2b[...] = w2_ref[0, 0].astype(BF16)

    def block(j, carry):
        slot = lax.rem(j, 2)
        x_slot = lax.rem(j, n_x)

        @pl.when(j + ahead < n)
        def _():
            x_copy(first + j + ahead, lax.rem(j + ahead, n_x)).start(BLOCK_DMA_PRIORITY)

        x_copy(first + j, x_slot).wait()

        @pl.when(j >= 2)
        def _():
            y_copy(first + j - 2, slot).wait()

        xb = _load_token_major(xbuf.at[x_slot], bm).astype(BF16)
        h1 = jnp.dot(xb, w1b[...], preferred_element_type=F32)
        h3 = jnp.dot(xb, w3b[...], preferred_element_type=F32)
        hb = (h1 * jax.nn.sigmoid(h1) * h3).astype(BF16)
        _store_token_major(ybuf.at[slot], jnp.dot(hb, w2b[...], preferred_element_type=F32))
        y_copy(first + j, slot).start(BLOCK_DMA_PRIORITY)
        return carry

    lax.fori_loop(0, n, block, 0)

    nxt = jnp.minimum(e + 1, n_exp - 1)

    @pl.when(e + 1 < n_exp)
    def _():
        fetch_head(first_blk_ref[nxt], n_blk_ref[nxt])

    for back in (2, 1):
        @pl.when(n >= back)
        def _(back=back):
            y_copy(first + n - back, lax.rem(n - back, 2)).wait()

    @pl.when(e == n_exp - 1)
    def _():
        ybuf[0] = jnp.zeros(ybuf.shape[1:], F32)
        n_all = y_hbm.shape[0] // (bm * SUBLANES)
        for blk in range(n_all - n_exp, n_all):
            @pl.when(blk >= n_used_ref[0])
            def _(blk=blk):
                y_copy(blk, 0).start(BLOCK_DMA_PRIORITY)
                y_copy(blk, 0).wait()


def _experts(xs, first_blk, n_blk, n_used, w1, w3, w2, layer, *, bm):
    n_exp, d, d_e = w1.shape[-3:]
    assert d == SUBLANES * LANES
    w_blk = lambda e, *_: (layer, e, 0, 0)
    grid_spec = pltpu.PrefetchScalarGridSpec(
        num_scalar_prefetch=3,
        grid=(n_exp,),
        in_specs=[
            pl.BlockSpec(memory_space=pl.ANY),
            pl.BlockSpec((1, 1, d, d_e), w_blk),
            pl.BlockSpec((1, 1, d, d_e), w_blk),
            pl.BlockSpec((1, 1, d_e, d), w_blk),
        ],
        out_specs=pl.BlockSpec(memory_space=pl.ANY),
        scratch_shapes=[
            pltpu.VMEM((d, d_e), BF16), pltpu.VMEM((d, d_e), BF16), pltpu.VMEM((d_e, d), BF16),
            pltpu.VMEM((EXPERT_IN_SLOTS, bm * SUBLANES, LANES), F32), pltpu.VMEM((2, bm * SUBLANES, LANES), F32),
            pltpu.SemaphoreType.DMA((EXPERT_IN_SLOTS,)), pltpu.SemaphoreType.DMA((2,)),
        ],
    )
    return pl.pallas_call(
        functools.partial(_experts_kernel, bm=bm, n_exp=n_exp),
        grid_spec=grid_spec,
        out_shape=jax.ShapeDtypeStruct(xs.shape, F32),
        compiler_params=pltpu.CompilerParams(
            dimension_semantics=("arbitrary",), vmem_limit_bytes=VMEM_LIMIT),
        name="experts",
    )(first_blk, n_blk, n_used, xs, w1, w3, w2)


def _combine_kernel(seg_src_ref, seg_len_ref, x_ref, slot_ref, gate_ref, lng_ref, lnb_ref, y_hbm, o_ref,
                    grouped, sems, *, alpha, tm, n_exp):
    i = pl.program_id(0)
    n_tiles = pl.num_programs(0)
    buf = lax.rem(i, 2)

    def fetch(tile, b):
        start = 0
        for e in range(n_exp):
            length = seg_len_ref[tile * n_exp + e]
            src = seg_src_ref[tile * n_exp + e]
            _segment_copies(lambda lo, n, src=src: _token_rows(y_hbm, src + lo, n),
                            lambda lo, n, start=start: _token_rows(grouped.at[b], start + lo, n),
                            length, tm, sems.at[b])
            start = start + length

    @pl.when(i == 0)
    def _():
        fetch(0, 0)

    @pl.when(i + 1 < n_tiles)
    def _():
        fetch(i + 1, 1 - buf)

    pltpu.make_async_copy(_token_rows(y_hbm, 0, 2 * tm), grouped.at[buf], sems.at[buf]).wait()

    row = lax.broadcasted_iota(I32, (2 * tm, tm), 0)
    g = (jnp.where(row == slot_ref[0:1, :], gate_ref[0:1, :], 0.0)
         + jnp.where(row == slot_ref[1:2, :], gate_ref[1:2, :], 0.0))
    yb = _load_token_major(grouped.at[buf], 2 * tm).astype(BF16)
    f = lax.dot_general(g.astype(BF16), yb, (((0,), (0,)), ((), ())), preferred_element_type=F32)
    o_ref[...] = _layer_norm(alpha * x_ref[...] + f, lng_ref[...], lnb_ref[...])


def _combine(xf, y, slots, gates, seg_src, seg_len, ln_g, ln_b, *, alpha, tm, n_exp):
    t, d = xf.shape
    grid_spec = pltpu.PrefetchScalarGridSpec(
        num_scalar_prefetch=2,
        grid=(t // tm,),
        in_specs=[
            pl.BlockSpec((tm, d), lambda i, *_: (i, 0)),
            pl.BlockSpec((2, tm), lambda i, *_: (0, i)),
            pl.BlockSpec((2, tm), lambda i, *_: (0, i)),
            pl.BlockSpec((1, d), lambda i, *_: (0, 0)),
            pl.BlockSpec((1, d), lambda i, *_: (0, 0)),
            pl.BlockSpec(memory_space=pl.ANY),
        ],
        out_specs=pl.BlockSpec((tm, d), lambda i, *_: (i, 0)),
        scratch_shapes=[pltpu.VMEM((2, 2 * tm * SUBLANES, LANES), F32), pltpu.SemaphoreType.DMA((2,))],
    )
    return pl.pallas_call(
        functools.partial(_combine_kernel, alpha=alpha, tm=tm, n_exp=n_exp),
        grid_spec=grid_spec,
        out_shape=jax.ShapeDtypeStruct((t, d), F32),
        compiler_params=pltpu.CompilerParams(
            dimension_semantics=("arbitrary",), vmem_limit_bytes=VMEM_LIMIT),
        name="combine",
    )(seg_src, seg_len, xf, slots, gates, _row(ln_g), _row(ln_b), y)


def _moe_ln(x, slots, gates, tab, w1, w3, w2, layer, ln_g, ln_b, *, alpha, bm=256):
    bsz, seq, d = x.shape
    t = bsz * seq
    n_exp = tab.shape[1]
    tm = t // tab.shape[0]
    xf = x.reshape(t, d)
    before, in_tile = tab[:, :, 0], tab[:, :, 1]
    counts = before[-1] + in_tile[-1]
    padded = (counts + bm - 1) // bm * bm
    pend = jnp.cumsum(padded)
    pstart = pend - padded
    n_blk = -(-(2 * t) // bm) + n_exp
    n_used = (pend[-1] // bm).reshape(1).astype(I32)
    seg_pos = (pstart[None, :] + before).reshape(-1).astype(I32)
    seg_len = in_tile.reshape(-1).astype(I32)
    xs = _dispatch(xf, slots, seg_pos, seg_len, (pstart + counts).astype(I32), (padded - counts).astype(I32),
                   n_used, n_blk * bm, tm=tm, bm=bm)
    y = _experts(xs, (pstart // bm).astype(I32), (padded // bm).astype(I32), n_used, w1, w3, w2, layer, bm=bm)
    out = _combine(xf, y, slots, gates, seg_pos, seg_len, ln_g, ln_b, alpha=alpha, tm=tm, n_exp=n_exp)
    return out.reshape(bsz, seq, d)


def kernel(x, ln1_g, ln1_b, ln2_g, ln2_b, even_w_in, even_a_dw, even_a_dw_b, even_a_ln_g, even_a_ln_b, even_b_dw, even_w_out, odd_w_in, odd_c_dw, odd_c_dw_b, odd_w_gate_a, odd_b_gate_a, odd_w_gate_x, odd_b_gate_x, odd_lam, odd_w_out, w_router, b_router, moe_w1, moe_w3, moe_w2):
    depth = ln1_g.shape[0]
    alpha = (2.0 * depth) ** 0.25
    for layer in range(depth):
        j = layer // 2
        if layer % 2 == 0:
            x, slots, gates, tab = _mix0(
                x, even_w_in[j], even_a_dw[j], even_a_dw_b[j], even_a_ln_g[j], even_a_ln_b[j],
                even_b_dw[j], even_w_out[j], ln1_g[layer], ln1_b[layer], w_router, b_router, alpha=alpha)
        else:
            x, slots, gates, tab = _mix1(
                x, odd_w_in[j], odd_c_dw[j], odd_c_dw_b[j], odd_w_gate_a[j], odd_b_gate_a[j],
                odd_w_gate_x[j], odd_b_gate_x[j], odd_lam[j], odd_w_out[j], ln1_g[layer], ln1_b[layer],
                w_router, b_router, alpha=alpha)
        x = _moe_ln(x, slots, gates, tab, moe_w1, moe_w3, moe_w2, layer,
                    ln2_g[layer], ln2_b[layer], alpha=alpha)
    return x
```

```python
import functools

import jax
import jax.numpy as jnp
from jax import lax
from jax.experimental import pallas as pl
from jax.experimental.pallas import tpu as pltpu

F32 = jnp.float32
BF16 = jnp.bfloat16
I32 = jnp.int32
LN_EPS = 1e-5
LRU_C = 8.0
SUBLANES = 8
LANES = 128
HALO_A = 32
HALO_B = 8
EXPERTS_PER_GROUP = 4
BLOCK_DMA_PRIORITY = 1
EXPERT_IN_SLOTS = 4
VMEM_LIMIT = 56 * 1024 * 1024


def _layer_norm(v, g, b):
    mu = jnp.mean(v, axis=-1, keepdims=True)
    c = v - mu
    var = jnp.mean(c * c, axis=-1, keepdims=True)
    return c * lax.rsqrt(var + LN_EPS) * g + b


def _row(v):
    return v.reshape(1, -1).astype(F32)


def _tap_rows(w):
    return jnp.broadcast_to(w.astype(F32)[:, None, :], (w.shape[0], SUBLANES, w.shape[1]))


def _tap(w_ref, k, rows):
    return jnp.concatenate([w_ref[k]] * (rows // SUBLANES), axis=0)


def _const_spec(shape):
    return pl.BlockSpec(shape, lambda *_: (0,) * len(shape))


def _mix0_kernel(x_ref, win_ref, adw_ref, adwb_ref, alng_ref, alnb_ref, bdw_ref, wout_ref, lng_ref, lnb_ref,
                 wrt_ref, br_ref, o_ref, slot_ref, gate_ref, tab_ref, *scratch, alpha, ts, tm, d_a, d_b, rc, parts):
    tri, base = scratch[-2:]
    bufs = [scratch[5 * p:5 * p + 5] for p in range(parts)]
    first = (pl.program_id(0) == 0) & (pl.program_id(1) == 0)
    n_a = adw_ref.shape[0]
    n_b = bdw_ref.shape[0]
    hs = ts // parts

    @pl.when(pl.program_id(1) == 0)
    def _():
        bufs[0][0][0:HALO_A, :] = jnp.zeros((HALO_A, d_a), F32)
        bufs[0][2][0:HALO_B, :] = jnp.zeros((HALO_B, d_b), F32)

    x = x_ref[0]
    xb = x.astype(BF16)
    cw = 2 * LANES
    dot = functools.partial(jnp.dot, preferred_element_type=F32)

    def proj_units(p):
        abuf, ash, cbuf, bgate, cat = bufs[p]
        xp = xb[p * hs:(p + 1) * hs, :]
        units = []
        for c0 in range(0, d_a, cw):
            def unit(c0=c0):
                val = dot(xp, win_ref[:, c0:c0 + cw])
                gate = dot(xp, win_ref[:, d_a + c0:d_a + c0 + cw])
                abuf[HALO_A:HALO_A + hs, c0:c0 + cw] = val * jax.nn.sigmoid(gate)
            units.append(unit)
        for c0 in range(0, d_b, cw):
            def unit(c0=c0, o=2 * d_a):
                bgate[:, c0:c0 + cw] = dot(xp, win_ref[:, o + c0:o + c0 + cw])
                cbuf[HALO_B:HALO_B + hs, c0:c0 + cw] = (dot(xp, win_ref[:, o + d_b + c0:o + d_b + c0 + cw])
                                                        * dot(xp, win_ref[:, o + 2 * d_b + c0:o + 2 * d_b + c0 + cw]))
            units.append(unit)
        return units

    def conv_units(p):
        abuf, ash, cbuf, bgate, cat = bufs[p]

        def shifted():
            n_sh = hs + HALO_A - SUBLANES
            for j in range(1, SUBLANES):
                ash[j - 1, :, :] = abuf[j:j + n_sh, :]
        units = [shifted]
        for c in range(hs // rc):
            def unit(r0=c * rc):
                acc = adwb_ref[...]
                for k in range(n_a):
                    q, j = divmod(HALO_A - (n_a - 1) + k, SUBLANES)
                    lo = r0 + q * SUBLANES
                    src = abuf[lo:lo + rc, :] if j == 0 else ash[j - 1, lo:lo + rc, :]
                    acc = acc + _tap(adw_ref, k, rc) * src
                a = _layer_norm(acc, alng_ref[...], alnb_ref[...])
                cat[r0:r0 + rc, 0:d_a] = (a * jax.nn.sigmoid(a)).astype(BF16)
                bb = jnp.zeros((rc, d_b), F32)
                for k in range(n_b):
                    lo = r0 + HALO_B - (n_b - 1) + k
                    bb = bb + _tap(bdw_ref, k, rc) * cbuf[lo:lo + rc, :]
                cat[r0:r0 + rc, d_a:d_a + d_b] = (bgate[r0:r0 + rc, :] * bb).astype(BF16)
            units.append(unit)
        return units

    m_cols = [[] for _ in range(parts)]

    def out_units(p):
        cat = bufs[p][4]
        units = []
        for c0 in range(0, wout_ref.shape[1], cw):
            def unit(c0=c0):
                m_cols[p].append(dot(cat[...], wout_ref[:, c0:c0 + cw]))
            units.append(unit)
        return units

    def run_interleaved(*groups):
        order = sorted(((i + 0.5) / len(g), gi, i) for gi, g in enumerate(groups) for i in range(len(g)))
        for _, gi, i in order:
            groups[gi][i]()

    for s in range(parts + 2):
        groups = []
        if s < parts:
            groups.append(proj_units(s))
        if 0 <= s - 1 < parts:
            groups.append(conv_units(s - 1))
        if 0 <= s - 2 < parts:
            groups.append(out_units(s - 2))
        run_interleaved(*groups)
        if s + 1 < parts:
            bufs[s + 1][0][0:HALO_A, :] = bufs[s][0][hs:hs + HALO_A, :]
            bufs[s + 1][2][0:HALO_B, :] = bufs[s][2][hs:hs + HALO_B, :]

    bufs[0][0][0:HALO_A, :] = bufs[-1][0][hs:hs + HALO_A, :]
    bufs[0][2][0:HALO_B, :] = bufs[-1][2][hs:hs + HALO_B, :]

    m = jnp.concatenate([jnp.concatenate(cols, axis=1) for cols in m_cols], axis=0)
    _ln_and_route(alpha * x + m, lng_ref, lnb_ref, wrt_ref, br_ref, o_ref, slot_ref, gate_ref, tab_ref,
                  tri, base, first, tm=tm)


def _mix0(x, w_in, a_dw, a_dw_b, a_ln_g, a_ln_b, b_dw, w_out, ln_g, ln_b, w_router, b_router,
          *, alpha, ts=512, tm=256, rc=32, parts=2):
    bsz, seq, d = x.shape
    d_a = a_dw.shape[-1]
    d_b = b_dw.shape[-1]
    assert a_dw.shape[0] - 1 <= HALO_A and b_dw.shape[0] - 1 <= HALO_B
    ts = min(ts, seq)
    hs = ts // parts
    rc = min(rc, hs)
    assert seq % ts == 0 and ts % parts == 0 and hs % rc == 0 and rc % (2 * SUBLANES) == 0
    r_in, r_args, r_out_specs, r_out_shape, r_scratch, tm = _router_io(w_router, b_router, bsz, seq, ts, tm)
    kern = functools.partial(_mix0_kernel, alpha=alpha, ts=ts, tm=tm, d_a=d_a, d_b=d_b, rc=rc, parts=parts)
    part_scratch = [
        pltpu.VMEM((HALO_A + hs, d_a), F32),
        pltpu.VMEM((SUBLANES - 1, HALO_A + hs - SUBLANES, d_a), F32),
        pltpu.VMEM((HALO_B + hs, d_b), F32),
        pltpu.VMEM((hs, d_b), F32),
        pltpu.VMEM((hs, d_a + d_b), BF16),
    ]
    return pl.pallas_call(
        kern,
        grid=(bsz, seq // ts),
        in_specs=[
            pl.BlockSpec((1, ts, d), lambda b, s: (b, s, 0)),
            _const_spec(w_in.shape), _const_spec((a_dw.shape[0], SUBLANES, d_a)), _const_spec((1, d_a)),
            _const_spec((1, d_a)), _const_spec((1, d_a)), _const_spec((b_dw.shape[0], SUBLANES, d_b)),
            _const_spec(w_out.shape),
            _const_spec((1, d)), _const_spec((1, d)),
        ] + r_in,
        out_specs=[pl.BlockSpec((1, ts, d), lambda b, s: (b, s, 0))] + r_out_specs,
        out_shape=[jax.ShapeDtypeStruct(x.shape, F32)] + r_out_shape,
        scratch_shapes=part_scratch * parts + r_scratch,
        compiler_params=pltpu.CompilerParams(
            dimension_semantics=("arbitrary", "arbitrary"), vmem_limit_bytes=VMEM_LIMIT),
        name="mix0",
    )(x, w_in.astype(BF16), _tap_rows(a_dw), _row(a_dw_b), _row(a_ln_g), _row(a_ln_b),
      _tap_rows(b_dw), w_out.astype(BF16), _row(ln_g), _row(ln_b), *r_args)


def _mix1_kernel(x_ref, win_ref, cdw_ref, cdwb_ref, wga_ref, bga_ref, wgx_ref, bgx_ref, lam_ref, wout_ref,
                 lng_ref, lnb_ref, wrt_ref, br_ref, o_ref, slot_ref, gate_ref, tab_ref,
                 cbuf, ybr, abuf, bbuf, hcar, tri, base, *, alpha, ts, tm, d_rnn, rc):
    first = (pl.program_id(0) == 0) & (pl.program_id(1) == 0)

    @pl.when(pl.program_id(1) == 0)
    def _():
        cbuf[0:HALO_B, :] = jnp.zeros((HALO_B, d_rnn), F32)
        hcar[...] = jnp.zeros((1, d_rnn), F32)

    x = x_ref[0]
    xb = x.astype(BF16)
    ybr[...] = jnp.dot(xb, win_ref[:, 0:d_rnn], preferred_element_type=F32)
    cbuf[HALO_B:HALO_B + ts, :] = jnp.dot(xb, win_ref[:, d_rnn:2 * d_rnn], preferred_element_type=F32)

    n_heads, blk = wga_ref.shape[0], wga_ref.shape[1]
    n_c = cdw_ref.shape[0]
    neg_c_sp = -LRU_C * jax.nn.softplus(-lam_ref[...])
    for c in range(ts // rc):
        r0 = c * rc
        xr = cdwb_ref[...]
        for k in range(n_c):
            lo = r0 + HALO_B - (n_c - 1) + k
            xr = xr + _tap(cdw_ref, k, rc) * cbuf[lo:lo + rc, :]
        xrb = xr.astype(BF16)
        for h in range(n_heads):
            cs = slice(h * blk, (h + 1) * blk)
            r = jax.nn.sigmoid(jnp.dot(xrb[:, cs], wga_ref[h], preferred_element_type=F32) + bga_ref[:, cs])
            i = jax.nn.sigmoid(jnp.dot(xrb[:, cs], wgx_ref[h], preferred_element_type=F32) + bgx_ref[:, cs])
            log_a = neg_c_sp[:, cs] * r
            abuf[r0:r0 + rc, cs] = jnp.exp(log_a)
            th = jnp.tanh(log_a)
            bbuf[r0:r0 + rc, cs] = jnp.sqrt(-2.0 * th / (1.0 - th)) * (i * xr[:, cs])

    cbuf[0:HALO_B, :] = cbuf[ts:ts + HALO_B, :]

    rowid = lax.broadcasted_iota(I32, (SUBLANES, d_rnn), 0)

    def group(g, h_prev):
        r = pl.multiple_of(g * SUBLANES, SUBLANES)
        a = abuf[pl.ds(r, SUBLANES), :]
        b = bbuf[pl.ds(r, SUBLANES), :]
        step = 1
        while step < SUBLANES:
            keep = rowid >= step
            b = jnp.where(keep, a * pltpu.roll(b, step, axis=0) + b, b)
            a = jnp.where(keep, a * pltpu.roll(a, step, axis=0), a)
            step *= 2
        h = a * h_prev + b
        bbuf[pl.ds(r, SUBLANES), :] = h
        return h[SUBLANES - 1:SUBLANES, :]

    hcar[...] = lax.fori_loop(0, ts // SUBLANES, group, hcar[...], unroll=2)

    out = (jax.nn.gelu(ybr[...]) * bbuf[...]).astype(BF16)
    m = jnp.dot(out, wout_ref[...], preferred_element_type=F32)
    _ln_and_route(alpha * x + m, lng_ref, lnb_ref, wrt_ref, br_ref, o_ref, slot_ref, gate_ref, tab_ref,
                  tri, base, first, tm=tm)


def _mix1(x, w_in, c_dw, c_dw_b, w_gate_a, b_gate_a, w_gate_x, b_gate_x, lam, w_out, ln_g, ln_b,
          w_router, b_router, *, alpha, ts=512, tm=256, rc=64):
    bsz, seq, d = x.shape
    d_rnn = c_dw.shape[-1]
    assert c_dw.shape[0] - 1 <= HALO_B
    ts = min(ts, seq)
    rc = min(rc, ts)
    assert seq % ts == 0 and ts % rc == 0 and ts % SUBLANES == 0
    r_in, r_args, r_out_specs, r_out_shape, r_scratch, tm = _router_io(w_router, b_router, bsz, seq, ts, tm)
    kern = functools.partial(_mix1_kernel, alpha=alpha, ts=ts, tm=tm, d_rnn=d_rnn, rc=rc)
    return pl.pallas_call(
        kern,
        grid=(bsz, seq // ts),
        in_specs=[
            pl.BlockSpec((1, ts, d), lambda b, s: (b, s, 0)),
            _const_spec(w_in.shape), _const_spec((c_dw.shape[0], SUBLANES, d_rnn)), _const_spec((1, d_rnn)),
            _const_spec(w_gate_a.shape), _const_spec((1, d_rnn)),
            _const_spec(w_gate_x.shape), _const_spec((1, d_rnn)),
            _const_spec((1, d_rnn)), _const_spec(w_out.shape), _const_spec((1, d)), _const_spec((1, d)),
        ] + r_in,
        out_specs=[pl.BlockSpec((1, ts, d), lambda b, s: (b, s, 0))] + r_out_specs,
        out_shape=[jax.ShapeDtypeStruct(x.shape, F32)] + r_out_shape,
        scratch_shapes=[
            pltpu.VMEM((HALO_B + ts, d_rnn), F32),
            pltpu.VMEM((ts, d_rnn), F32),
            pltpu.VMEM((ts, d_rnn), F32),
            pltpu.VMEM((ts, d_rnn), F32),
            pltpu.VMEM((1, d_rnn), F32),
        ] + r_scratch,
        compiler_params=pltpu.CompilerParams(
            dimension_semantics=("arbitrary", "arbitrary"), vmem_limit_bytes=VMEM_LIMIT),
        name="mix1",
    )(x, w_in.astype(BF16), _tap_rows(c_dw), _row(c_dw_b), w_gate_a.astype(BF16), _row(b_gate_a),
      w_gate_x.astype(BF16), _row(b_gate_x), _row(lam), w_out.astype(BF16), _row(ln_g), _row(ln_b), *r_args)


def _first_argmax(vals):
    best, idx = vals[0], jnp.zeros(vals[0].shape, I32)
    for j in range(1, len(vals)):
        upd = vals[j] > best
        best = jnp.where(upd, vals[j], best)
        idx = jnp.where(upd, j, idx)
    return best, idx


def _route_tile(x, wrt_ref, br_ref, tri, base, tm):
    n_exp = wrt_ref.shape[0]
    tt = x.shape[0]
    w = wrt_ref[...]
    w_hi = w.astype(BF16)
    w_lo = (w - w_hi.astype(F32)).astype(BF16)
    x_hi = x.astype(BF16)
    x_lo = (x - x_hi.astype(F32)).astype(BF16)
    over_d = (((1,), (1,)), ((), ()))
    logits = (lax.dot_general(w_hi, x_hi, over_d, preferred_element_type=F32)
              + lax.dot_general(w_hi, x_lo, over_d, preferred_element_type=F32)
              + lax.dot_general(w_lo, x_hi, over_d, preferred_element_type=F32))
    logits = logits + br_ref[:, 0:1]
    e = jnp.exp(logits - jnp.max(logits, axis=0, keepdims=True))
    p = e / jnp.sum(e, axis=0, keepdims=True)
    rows = [p[j:j + 1, :] for j in range(n_exp)]

    n_grp = n_exp // EXPERTS_PER_GROUP
    scores = []
    for g in range(n_grp):
        a, b, c, d = rows[EXPERTS_PER_GROUP * g:EXPERTS_PER_GROUP * (g + 1)]
        scores.append(jnp.maximum(jnp.maximum(a, b) + jnp.maximum(c, d), jnp.maximum(a + b, c + d)))
    _, g_sel = _first_argmax(scores)

    p_in = []
    for j in range(EXPERTS_PER_GROUP):
        v = rows[j]
        for g in range(1, n_grp):
            v = jnp.where(g_sel == g, rows[EXPERTS_PER_GROUP * g + j], v)
        p_in.append(v)
    v1, i1 = _first_argmax(p_in)
    v2, i2 = _first_argmax([jnp.where(i1 == j, -1.0, p_in[j]) for j in range(EXPERTS_PER_GROUP)])
    den = v1 + v2
    e1 = g_sel * EXPERTS_PER_GROUP + i1
    e2 = g_sel * EXPERTS_PER_GROUP + i2

    eid = lax.broadcasted_iota(I32, (n_exp, tt), 0)
    hit1 = eid == e1
    hit2 = eid == e2
    member = jnp.where(hit1 | hit2, 1.0, 0.0)
    member_b = member.astype(BF16)
    lower = (lax.broadcasted_iota(I32, (n_exp, n_exp), 1) < lax.broadcasted_iota(I32, (n_exp, n_exp), 0))
    lower = jnp.where(lower, 1.0, 0.0).astype(BF16)
    lane = lax.broadcasted_iota(I32, (n_exp, LANES), 1)
    place, tabs = [], []
    for h in range(tt // tm):
        cols = slice(h * tm, (h + 1) * tm)
        incl = jnp.dot(member_b[:, cols], tri[...], preferred_element_type=F32)
        count = jnp.sum(member[:, cols], axis=1, keepdims=True)
        offset = jnp.dot(lower, jnp.broadcast_to(count, (n_exp, LANES)).astype(BF16),
                         preferred_element_type=F32)[:, 0:1]
        place.append(offset + incl - member[:, cols])
        tabs.append(jnp.where(lane == 0, base, jnp.where(lane == 1, count, 0.0)).astype(I32))
        base = base + count
    place = jnp.concatenate(place, axis=1)
    s1 = jnp.sum(jnp.where(hit1, place, 0.0), axis=0, keepdims=True)
    s2 = jnp.sum(jnp.where(hit2, place, 0.0), axis=0, keepdims=True)
    return (s1.astype(I32), s2.astype(I32)), (v1 / den, v2 / den), tabs, base


def _ln_and_route(v, lng_ref, lnb_ref, wrt_ref, br_ref, o_ref, slot_ref, gate_ref, tab_ref, tri, base, first, *, tm):
    @pl.when(first)
    def _():
        src = lax.broadcasted_iota(I32, (tm, tm), 0)
        dst = lax.broadcasted_iota(I32, (tm, tm), 1)
        tri[...] = jnp.where(src <= dst, 1.0, 0.0).astype(BF16)
        base[...] = jnp.zeros(base.shape, F32)

    x1 = _layer_norm(v, lng_ref[...], lnb_ref[...])
    o_ref[0] = x1
    slots, gates, tabs, new_base = _route_tile(x1, wrt_ref, br_ref, tri, base[...], tm)
    for k in range(2):
        slot_ref[k:k + 1, :] = slots[k]
        gate_ref[k:k + 1, :] = gates[k]
    for h, tab in enumerate(tabs):
        tab_ref[h] = tab
    base[...] = new_base


def _router_io(w_router, b_router, bsz, seq, ts, tm):
    d, n_exp = w_router.shape
    tm = min(tm, ts)
    assert ts % tm == 0 and n_exp % EXPERTS_PER_GROUP == 0
    t = bsz * seq
    per_seq = seq // ts
    in_specs = [_const_spec((n_exp, d)), _const_spec((n_exp, LANES))]
    args = (w_router.T.astype(F32), jnp.broadcast_to(b_router.astype(F32)[:, None], (n_exp, LANES)))
    out_specs = [
        pl.BlockSpec((2, ts), lambda b, s: (0, b * per_seq + s)),
        pl.BlockSpec((2, ts), lambda b, s: (0, b * per_seq + s)),
        pl.BlockSpec((ts // tm, n_exp, LANES), lambda b, s: (b * per_seq + s, 0, 0)),
    ]
    out_shape = [
        jax.ShapeDtypeStruct((2, t), I32),
        jax.ShapeDtypeStruct((2, t), F32),
        jax.ShapeDtypeStruct((t // tm, n_exp, LANES), I32),
    ]
    scratch = [pltpu.VMEM((tm, tm), BF16), pltpu.VMEM((n_exp, LANES), F32)]
    return in_specs, args, out_specs, out_shape, scratch, tm


def _segment_copies(src_of, dst_of, length, max_len, sem, *, wait=False):
    done = 0
    for bit in reversed(range(max_len.bit_length())):
        n = 1 << bit
        piece = length & n

        @pl.when(piece != 0)
        def _(done=done, n=n):
            copy = pltpu.make_async_copy(src_of(done, n), dst_of(done, n), sem)
            copy.wait() if wait else copy.start()

        done = done + piece


def _token_rows(ref, lo, n):
    return ref.at[pl.ds(pl.multiple_of(lo * SUBLANES, SUBLANES), n * SUBLANES)]


def _store_token_major(ref, val):
    for j in range(SUBLANES):
        ref[pl.ds(j, val.shape[0], stride=SUBLANES), :] = val[:, j * LANES:(j + 1) * LANES]


def _load_token_major(ref, rows):
    return jnp.concatenate([ref[pl.ds(j, rows, stride=SUBLANES), :] for j in range(SUBLANES)], axis=1)


def _dispatch_kernel(seg_dst_ref, seg_len_ref, pad_dst_ref, pad_len_ref, n_used_ref, x_ref, slot_ref, xs_hbm,
                     grouped, zeros, sems, pad_sem, *, tm, n_exp, bm):
    i = pl.program_id(0)
    n_tiles = pl.num_programs(0)
    buf = lax.rem(i, 2)

    def wait_tile(b):
        pltpu.make_async_copy(grouped.at[b], _token_rows(xs_hbm, 0, 2 * tm), sems.at[b]).wait()

    def pad_copies(wait):
        for e in range(n_exp):
            _segment_copies(lambda lo, n: _token_rows(zeros, lo, n),
                            lambda lo, n, e=e: _token_rows(xs_hbm, pad_dst_ref[e] + lo, n),
                            pad_len_ref[e], bm - 1, pad_sem, wait=wait)
        n_blk = xs_hbm.shape[0] // (bm * SUBLANES)
        for blk in range(n_blk - n_exp, n_blk):
            @pl.when(blk >= n_used_ref[0])
            def _(blk=blk):
                copy = pltpu.make_async_copy(zeros, _token_rows(xs_hbm, blk * bm, bm), pad_sem)
                copy.wait() if wait else copy.start()

    @pl.when(i == 0)
    def _():
        zeros[...] = jnp.zeros(zeros.shape, F32)
        pad_copies(wait=False)

    @pl.when(i >= 2)
    def _():
        wait_tile(buf)

    row = lax.broadcasted_iota(I32, (2 * tm, tm), 0)
    onehot = jnp.where((row == slot_ref[0:1, :]) | (row == slot_ref[1:2, :]), 1.0, 0.0).astype(BF16)
    _store_token_major(grouped.at[buf], jnp.dot(onehot, x_ref[...].astype(BF16), preferred_element_type=F32))

    start = 0
    for e in range(n_exp):
        length = seg_len_ref[i * n_exp + e]
        dst = seg_dst_ref[i * n_exp + e]
        _segment_copies(lambda lo, n, start=start: _token_rows(grouped.at[buf], start + lo, n),
                        lambda lo, n, dst=dst: _token_rows(xs_hbm, dst + lo, n),
                        length, tm, sems.at[buf])
        start = start + length

    @pl.when(i == n_tiles - 1)
    def _():
        wait_tile(buf)

        @pl.when(n_tiles > 1)
        def _():
            wait_tile(1 - buf)

        pad_copies(wait=True)


def _dispatch(xf, slots, seg_dst, seg_len, pad_dst, pad_len, n_used, n_pad, *, tm, bm):
    t, d = xf.shape
    assert d == SUBLANES * LANES
    n_exp = pad_dst.shape[0]
    grid_spec = pltpu.PrefetchScalarGridSpec(
        num_scalar_prefetch=5,
        grid=(t // tm,),
        in_specs=[
            pl.BlockSpec((tm, d), lambda i, *_: (i, 0)),
            pl.BlockSpec((2, tm), lambda i, *_: (0, i)),
        ],
        out_specs=pl.BlockSpec(memory_space=pl.ANY),
        scratch_shapes=[
            pltpu.VMEM((2, 2 * tm * SUBLANES, LANES), F32),
            pltpu.VMEM((bm * SUBLANES, LANES), F32),
            pltpu.SemaphoreType.DMA((2,)),
            pltpu.SemaphoreType.DMA(()),
        ],
    )
    return pl.pallas_call(
        functools.partial(_dispatch_kernel, tm=tm, n_exp=n_exp, bm=bm),
        grid_spec=grid_spec,
        out_shape=jax.ShapeDtypeStruct((n_pad * SUBLANES, LANES), F32),
        compiler_params=pltpu.CompilerParams(
            dimension_semantics=("arbitrary",), vmem_limit_bytes=VMEM_LIMIT),
        name="dispatch",
    )(seg_dst, seg_len, pad_dst, pad_len, n_used, xf, slots)


def _experts_kernel(first_blk_ref, n_blk_ref, n_used_ref, xs_hbm, w1_ref, w3_ref, w2_ref, y_hbm,
                    w1b, w3b, w2b, xbuf, ybuf, in_sems, out_sems, *, bm, n_exp):
    e = pl.program_id(0)
    first = first_blk_ref[e]
    n = n_blk_ref[e]

    n_x = xbuf.shape[0]
    ahead = n_x - 1

    def x_copy(blk, slot):
        return pltpu.make_async_copy(_token_rows(xs_hbm, blk * bm, bm), xbuf.at[slot], in_sems.at[slot])

    def y_copy(blk, slot):
        return pltpu.make_async_copy(ybuf.at[slot], _token_rows(y_hbm, blk * bm, bm), out_sems.at[slot])

    def fetch_head(first_blk, count):
        for k in range(ahead):
            @pl.when(k < count)
            def _(k=k):
                x_copy(first_blk + k, k).start(BLOCK_DMA_PRIORITY)

    @pl.when(e == 0)
    def _():
        fetch_head(first, n)

    w1b[...] = w1_ref[0, 0].astype(BF16)
    w3b[...] = w3_ref[0, 0].astype(BF16)
    w2b[...] = w2_ref[0, 0].astype(BF16)

    def block(j, carry):
        slot = lax.rem(j, 2)
        x_slot = lax.rem(j, n_x)

        @pl.when(j + ahead < n)
        def _():
            x_copy(first + j + ahead, lax.rem(j + ahead, n_x)).start(BLOCK_DMA_PRIORITY)

        x_copy(first + j, x_slot).wait()

        @pl.when(j >= 2)
        def _():
            y_copy(first + j - 2, slot).wait()

        xb = _load_token_major(xbuf.at[x_slot], bm).astype(BF16)
        h1 = jnp.dot(xb, w1b[...], preferred_element_type=F32)
        h3 = jnp.dot(xb, w3b[...], preferred_element_type=F32)
        hb = (h1 * jax.nn.sigmoid(h1) * h3).astype(BF16)
        _store_token_major(ybuf.at[slot], jnp.dot(hb, w2b[...], preferred_element_type=F32))
        y_copy(first + j, slot).start(BLOCK_DMA_PRIORITY)
        return carry

    lax.fori_loop(0, n, block, 0)

    nxt = jnp.minimum(e + 1, n_exp - 1)

    @pl.when(e + 1 < n_exp)
    def _():
        fetch_head(first_blk_ref[nxt], n_blk_ref[nxt])

    for back in (2, 1):
        @pl.when(n >= back)
        def _(back=back):
            y_copy(first + n - back, lax.rem(n - back, 2)).wait()

    @pl.when(e == n_exp - 1)
    def _():
        ybuf[0] = jnp.zeros(ybuf.shape[1:], F32)
        n_all = y_hbm.shape[0] // (bm * SUBLANES)
        for blk in range(n_all - n_exp, n_all):
            @pl.when(blk >= n_used_ref[0])
            def _(blk=blk):
                y_copy(blk, 0).start(BLOCK_DMA_PRIORITY)
                y_copy(blk, 0).wait()


def _experts(xs, first_blk, n_blk, n_used, w1, w3, w2, layer, *, bm):
    n_exp, d, d_e = w1.shape[-3:]
    assert d == SUBLANES * LANES
    w_blk = lambda e, *_: (layer, e, 0, 0)
    grid_spec = pltpu.PrefetchScalarGridSpec(
        num_scalar_prefetch=3,
        grid=(n_exp,),
        in_specs=[
            pl.BlockSpec(memory_space=pl.ANY),
            pl.BlockSpec((1, 1, d, d_e), w_blk),
            pl.BlockSpec((1, 1, d, d_e), w_blk),
            pl.BlockSpec((1, 1, d_e, d), w_blk),
        ],
        out_specs=pl.BlockSpec(memory_space=pl.ANY),
        scratch_shapes=[
            pltpu.VMEM((d, d_e), BF16), pltpu.VMEM((d, d_e), BF16), pltpu.VMEM((d_e, d), BF16),
            pltpu.VMEM((EXPERT_IN_SLOTS, bm * SUBLANES, LANES), F32), pltpu.VMEM((2, bm * SUBLANES, LANES), F32),
            pltpu.SemaphoreType.DMA((EXPERT_IN_SLOTS,)), pltpu.SemaphoreType.DMA((2,)),
        ],
    )
    return pl.pallas_call(
        functools.partial(_experts_kernel, bm=bm, n_exp=n_exp),
        grid_spec=grid_spec,
        out_shape=jax.ShapeDtypeStruct(xs.shape, F32),
        compiler_params=pltpu.CompilerParams(
            dimension_semantics=("arbitrary",), vmem_limit_bytes=VMEM_LIMIT),
        name="experts",
    )(first_blk, n_blk, n_used, xs, w1, w3, w2)


def _combine_kernel(seg_src_ref, seg_len_ref, x_ref, slot_ref, gate_ref, lng_ref, lnb_ref, y_hbm, o_ref,
                    grouped, sems, *, alpha, tm, n_exp):
    i = pl.program_id(0)
    n_tiles = pl.num_programs(0)
    buf = lax.rem(i, 2)

    def fetch(tile, b):
        start = 0
        for e in range(n_exp):
            length = seg_len_ref[tile * n_exp + e]
            src = seg_src_ref[tile * n_exp + e]
            _segment_copies(lambda lo, n, src=src: _token_rows(y_hbm, src + lo, n),
                            lambda lo, n, start=start: _token_rows(grouped.at[b], start + lo, n),
                            length, tm, sems.at[b])
            start = start + length

    @pl.when(i == 0)
    def _():
        fetch(0, 0)

    @pl.when(i + 1 < n_tiles)
    def _():
        fetch(i + 1, 1 - buf)

    pltpu.make_async_copy(_token_rows(y_hbm, 0, 2 * tm), grouped.at[buf], sems.at[buf]).wait()

    row = lax.broadcasted_iota(I32, (2 * tm, tm), 0)
    g = (jnp.where(row == slot_ref[0:1, :], gate_ref[0:1, :], 0.0)
         + jnp.where(row == slot_ref[1:2, :], gate_ref[1:2, :], 0.0))
    yb = _load_token_major(grouped.at[buf], 2 * tm).astype(BF16)
    f = lax.dot_general(g.astype(BF16), yb, (((0,), (0,)), ((), ())), preferred_element_type=F32)
    o_ref[...] = _layer_norm(alpha * x_ref[...] + f, lng_ref[...], lnb_ref[...])


def _combine(xf, y, slots, gates, seg_src, seg_len, ln_g, ln_b, *, alpha, tm, n_exp):
    t, d = xf.shape
    grid_spec = pltpu.PrefetchScalarGridSpec(
        num_scalar_prefetch=2,
        grid=(t // tm,),
        in_specs=[
            pl.BlockSpec((tm, d), lambda i, *_: (i, 0)),
            pl.BlockSpec((2, tm), lambda i, *_: (0, i)),
            pl.BlockSpec((2, tm), lambda i, *_: (0, i)),
            pl.BlockSpec((1, d), lambda i, *_: (0, 0)),
            pl.BlockSpec((1, d), lambda i, *_: (0, 0)),
            pl.BlockSpec(memory_space=pl.ANY),
        ],
        out_specs=pl.BlockSpec((tm, d), lambda i, *_: (i, 0)),
        scratch_shapes=[pltpu.VMEM((2, 2 * tm * SUBLANES, LANES), F32), pltpu.SemaphoreType.DMA((2,))],
    )
    return pl.pallas_call(
        functools.partial(_combine_kernel, alpha=alpha, tm=tm, n_exp=n_exp),
        grid_spec=grid_spec,
        out_shape=jax.ShapeDtypeStruct((t, d), F32),
        compiler_params=pltpu.CompilerParams(
            dimension_semantics=("arbitrary",), vmem_limit_bytes=VMEM_LIMIT),
        name="combine",
    )(seg_src, seg_len, xf, slots, gates, _row(ln_g), _row(ln_b), y)


def _moe_ln(x, slots, gates, tab, w1, w3, w2, layer, ln_g, ln_b, *, alpha, bm=256):
    bsz, seq, d = x.shape
    t = bsz * seq
    n_exp = tab.shape[1]
    tm = t // tab.shape[0]
    xf = x.reshape(t, d)
    before, in_tile = tab[:, :, 0], tab[:, :, 1]
    counts = before[-1] + in_tile[-1]
    padded = (counts + bm - 1) // bm * bm
    pend = jnp.cumsum(padded)
    pstart = pend - padded
    n_blk = -(-(2 * t) // bm) + n_exp
    n_used = (pend[-1] // bm).reshape(1).astype(I32)
    seg_pos = (pstart[None, :] + before).reshape(-1).astype(I32)
    seg_len = in_tile.reshape(-1).astype(I32)
    xs = _dispatch(xf, slots, seg_pos, seg_len, (pstart + counts).astype(I32), (padded - counts).astype(I32),
                   n_used, n_blk * bm, tm=tm, bm=bm)
    y = _experts(xs, (pstart // bm).astype(I32), (padded // bm).astype(I32), n_used, w1, w3, w2, layer, bm=bm)
    out = _combine(xf, y, slots, gates, seg_pos, seg_len, ln_g, ln_b, alpha=alpha, tm=tm, n_exp=n_exp)
    return out.reshape(bsz, seq, d)


def kernel(x, ln1_g, ln1_b, ln2_g, ln2_b, even_w_in, even_a_dw, even_a_dw_b, even_a_ln_g, even_a_ln_b, even_b_dw, even_w_out, odd_w_in, odd_c_dw, odd_c_dw_b, odd_w_gate_a, odd_b_gate_a, odd_w_gate_x, odd_b_gate_x, odd_lam, odd_w_out, w_router, b_router, moe_w1, moe_w3, moe_w2):
    depth = ln1_g.shape[0]
    alpha = (2.0 * depth) ** 0.25
    for layer in range(depth):
        j = layer // 2
        if layer % 2 == 0:
            x, slots, gates, tab = _mix0(
                x, even_w_in[j], even_a_dw[j], even_a_dw_b[j], even_a_ln_g[j], even_a_ln_b[j],
                even_b_dw[j], even_w_out[j], ln1_g[layer], ln1_b[layer], w_router, b_router, alpha=alpha)
        else:
            x, slots, gates, tab = _mix1(
                x, odd_w_in[j], odd_c_dw[j], odd_c_dw_b[j], odd_w_gate_a[j], odd_b_gate_a[j],
                odd_w_gate_x[j], odd_b_gate_x[j], odd_lam[j], odd_w_out[j], ln1_g[layer], ln1_b[layer],
                w_router, b_router, alpha=alpha)
        x = _moe_ln(x, slots, gates, tab, moe_w1, moe_w3, moe_w2, layer,
                    ln2_g[layer], ln2_b[layer], alpha=alpha)
    return x
```

```python
import functools
import math

import jax
import jax.numpy as jnp
from jax import lax
from jax.experimental import pallas as pl
from jax.experimental.pallas import tpu as pltpu

F32 = jnp.float32
BF16 = jnp.bfloat16
I32 = jnp.int32
LN_EPS = 1e-5
LRU_C = 8.0
SUBLANES = 8
LANES = 128
HALO_A = 32
HALO_B = 8
EXPERTS_PER_GROUP = 4
BLOCK_DMA_PRIORITY = 1
TILES_PER_STEP = 4
COMBINE_SLOTS = 3
EXPERT_IN_SLOTS = 4
VMEM_LIMIT = 56 * 1024 * 1024


def _layer_norm(v, g, b):
    mu = jnp.mean(v, axis=-1, keepdims=True)
    c = v - mu
    var = jnp.mean(c * c, axis=-1, keepdims=True)
    return c * lax.rsqrt(var + LN_EPS) * g + b


def _row(v):
    return v.reshape(1, -1).astype(F32)


def _tap_rows(w):
    return jnp.broadcast_to(w.astype(F32)[:, None, :], (w.shape[0], SUBLANES, w.shape[1]))


def _tap(w_ref, k, rows):
    return jnp.concatenate([w_ref[k]] * (rows // SUBLANES), axis=0)


def _const_spec(shape):
    return pl.BlockSpec(shape, lambda *_: (0,) * len(shape))


def _mix0_kernel(x_ref, win_ref, adw_ref, adwb_ref, alng_ref, alnb_ref, bdw_ref, wout_ref, lng_ref, lnb_ref,
                 wrt_ref, br_ref, o_ref, slot_ref, gate_ref, tab_ref, *scratch, alpha, ts, tm, d_a, d_b, rc, parts):
    tri, base = scratch[-2:]
    bufs = [scratch[5 * p:5 * p + 5] for p in range(parts)]
    first = (pl.program_id(0) == 0) & (pl.program_id(1) == 0)
    n_a = adw_ref.shape[0]
    n_b = bdw_ref.shape[0]
    hs = ts // parts

    @pl.when(pl.program_id(1) == 0)
    def _():
        bufs[0][0][0:HALO_A, :] = jnp.zeros((HALO_A, d_a), F32)
        bufs[0][2][0:HALO_B, :] = jnp.zeros((HALO_B, d_b), F32)

    x = x_ref[0]
    xb = x.astype(BF16)
    cw = 2 * LANES
    dot = functools.partial(jnp.dot, preferred_element_type=F32)

    def proj_units(p):
        abuf, ash, cbuf, bgate, cat = bufs[p]
        xp = xb[p * hs:(p + 1) * hs, :]
        units = []
        for c0 in range(0, d_a, cw):
            def unit(c0=c0):
                val = dot(xp, win_ref[:, c0:c0 + cw])
                gate = dot(xp, win_ref[:, d_a + c0:d_a + c0 + cw])
                abuf[HALO_A:HALO_A + hs, c0:c0 + cw] = val * jax.nn.sigmoid(gate)
            units.append(unit)
        for c0 in range(0, d_b, cw):
            def unit(c0=c0, o=2 * d_a):
                bgate[:, c0:c0 + cw] = dot(xp, win_ref[:, o + c0:o + c0 + cw])
                cbuf[HALO_B:HALO_B + hs, c0:c0 + cw] = (dot(xp, win_ref[:, o + d_b + c0:o + d_b + c0 + cw])
                                                        * dot(xp, win_ref[:, o + 2 * d_b + c0:o + 2 * d_b + c0 + cw]))
            units.append(unit)
        return units

    def conv_units(p):
        abuf, ash, cbuf, bgate, cat = bufs[p]

        def shifted():
            n_sh = hs + HALO_A - SUBLANES
            for j in range(1, SUBLANES):
                ash[j - 1, :, :] = abuf[j:j + n_sh, :]
        units = [shifted]
        for c in range(hs // rc):
            def unit(r0=c * rc):
                acc = adwb_ref[...]
                for k in range(n_a):
                    q, j = divmod(HALO_A - (n_a - 1) + k, SUBLANES)
                    lo = r0 + q * SUBLANES
                    src = abuf[lo:lo + rc, :] if j == 0 else ash[j - 1, lo:lo + rc, :]
                    acc = acc + _tap(adw_ref, k, rc) * src
                a = _layer_norm(acc, alng_ref[...], alnb_ref[...])
                cat[r0:r0 + rc, 0:d_a] = (a * jax.nn.sigmoid(a)).astype(BF16)
                bb = jnp.zeros((rc, d_b), F32)
                for k in range(n_b):
                    lo = r0 + HALO_B - (n_b - 1) + k
                    bb = bb + _tap(bdw_ref, k, rc) * cbuf[lo:lo + rc, :]
                cat[r0:r0 + rc, d_a:d_a + d_b] = (bgate[r0:r0 + rc, :] * bb).astype(BF16)
            units.append(unit)
        return units

    m_cols = [[] for _ in range(parts)]

    def out_units(p):
        cat = bufs[p][4]
        units = []
        for c0 in range(0, wout_ref.shape[1], cw):
            def unit(c0=c0):
                m_cols[p].append(dot(cat[...], wout_ref[:, c0:c0 + cw]))
            units.append(unit)
        return units

    def run_interleaved(*groups):
        order = sorted(((i + 0.5) / len(g), gi, i) for gi, g in enumerate(groups) for i in range(len(g)))
        for _, gi, i in order:
            groups[gi][i]()

    for s in range(parts + 2):
        groups = []
        if s < parts:
            groups.append(proj_units(s))
        if 0 <= s - 1 < parts:
            groups.append(conv_units(s - 1))
        if 0 <= s - 2 < parts:
            groups.append(out_units(s - 2))
        run_interleaved(*groups)
        if s + 1 < parts:
            bufs[s + 1][0][0:HALO_A, :] = bufs[s][0][hs:hs + HALO_A, :]
            bufs[s + 1][2][0:HALO_B, :] = bufs[s][2][hs:hs + HALO_B, :]

    bufs[0][0][0:HALO_A, :] = bufs[-1][0][hs:hs + HALO_A, :]
    bufs[0][2][0:HALO_B, :] = bufs[-1][2][hs:hs + HALO_B, :]

    m = jnp.concatenate([jnp.concatenate(cols, axis=1) for cols in m_cols], axis=0)
    _ln_and_route(alpha * x + m, lng_ref, lnb_ref, wrt_ref, br_ref, o_ref, slot_ref, gate_ref, tab_ref,
                  tri, base, first, tm=tm)


def _mix0(x, w_in, a_dw, a_dw_b, a_ln_g, a_ln_b, b_dw, w_out, ln_g, ln_b, w_router, b_router,
          *, alpha, ts=512, tm=256, rc=32, parts=2):
    bsz, seq, d = x.shape
    d_a = a_dw.shape[-1]
    d_b = b_dw.shape[-1]
    assert a_dw.shape[0] - 1 <= HALO_A and b_dw.shape[0] - 1 <= HALO_B
    ts = min(ts, seq)
    hs = ts // parts
    rc = min(rc, hs)
    assert seq % ts == 0 and ts % parts == 0 and hs % rc == 0 and rc % (2 * SUBLANES) == 0
    r_in, r_args, r_out_specs, r_out_shape, r_scratch, tm = _router_io(w_router, b_router, bsz, seq, ts, tm)
    kern = functools.partial(_mix0_kernel, alpha=alpha, ts=ts, tm=tm, d_a=d_a, d_b=d_b, rc=rc, parts=parts)
    part_scratch = [
        pltpu.VMEM((HALO_A + hs, d_a), F32),
        pltpu.VMEM((SUBLANES - 1, HALO_A + hs - SUBLANES, d_a), F32),
        pltpu.VMEM((HALO_B + hs, d_b), F32),
        pltpu.VMEM((hs, d_b), F32),
        pltpu.VMEM((hs, d_a + d_b), BF16),
    ]
    return pl.pallas_call(
        kern,
        grid=(bsz, seq // ts),
        in_specs=[
            pl.BlockSpec((1, ts, d), lambda b, s: (b, s, 0)),
            _const_spec(w_in.shape), _const_spec((a_dw.shape[0], SUBLANES, d_a)), _const_spec((1, d_a)),
            _const_spec((1, d_a)), _const_spec((1, d_a)), _const_spec((b_dw.shape[0], SUBLANES, d_b)),
            _const_spec(w_out.shape),
            _const_spec((1, d)), _const_spec((1, d)),
        ] + r_in,
        out_specs=[pl.BlockSpec((1, ts, d), lambda b, s: (b, s, 0))] + r_out_specs,
        out_shape=[jax.ShapeDtypeStruct(x.shape, F32)] + r_out_shape,
        scratch_shapes=part_scratch * parts + r_scratch,
        compiler_params=pltpu.CompilerParams(
            dimension_semantics=("arbitrary", "arbitrary"), vmem_limit_bytes=VMEM_LIMIT),
        name="mix0",
    )(x, w_in.astype(BF16), _tap_rows(a_dw), _row(a_dw_b), _row(a_ln_g), _row(a_ln_b),
      _tap_rows(b_dw), w_out.astype(BF16), _row(ln_g), _row(ln_b), *r_args)


def _mix1_kernel(x_ref, win_ref, cdw_ref, cdwb_ref, wga_ref, bga_ref, wgx_ref, bgx_ref, lam_ref, wout_ref,
                 lng_ref, lnb_ref, wrt_ref, br_ref, o_ref, slot_ref, gate_ref, tab_ref,
                 cbuf, ybr, abuf, bbuf, hcar, tri, base, *, alpha, ts, tm, d_rnn, rc):
    first = (pl.program_id(0) == 0) & (pl.program_id(1) == 0)

    @pl.when(pl.program_id(1) == 0)
    def _():
        cbuf[0:HALO_B, :] = jnp.zeros((HALO_B, d_rnn), F32)
        hcar[...] = jnp.zeros((1, d_rnn), F32)

    x = x_ref[0]
    xb = x.astype(BF16)
    ybr[...] = jnp.dot(xb, win_ref[:, 0:d_rnn], preferred_element_type=F32)
    cbuf[HALO_B:HALO_B + ts, :] = jnp.dot(xb, win_ref[:, d_rnn:2 * d_rnn], preferred_element_type=F32)

    n_heads, blk = wga_ref.shape[0], wga_ref.shape[1]
    n_c = cdw_ref.shape[0]
    neg_c_sp = -LRU_C * jax.nn.softplus(-lam_ref[...])
    for c in range(ts // rc):
        r0 = c * rc
        xr = cdwb_ref[...]
        for k in range(n_c):
            lo = r0 + HALO_B - (n_c - 1) + k
            xr = xr + _tap(cdw_ref, k, rc) * cbuf[lo:lo + rc, :]
        xrb = xr.astype(BF16)
        for h in range(n_heads):
            cs = slice(h * blk, (h + 1) * blk)
            r = jax.nn.sigmoid(jnp.dot(xrb[:, cs], wga_ref[h], preferred_element_type=F32) + bga_ref[:, cs])
            i = jax.nn.sigmoid(jnp.dot(xrb[:, cs], wgx_ref[h], preferred_element_type=F32) + bgx_ref[:, cs])
            log_a = neg_c_sp[:, cs] * r
            abuf[r0:r0 + rc, cs] = jnp.exp(log_a)
            th = jnp.tanh(log_a)
            bbuf[r0:r0 + rc, cs] = jnp.sqrt(-2.0 * th / (1.0 - th)) * (i * xr[:, cs])

    cbuf[0:HALO_B, :] = cbuf[ts:ts + HALO_B, :]

    rowid = lax.broadcasted_iota(I32, (SUBLANES, d_rnn), 0)

    def group(g, h_prev):
        r = pl.multiple_of(g * SUBLANES, SUBLANES)
        a = abuf[pl.ds(r, SUBLANES), :]
        b = bbuf[pl.ds(r, SUBLANES), :]
        step = 1
        while step < SUBLANES:
            keep = rowid >= step
            b = jnp.where(keep, a * pltpu.roll(b, step, axis=0) + b, b)
            a = jnp.where(keep, a * pltpu.roll(a, step, axis=0), a)
            step *= 2
        h = a * h_prev + b
        bbuf[pl.ds(r, SUBLANES), :] = h
        return h[SUBLANES - 1:SUBLANES, :]

    hcar[...] = lax.fori_loop(0, ts // SUBLANES, group, hcar[...], unroll=2)

    out = (jax.nn.gelu(ybr[...]) * bbuf[...]).astype(BF16)
    m = jnp.dot(out, wout_ref[...], preferred_element_type=F32)
    _ln_and_route(alpha * x + m, lng_ref, lnb_ref, wrt_ref, br_ref, o_ref, slot_ref, gate_ref, tab_ref,
                  tri, base, first, tm=tm)


def _mix1(x, w_in, c_dw, c_dw_b, w_gate_a, b_gate_a, w_gate_x, b_gate_x, lam, w_out, ln_g, ln_b,
          w_router, b_router, *, alpha, ts=512, tm=256, rc=64):
    bsz, seq, d = x.shape
    d_rnn = c_dw.shape[-1]
    assert c_dw.shape[0] - 1 <= HALO_B
    ts = min(ts, seq)
    rc = min(rc, ts)
    assert seq % ts == 0 and ts % rc == 0 and ts % SUBLANES == 0
    r_in, r_args, r_out_specs, r_out_shape, r_scratch, tm = _router_io(w_router, b_router, bsz, seq, ts, tm)
    kern = functools.partial(_mix1_kernel, alpha=alpha, ts=ts, tm=tm, d_rnn=d_rnn, rc=rc)
    return pl.pallas_call(
        kern,
        grid=(bsz, seq // ts),
        in_specs=[
            pl.BlockSpec((1, ts, d), lambda b, s: (b, s, 0)),
            _const_spec(w_in.shape), _const_spec((c_dw.shape[0], SUBLANES, d_rnn)), _const_spec((1, d_rnn)),
            _const_spec(w_gate_a.shape), _const_spec((1, d_rnn)),
            _const_spec(w_gate_x.shape), _const_spec((1, d_rnn)),
            _const_spec((1, d_rnn)), _const_spec(w_out.shape), _const_spec((1, d)), _const_spec((1, d)),
        ] + r_in,
        out_specs=[pl.BlockSpec((1, ts, d), lambda b, s: (b, s, 0))] + r_out_specs,
        out_shape=[jax.ShapeDtypeStruct(x.shape, F32)] + r_out_shape,
        scratch_shapes=[
            pltpu.VMEM((HALO_B + ts, d_rnn), F32),
            pltpu.VMEM((ts, d_rnn), F32),
            pltpu.VMEM((ts, d_rnn), F32),
            pltpu.VMEM((ts, d_rnn), F32),
            pltpu.VMEM((1, d_rnn), F32),
        ] + r_scratch,
        compiler_params=pltpu.CompilerParams(
            dimension_semantics=("arbitrary", "arbitrary"), vmem_limit_bytes=VMEM_LIMIT),
        name="mix1",
    )(x, w_in.astype(BF16), _tap_rows(c_dw), _row(c_dw_b), w_gate_a.astype(BF16), _row(b_gate_a),
      w_gate_x.astype(BF16), _row(b_gate_x), _row(lam), w_out.astype(BF16), _row(ln_g), _row(ln_b), *r_args)


def _first_argmax(vals):
    best, idx = vals[0], jnp.zeros(vals[0].shape, I32)
    for j in range(1, len(vals)):
        upd = vals[j] > best
        best = jnp.where(upd, vals[j], best)
        idx = jnp.where(upd, j, idx)
    return best, idx


def _route_tile(x, wrt_ref, br_ref, tri, base, tm):
    n_exp = wrt_ref.shape[0]
    tt = x.shape[0]
    w = wrt_ref[...]
    w_hi = w.astype(BF16)
    w_lo = (w - w_hi.astype(F32)).astype(BF16)
    x_hi = x.astype(BF16)
    x_lo = (x - x_hi.astype(F32)).astype(BF16)
    over_d = (((1,), (1,)), ((), ()))
    logits = (lax.dot_general(w_hi, x_hi, over_d, preferred_element_type=F32)
              + lax.dot_general(w_hi, x_lo, over_d, preferred_element_type=F32)
              + lax.dot_general(w_lo, x_hi, over_d, preferred_element_type=F32))
    logits = logits + br_ref[:, 0:1]
    e = jnp.exp(logits - jnp.max(logits, axis=0, keepdims=True))
    p = e / jnp.sum(e, axis=0, keepdims=True)
    rows = [p[j:j + 1, :] for j in range(n_exp)]

    n_grp = n_exp // EXPERTS_PER_GROUP
    scores = []
    for g in range(n_grp):
        a, b, c, d = rows[EXPERTS_PER_GROUP * g:EXPERTS_PER_GROUP * (g + 1)]
        scores.append(jnp.maximum(jnp.maximum(a, b) + jnp.maximum(c, d), jnp.maximum(a + b, c + d)))
    _, g_sel = _first_argmax(scores)

    p_in = []
    for j in range(EXPERTS_PER_GROUP):
        v = rows[j]
        for g in range(1, n_grp):
            v = jnp.where(g_sel == g, rows[EXPERTS_PER_GROUP * g + j], v)
        p_in.append(v)
    v1, i1 = _first_argmax(p_in)
    v2, i2 = _first_argmax([jnp.where(i1 == j, -1.0, p_in[j]) for j in range(EXPERTS_PER_GROUP)])
    den = v1 + v2
    e1 = g_sel * EXPERTS_PER_GROUP + i1
    e2 = g_sel * EXPERTS_PER_GROUP + i2

    eid = lax.broadcasted_iota(I32, (n_exp, tt), 0)
    hit1 = eid == e1
    hit2 = eid == e2
    member = jnp.where(hit1 | hit2, 1.0, 0.0)
    member_b = member.astype(BF16)
    lower = (lax.broadcasted_iota(I32, (n_exp, n_exp), 1) < lax.broadcasted_iota(I32, (n_exp, n_exp), 0))
    lower = jnp.where(lower, 1.0, 0.0).astype(BF16)
    lane = lax.broadcasted_iota(I32, (n_exp, LANES), 1)
    place, tabs = [], []
    for h in range(tt // tm):
        cols = slice(h * tm, (h + 1) * tm)
        incl = jnp.dot(member_b[:, cols], tri[...], preferred_element_type=F32)
        count = jnp.sum(member[:, cols], axis=1, keepdims=True)
        offset = jnp.dot(lower, jnp.broadcast_to(count, (n_exp, LANES)).astype(BF16),
                         preferred_element_type=F32)[:, 0:1]
        place.append(offset + incl - member[:, cols])
        tabs.append(jnp.where(lane == 0, base, jnp.where(lane == 1, count, 0.0)).astype(I32))
        base = base + count
    place = jnp.concatenate(place, axis=1)
    s1 = jnp.sum(jnp.where(hit1, place, 0.0), axis=0, keepdims=True)
    s2 = jnp.sum(jnp.where(hit2, place, 0.0), axis=0, keepdims=True)
    return (s1.astype(I32), s2.astype(I32)), (v1 / den, v2 / den), tabs, base


def _ln_and_route(v, lng_ref, lnb_ref, wrt_ref, br_ref, o_ref, slot_ref, gate_ref, tab_ref, tri, base, first, *, tm):
    @pl.when(first)
    def _():
        src = lax.broadcasted_iota(I32, (tm, tm), 0)
        dst = lax.broadcasted_iota(I32, (tm, tm), 1)
        tri[...] = jnp.where(src <= dst, 1.0, 0.0).astype(BF16)
        base[...] = jnp.zeros(base.shape, F32)

    x1 = _layer_norm(v, lng_ref[...], lnb_ref[...])
    o_ref[0] = x1
    slots, gates, tabs, new_base = _route_tile(x1, wrt_ref, br_ref, tri, base[...], tm)
    for h, tab in enumerate(tabs):
        for k in range(2):
            slot_ref[h, k:k + 1, :] = slots[k][:, h * tm:(h + 1) * tm]
            gate_ref[h, k:k + 1, :] = gates[k][:, h * tm:(h + 1) * tm]
        tab_ref[h] = tab
    base[...] = new_base


def _router_io(w_router, b_router, bsz, seq, ts, tm):
    d, n_exp = w_router.shape
    tm = min(tm, ts)
    assert ts % tm == 0 and n_exp % EXPERTS_PER_GROUP == 0
    t = bsz * seq
    per_seq = seq // ts
    in_specs = [_const_spec((n_exp, d)), _const_spec((n_exp, LANES))]
    args = (w_router.T.astype(F32), jnp.broadcast_to(b_router.astype(F32)[:, None], (n_exp, LANES)))
    per_tile = lambda b, s: (b * per_seq + s, 0, 0)
    out_specs = [
        pl.BlockSpec((ts // tm, 2, tm), per_tile),
        pl.BlockSpec((ts // tm, 2, tm), per_tile),
        pl.BlockSpec((ts // tm, n_exp, LANES), per_tile),
    ]
    out_shape = [
        jax.ShapeDtypeStruct((t // tm, 2, tm), I32),
        jax.ShapeDtypeStruct((t // tm, 2, tm), F32),
        jax.ShapeDtypeStruct((t // tm, n_exp, LANES), I32),
    ]
    scratch = [pltpu.VMEM((tm, tm), BF16), pltpu.VMEM((n_exp, LANES), F32)]
    return in_specs, args, out_specs, out_shape, scratch, tm


def _segment_copies(src_of, dst_of, length, max_len, sem, *, wait=False, priority=0):
    done = 0
    for bit in reversed(range(max_len.bit_length())):
        n = 1 << bit
        piece = length & n

        @pl.when(piece != 0)
        def _(done=done, n=n):
            copy = pltpu.make_async_copy(src_of(done, n), dst_of(done, n), sem)
            copy.wait() if wait else copy.start(priority)

        done = done + piece


def _token_rows(ref, lo, n):
    return ref.at[pl.ds(pl.multiple_of(lo * SUBLANES, SUBLANES), n * SUBLANES)]


def _store_token_major(ref, val):
    for j in range(SUBLANES):
        ref[pl.ds(j, val.shape[0], stride=SUBLANES), :] = val[:, j * LANES:(j + 1) * LANES]


def _load_token_major(ref, rows):
    return jnp.concatenate([ref[pl.ds(j, rows, stride=SUBLANES), :] for j in range(SUBLANES)], axis=1)


def _dispatch_kernel(seg_dst_ref, seg_len_ref, pad_dst_ref, pad_len_ref, n_used_ref, x_ref, slot_ref, xs_hbm,
                     grouped, zeros, sems, pad_sem, *, tm, n_exp, bm, group):
    i = pl.program_id(0)
    n_tiles = pl.num_programs(0) * group

    def wait_tile(b):
        pltpu.make_async_copy(grouped.at[b], _token_rows(xs_hbm, 0, 2 * tm), sems.at[b]).wait()

    def pad_copies(wait):
        for e in range(n_exp):
            _segment_copies(lambda lo, n: _token_rows(zeros, lo, n),
                            lambda lo, n, e=e: _token_rows(xs_hbm, pad_dst_ref[e] + lo, n),
                            pad_len_ref[e], bm - 1, pad_sem, wait=wait)
        n_blk = xs_hbm.shape[0] // (bm * SUBLANES)
        for blk in range(n_blk - n_exp, n_blk):
            @pl.when(blk >= n_used_ref[0])
            def _(blk=blk):
                copy = pltpu.make_async_copy(zeros, _token_rows(xs_hbm, blk * bm, bm), pad_sem)
                copy.wait() if wait else copy.start()

    @pl.when(i == 0)
    def _():
        zeros[...] = jnp.zeros(zeros.shape, F32)
        pad_copies(wait=False)

    def tile(h, carry):
        k = i * group + h
        buf = lax.rem(k, 2)

        @pl.when(k >= 2)
        def _():
            wait_tile(buf)

        slots = slot_ref[h]
        row = lax.broadcasted_iota(I32, (2 * tm, tm), 0)
        onehot = jnp.where((row == slots[0:1, :]) | (row == slots[1:2, :]), 1.0, 0.0).astype(BF16)
        xt = x_ref[pl.ds(pl.multiple_of(h * tm, tm), tm), :].astype(BF16)
        _store_token_major(grouped.at[buf], jnp.dot(onehot, xt, preferred_element_type=F32))

        start = 0
        for e in range(n_exp):
            length = seg_len_ref[k * n_exp + e]
            dst = seg_dst_ref[k * n_exp + e]
            _segment_copies(lambda lo, n, start=start: _token_rows(grouped.at[buf], start + lo, n),
                            lambda lo, n, dst=dst: _token_rows(xs_hbm, dst + lo, n),
                            length, tm, sems.at[buf])
            start = start + length
        return carry

    lax.fori_loop(0, group, tile, 0)

    @pl.when(i == pl.num_programs(0) - 1)
    def _():
        wait_tile(lax.rem(n_tiles - 1, 2))

        @pl.when(n_tiles > 1)
        def _():
            wait_tile(lax.rem(n_tiles, 2))

        pad_copies(wait=True)


def _dispatch(xf, slots, seg_dst, seg_len, pad_dst, pad_len, n_used, n_pad, *, tm, bm, group):
    t, d = xf.shape
    assert d == SUBLANES * LANES and t % (tm * group) == 0
    n_exp = pad_dst.shape[0]
    grid_spec = pltpu.PrefetchScalarGridSpec(
        num_scalar_prefetch=5,
        grid=(t // (tm * group),),
        in_specs=[
            pl.BlockSpec((group * tm, d), lambda i, *_: (i, 0)),
            pl.BlockSpec((group, 2, tm), lambda i, *_: (i, 0, 0)),
        ],
        out_specs=pl.BlockSpec(memory_space=pl.ANY),
        scratch_shapes=[
            pltpu.VMEM((2, 2 * tm * SUBLANES, LANES), F32),
            pltpu.VMEM((bm * SUBLANES, LANES), F32),
            pltpu.SemaphoreType.DMA((2,)),
            pltpu.SemaphoreType.DMA(()),
        ],
    )
    return pl.pallas_call(
        functools.partial(_dispatch_kernel, tm=tm, n_exp=n_exp, bm=bm, group=group),
        grid_spec=grid_spec,
        out_shape=jax.ShapeDtypeStruct((n_pad * SUBLANES, LANES), F32),
        compiler_params=pltpu.CompilerParams(
            dimension_semantics=("arbitrary",), vmem_limit_bytes=VMEM_LIMIT),
        name="dispatch",
    )(seg_dst, seg_len, pad_dst, pad_len, n_used, xf, slots)


def _experts_kernel(first_blk_ref, n_blk_ref, n_used_ref, xs_hbm, w1_ref, w3_ref, w2_ref, y_hbm,
                    w1b, w3b, w2b, xbuf, ybuf, in_sems, out_sems, *, bm, n_exp):
    e = pl.program_id(0)
    first = first_blk_ref[e]
    n = n_blk_ref[e]

    n_x = xbuf.shape[0]
    ahead = n_x - 1

    def x_copy(blk, slot):
        return pltpu.make_async_copy(_token_rows(xs_hbm, blk * bm, bm), xbuf.at[slot], in_sems.at[slot])

    def y_copy(blk, slot):
        return pltpu.make_async_copy(ybuf.at[slot], _token_rows(y_hbm, blk * bm, bm), out_sems.at[slot])

    def fetch_head(first_blk, count):
        for k in range(ahead):
            @pl.when(k < count)
            def _(k=k):
                x_copy(first_blk + k, k).start(BLOCK_DMA_PRIORITY)

    @pl.when(e == 0)
    def _():
        fetch_head(first, n)

    w1b[...] = w1_ref[0, 0].astype(BF16)
    w3b[...] = w3_ref[0, 0].astype(BF16)
    w2b[...] = w2_ref[0, 0].astype(BF16)

    def block(j, carry):
        slot = lax.rem(j, 2)
        x_slot = lax.rem(j, n_x)

        @pl.when(j + ahead < n)
        def _():
            x_copy(first + j + ahead, lax.rem(j + ahead, n_x)).start(BLOCK_DMA_PRIORITY)

        x_copy(first + j, x_slot).wait()

        @pl.when(j >= 2)
        def _():
            y_copy(first + j - 2, slot).wait()

        xb = _load_token_major(xbuf.at[x_slot], bm).astype(BF16)
        h1 = jnp.dot(xb, w1b[...], preferred_element_type=F32)
        h3 = jnp.dot(xb, w3b[...], preferred_element_type=F32)
        hb = (h1 * jax.nn.sigmoid(h1) * h3).astype(BF16)
        _store_token_major(ybuf.at[slot], jnp.dot(hb, w2b[...], preferred_element_type=F32))
        y_copy(first + j, slot).start(BLOCK_DMA_PRIORITY)
        return carry

    lax.fori_loop(0, n, block, 0)

    nxt = jnp.minimum(e + 1, n_exp - 1)

    @pl.when(e + 1 < n_exp)
    def _():
        fetch_head(first_blk_ref[nxt], n_blk_ref[nxt])

    for back in (2, 1):
        @pl.when(n >= back)
        def _(back=back):
            y_copy(first + n - back, lax.rem(n - back, 2)).wait()

    @pl.when(e == n_exp - 1)
    def _():
        ybuf[0] = jnp.zeros(ybuf.shape[1:], F32)
        n_all = y_hbm.shape[0] // (bm * SUBLANES)
        for blk in range(n_all - n_exp, n_all):
            @pl.when(blk >= n_used_ref[0])
            def _(blk=blk):
                y_copy(blk, 0).start(BLOCK_DMA_PRIORITY)
                y_copy(blk, 0).wait()


def _experts(xs, first_blk, n_blk, n_used, w1, w3, w2, layer, *, bm):
    n_exp, d, d_e = w1.shape[-3:]
    assert d == SUBLANES * LANES
    w_blk = lambda e, *_: (layer, e, 0, 0)
    grid_spec = pltpu.PrefetchScalarGridSpec(
        num_scalar_prefetch=3,
        grid=(n_exp,),
        in_specs=[
            pl.BlockSpec(memory_space=pl.ANY),
            pl.BlockSpec((1, 1, d, d_e), w_blk),
            pl.BlockSpec((1, 1, d, d_e), w_blk),
            pl.BlockSpec((1, 1, d_e, d), w_blk),
        ],
        out_specs=pl.BlockSpec(memory_space=pl.ANY),
        scratch_shapes=[
            pltpu.VMEM((d, d_e), BF16), pltpu.VMEM((d, d_e), BF16), pltpu.VMEM((d_e, d), BF16),
            pltpu.VMEM((EXPERT_IN_SLOTS, bm * SUBLANES, LANES), F32), pltpu.VMEM((2, bm * SUBLANES, LANES), F32),
            pltpu.SemaphoreType.DMA((EXPERT_IN_SLOTS,)), pltpu.SemaphoreType.DMA((2,)),
        ],
    )
    return pl.pallas_call(
        functools.partial(_experts_kernel, bm=bm, n_exp=n_exp),
        grid_spec=grid_spec,
        out_shape=jax.ShapeDtypeStruct(xs.shape, F32),
        compiler_params=pltpu.CompilerParams(
            dimension_semantics=("arbitrary",), vmem_limit_bytes=VMEM_LIMIT),
        name="experts",
    )(first_blk, n_blk, n_used, xs, w1, w3, w2)


def _combine_kernel(seg_src_ref, seg_len_ref, x_ref, slot_ref, gate_ref, lng_ref, lnb_ref, y_hbm, o_ref,
                    grouped, sems, *, alpha, tm, n_exp, group):
    i = pl.program_id(0)
    n_tiles = pl.num_programs(0) * group

    def fetch(tile, b):
        start = 0
        for e in range(n_exp):
            length = seg_len_ref[tile * n_exp + e]
            src = seg_src_ref[tile * n_exp + e]
            _segment_copies(lambda lo, n, src=src: _token_rows(y_hbm, src + lo, n),
                            lambda lo, n, start=start: _token_rows(grouped.at[b], start + lo, n),
                            length, tm, sems.at[b], priority=BLOCK_DMA_PRIORITY)
            start = start + length

    n_buf = grouped.shape[0]
    ahead = n_buf - 1

    @pl.when(i == 0)
    def _():
        for k0 in range(ahead):
            @pl.when(k0 < n_tiles)
            def _(k0=k0):
                fetch(k0, k0)

    def tile(h, carry):
        k = i * group + h
        buf = lax.rem(k, n_buf)

        @pl.when(k + ahead < n_tiles)
        def _():
            fetch(k + ahead, lax.rem(k + ahead, n_buf))

        pltpu.make_async_copy(_token_rows(y_hbm, 0, 2 * tm), grouped.at[buf], sems.at[buf]).wait()

        slots = slot_ref[h]
        gates = gate_ref[h]
        row = lax.broadcasted_iota(I32, (2 * tm, tm), 0)
        g = (jnp.where(row == slots[0:1, :], gates[0:1, :], 0.0)
             + jnp.where(row == slots[1:2, :], gates[1:2, :], 0.0))
        yb = _load_token_major(grouped.at[buf], 2 * tm).astype(BF16)
        f = lax.dot_general(g.astype(BF16), yb, (((0,), (0,)), ((), ())), preferred_element_type=F32)
        rows = pl.ds(pl.multiple_of(h * tm, tm), tm)
        o_ref[rows, :] = _layer_norm(alpha * x_ref[rows, :] + f, lng_ref[...], lnb_ref[...])
        return carry

    lax.fori_loop(0, group, tile, 0)


def _combine(xf, y, slots, gates, seg_src, seg_len, ln_g, ln_b, *, alpha, tm, n_exp, group):
    t, d = xf.shape
    assert t % (tm * group) == 0
    grid_spec = pltpu.PrefetchScalarGridSpec(
        num_scalar_prefetch=2,
        grid=(t // (tm * group),),
        in_specs=[
            pl.BlockSpec((group * tm, d), lambda i, *_: (i, 0)),
            pl.BlockSpec((group, 2, tm), lambda i, *_: (i, 0, 0)),
            pl.BlockSpec((group, 2, tm), lambda i, *_: (i, 0, 0)),
            pl.BlockSpec((1, d), lambda i, *_: (0, 0)),
            pl.BlockSpec((1, d), lambda i, *_: (0, 0)),
            pl.BlockSpec(memory_space=pl.ANY),
        ],
        out_specs=pl.BlockSpec((group * tm, d), lambda i, *_: (i, 0)),
        scratch_shapes=[pltpu.VMEM((COMBINE_SLOTS, 2 * tm * SUBLANES, LANES), F32),
                        pltpu.SemaphoreType.DMA((COMBINE_SLOTS,))],
    )
    return pl.pallas_call(
        functools.partial(_combine_kernel, alpha=alpha, tm=tm, n_exp=n_exp, group=group),
        grid_spec=grid_spec,
        out_shape=jax.ShapeDtypeStruct((t, d), F32),
        compiler_params=pltpu.CompilerParams(
            dimension_semantics=("arbitrary",), vmem_limit_bytes=VMEM_LIMIT),
        name="combine",
    )(seg_src, seg_len, xf, slots, gates, _row(ln_g), _row(ln_b), y)


def _moe_ln(x, slots, gates, tab, w1, w3, w2, layer, ln_g, ln_b, *, alpha, bm=256):
    bsz, seq, d = x.shape
    t = bsz * seq
    n_exp = tab.shape[1]
    tm = t // tab.shape[0]
    xf = x.reshape(t, d)
    before, in_tile = tab[:, :, 0], tab[:, :, 1]
    counts = before[-1] + in_tile[-1]
    padded = (counts + bm - 1) // bm * bm
    pend = jnp.cumsum(padded)
    pstart = pend - padded
    n_blk = -(-(2 * t) // bm) + n_exp
    n_used = (pend[-1] // bm).reshape(1).astype(I32)
    seg_pos = (pstart[None, :] + before).reshape(-1).astype(I32)
    seg_len = in_tile.reshape(-1).astype(I32)
    group = math.gcd(TILES_PER_STEP, t // tm)
    xs = _dispatch(xf, slots, seg_pos, seg_len, (pstart + counts).astype(I32), (padded - counts).astype(I32),
                   n_used, n_blk * bm, tm=tm, bm=bm, group=group)
    y = _experts(xs, (pstart // bm).astype(I32), (padded // bm).astype(I32), n_used, w1, w3, w2, layer, bm=bm)
    out = _combine(xf, y, slots, gates, seg_pos, seg_len, ln_g, ln_b, alpha=alpha, tm=tm, n_exp=n_exp, group=group)
    return out.reshape(bsz, seq, d)


def kernel(x, ln1_g, ln1_b, ln2_g, ln2_b, even_w_in, even_a_dw, even_a_dw_b, even_a_ln_g, even_a_ln_b, even_b_dw, even_w_out, odd_w_in, odd_c_dw, odd_c_dw_b, odd_w_gate_a, odd_b_gate_a, odd_w_gate_x, odd_b_gate_x, odd_lam, odd_w_out, w_router, b_router, moe_w1, moe_w3, moe_w2):
    depth = ln1_g.shape[0]
    alpha = (2.0 * depth) ** 0.25
    for layer in range(depth):
        j = layer // 2
        if layer % 2 == 0:
            x, slots, gates, tab = _mix0(
                x, even_w_in[j], even_a_dw[j], even_a_dw_b[j], even_a_ln_g[j], even_a_ln_b[j],
                even_b_dw[j], even_w_out[j], ln1_g[layer], ln1_b[layer], w_router, b_router, alpha=alpha)
        else:
            x, slots, gates, tab = _mix1(
                x, odd_w_in[j], odd_c_dw[j], odd_c_dw_b[j], odd_w_gate_a[j], odd_b_gate_a[j],
                odd_w_gate_x[j], odd_b_gate_x[j], odd_lam[j], odd_w_out[j], ln1_g[layer], ln1_b[layer],
                w_router, b_router, alpha=alpha)
        x = _moe_ln(x, slots, gates, tab, moe_w1, moe_w3, moe_w2, layer,
                    ln2_g[layer], ln2_b[layer], alpha=alpha)
    return x
```

```python
import functools
import math

import jax
import jax.numpy as jnp
from jax import lax
from jax.experimental import pallas as pl
from jax.experimental.pallas import tpu as pltpu

F32 = jnp.float32
BF16 = jnp.bfloat16
I32 = jnp.int32
LN_EPS = 1e-5
LRU_C = 8.0
SUBLANES = 8
LANES = 128
HALO_A = 32
HALO_B = 8
EXPERTS_PER_GROUP = 4
BLOCK_DMA_PRIORITY = 1
TILES_PER_STEP = 4
COMBINE_SLOTS = 3
EXPERT_IN_SLOTS = 4
SEQ_TILE = 512
TOKEN_TILE = 256
EXPERT_BLOCK = 256
VMEM_LIMIT = 56 * 1024 * 1024


def _layer_norm(v, g, b):
    mu = jnp.mean(v, axis=-1, keepdims=True)
    c = v - mu
    var = jnp.mean(c * c, axis=-1, keepdims=True)
    return c * lax.rsqrt(var + LN_EPS) * g + b


def _row(v):
    return v.reshape(1, -1).astype(F32)


def _tap_rows(w):
    return jnp.broadcast_to(w.astype(F32)[:, None, :], (w.shape[0], SUBLANES, w.shape[1]))


def _tap(w_ref, k, rows):
    return jnp.concatenate([w_ref[k]] * (rows // SUBLANES), axis=0)


def _const_spec(shape):
    return pl.BlockSpec(shape, lambda *_: (0,) * len(shape))


def _mix0_kernel(x_ref, win_ref, adw_ref, adwb_ref, alng_ref, alnb_ref, bdw_ref, wout_ref, lng_ref, lnb_ref,
                 wrt_ref, br_ref, o_ref, slot_ref, gate_ref, tab_ref, *scratch, alpha, ts, tm, d_a, d_b, rc, parts):
    tri, base = scratch[-2:]
    bufs = [scratch[5 * p:5 * p + 5] for p in range(parts)]
    first = (pl.program_id(0) == 0) & (pl.program_id(1) == 0)
    n_a = adw_ref.shape[0]
    n_b = bdw_ref.shape[0]
    hs = ts // parts

    @pl.when(pl.program_id(1) == 0)
    def _():
        bufs[0][0][0:HALO_A, :] = jnp.zeros((HALO_A, d_a), F32)
        bufs[0][2][0:HALO_B, :] = jnp.zeros((HALO_B, d_b), F32)

    x = x_ref[0]
    xb = x.astype(BF16)
    cw = 2 * LANES
    dot = functools.partial(jnp.dot, preferred_element_type=F32)

    def proj_units(p):
        abuf, ash, cbuf, bgate, cat = bufs[p]
        xp = xb[p * hs:(p + 1) * hs, :]
        units = []
        for c0 in range(0, d_a, cw):
            def unit(c0=c0):
                val = dot(xp, win_ref[:, c0:c0 + cw])
                gate = dot(xp, win_ref[:, d_a + c0:d_a + c0 + cw])
                abuf[HALO_A:HALO_A + hs, c0:c0 + cw] = val * jax.nn.sigmoid(gate)
            units.append(unit)
        for c0 in range(0, d_b, cw):
            def unit(c0=c0, o=2 * d_a):
                bgate[:, c0:c0 + cw] = dot(xp, win_ref[:, o + c0:o + c0 + cw])
                cbuf[HALO_B:HALO_B + hs, c0:c0 + cw] = (dot(xp, win_ref[:, o + d_b + c0:o + d_b + c0 + cw])
                                                        * dot(xp, win_ref[:, o + 2 * d_b + c0:o + 2 * d_b + c0 + cw]))
            units.append(unit)
        return units

    def conv_units(p):
        abuf, ash, cbuf, bgate, cat = bufs[p]

        def shifted():
            n_sh = hs + HALO_A - SUBLANES
            for j in range(1, SUBLANES):
                ash[j - 1, :, :] = abuf[j:j + n_sh, :]
        units = [shifted]
        for c in range(hs // rc):
            def unit(r0=c * rc):
                acc = adwb_ref[...]
                for k in range(n_a):
                    q, j = divmod(HALO_A - (n_a - 1) + k, SUBLANES)
                    lo = r0 + q * SUBLANES
                    src = abuf[lo:lo + rc, :] if j == 0 else ash[j - 1, lo:lo + rc, :]
                    acc = acc + _tap(adw_ref, k, rc) * src
                a = _layer_norm(acc, alng_ref[...], alnb_ref[...])
                cat[r0:r0 + rc, 0:d_a] = (a * jax.nn.sigmoid(a)).astype(BF16)
                bb = jnp.zeros((rc, d_b), F32)
                for k in range(n_b):
                    lo = r0 + HALO_B - (n_b - 1) + k
                    bb = bb + _tap(bdw_ref, k, rc) * cbuf[lo:lo + rc, :]
                cat[r0:r0 + rc, d_a:d_a + d_b] = (bgate[r0:r0 + rc, :] * bb).astype(BF16)
            units.append(unit)
        return units

    m_cols = [[] for _ in range(parts)]

    def out_units(p):
        cat = bufs[p][4]
        units = []
        for c0 in range(0, wout_ref.shape[1], cw):
            def unit(c0=c0):
                m_cols[p].append(dot(cat[...], wout_ref[:, c0:c0 + cw]))
            units.append(unit)
        return units

    def run_interleaved(*groups):
        order = sorted(((i + 0.5) / len(g), gi, i) for gi, g in enumerate(groups) for i in range(len(g)))
        for _, gi, i in order:
            groups[gi][i]()

    for s in range(parts + 2):
        groups = []
        if s < parts:
            groups.append(proj_units(s))
        if 0 <= s - 1 < parts:
            groups.append(conv_units(s - 1))
        if 0 <= s - 2 < parts:
            groups.append(out_units(s - 2))
        run_interleaved(*groups)
        if s + 1 < parts:
            bufs[s + 1][0][0:HALO_A, :] = bufs[s][0][hs:hs + HALO_A, :]
            bufs[s + 1][2][0:HALO_B, :] = bufs[s][2][hs:hs + HALO_B, :]

    bufs[0][0][0:HALO_A, :] = bufs[-1][0][hs:hs + HALO_A, :]
    bufs[0][2][0:HALO_B, :] = bufs[-1][2][hs:hs + HALO_B, :]

    m = jnp.concatenate([jnp.concatenate(cols, axis=1) for cols in m_cols], axis=0)
    _ln_and_route(alpha * x + m, lng_ref, lnb_ref, wrt_ref, br_ref, o_ref, slot_ref, gate_ref, tab_ref,
                  tri, base, first, tm=tm)


def _mix0(x, w_in, a_dw, a_dw_b, a_ln_g, a_ln_b, b_dw, w_out, ln_g, ln_b, w_router, b_router,
          *, alpha, ts=SEQ_TILE, tm=TOKEN_TILE, rc=32, parts=2):
    bsz, seq, d = x.shape
    d_a = a_dw.shape[-1]
    d_b = b_dw.shape[-1]
    assert a_dw.shape[0] - 1 <= HALO_A and b_dw.shape[0] - 1 <= HALO_B
    ts = min(ts, seq)
    hs = ts // parts
    rc = min(rc, hs)
    assert seq % ts == 0 and ts % parts == 0 and hs % rc == 0 and rc % (2 * SUBLANES) == 0
    r_in, r_args, r_out_specs, r_out_shape, r_scratch, tm = _router_io(w_router, b_router, bsz, seq, ts, tm)
    kern = functools.partial(_mix0_kernel, alpha=alpha, ts=ts, tm=tm, d_a=d_a, d_b=d_b, rc=rc, parts=parts)
    part_scratch = [
        pltpu.VMEM((HALO_A + hs, d_a), F32),
        pltpu.VMEM((SUBLANES - 1, HALO_A + hs - SUBLANES, d_a), F32),
        pltpu.VMEM((HALO_B + hs, d_b), F32),
        pltpu.VMEM((hs, d_b), F32),
        pltpu.VMEM((hs, d_a + d_b), BF16),
    ]
    return pl.pallas_call(
        kern,
        grid=(bsz, seq // ts),
        in_specs=[
            pl.BlockSpec((1, ts, d), lambda b, s: (b, s, 0)),
            _const_spec(w_in.shape), _const_spec((a_dw.shape[0], SUBLANES, d_a)), _const_spec((1, d_a)),
            _const_spec((1, d_a)), _const_spec((1, d_a)), _const_spec((b_dw.shape[0], SUBLANES, d_b)),
            _const_spec(w_out.shape),
            _const_spec((1, d)), _const_spec((1, d)),
        ] + r_in,
        out_specs=[pl.BlockSpec((1, ts, d), lambda b, s: (b, s, 0))] + r_out_specs,
        out_shape=[jax.ShapeDtypeStruct(x.shape, F32)] + r_out_shape,
        scratch_shapes=part_scratch * parts + r_scratch,
        compiler_params=pltpu.CompilerParams(
            dimension_semantics=("arbitrary", "arbitrary"), vmem_limit_bytes=VMEM_LIMIT),
        name="mix0",
    )(x, w_in.astype(BF16), _tap_rows(a_dw), _row(a_dw_b), _row(a_ln_g), _row(a_ln_b),
      _tap_rows(b_dw), w_out.astype(BF16), _row(ln_g), _row(ln_b), *r_args)


def _mix1_kernel(x_ref, win_ref, cdw_ref, cdwb_ref, wga_ref, bga_ref, wgx_ref, bgx_ref, lam_ref, wout_ref,
                 lng_ref, lnb_ref, wrt_ref, br_ref, o_ref, slot_ref, gate_ref, tab_ref,
                 cbuf, ybr, abuf, bbuf, hcar, tri, base, *, alpha, ts, tm, d_rnn, rc):
    first = (pl.program_id(0) == 0) & (pl.program_id(1) == 0)

    @pl.when(pl.program_id(1) == 0)
    def _():
        cbuf[0:HALO_B, :] = jnp.zeros((HALO_B, d_rnn), F32)
        hcar[...] = jnp.zeros((1, d_rnn), F32)

    x = x_ref[0]
    xb = x.astype(BF16)
    ybr[...] = jnp.dot(xb, win_ref[:, 0:d_rnn], preferred_element_type=F32)
    cbuf[HALO_B:HALO_B + ts, :] = jnp.dot(xb, win_ref[:, d_rnn:2 * d_rnn], preferred_element_type=F32)

    n_heads, blk = wga_ref.shape[0], wga_ref.shape[1]
    n_c = cdw_ref.shape[0]
    neg_c_sp = -LRU_C * jax.nn.softplus(-lam_ref[...])
    for c in range(ts // rc):
        r0 = c * rc
        xr = cdwb_ref[...]
        for k in range(n_c):
            lo = r0 + HALO_B - (n_c - 1) + k
            xr = xr + _tap(cdw_ref, k, rc) * cbuf[lo:lo + rc, :]
        xrb = xr.astype(BF16)
        for h in range(n_heads):
            cs = slice(h * blk, (h + 1) * blk)
            r = jax.nn.sigmoid(jnp.dot(xrb[:, cs], wga_ref[h], preferred_element_type=F32) + bga_ref[:, cs])
            i = jax.nn.sigmoid(jnp.dot(xrb[:, cs], wgx_ref[h], preferred_element_type=F32) + bgx_ref[:, cs])
            log_a = neg_c_sp[:, cs] * r
            abuf[r0:r0 + rc, cs] = jnp.exp(log_a)
            th = jnp.tanh(log_a)
            bbuf[r0:r0 + rc, cs] = jnp.sqrt(-2.0 * th / (1.0 - th)) * (i * xr[:, cs])

    cbuf[0:HALO_B, :] = cbuf[ts:ts + HALO_B, :]

    rowid = lax.broadcasted_iota(I32, (SUBLANES, d_rnn), 0)

    def group(g, h_prev):
        r = pl.multiple_of(g * SUBLANES, SUBLANES)
        a = abuf[pl.ds(r, SUBLANES), :]
        b = bbuf[pl.ds(r, SUBLANES), :]
        step = 1
        while step < SUBLANES:
            keep = rowid >= step
            b = jnp.where(keep, a * pltpu.roll(b, step, axis=0) + b, b)
            a = jnp.where(keep, a * pltpu.roll(a, step, axis=0), a)
            step *= 2
        h = a * h_prev + b
        bbuf[pl.ds(r, SUBLANES), :] = h
        return h[SUBLANES - 1:SUBLANES, :]

    hcar[...] = lax.fori_loop(0, ts // SUBLANES, group, hcar[...], unroll=2)

    out = (jax.nn.gelu(ybr[...]) * bbuf[...]).astype(BF16)
    m = jnp.dot(out, wout_ref[...], preferred_element_type=F32)
    _ln_and_route(alpha * x + m, lng_ref, lnb_ref, wrt_ref, br_ref, o_ref, slot_ref, gate_ref, tab_ref,
                  tri, base, first, tm=tm)


def _mix1(x, w_in, c_dw, c_dw_b, w_gate_a, b_gate_a, w_gate_x, b_gate_x, lam, w_out, ln_g, ln_b,
          w_router, b_router, *, alpha, ts=SEQ_TILE, tm=TOKEN_TILE, rc=64):
    bsz, seq, d = x.shape
    d_rnn = c_dw.shape[-1]
    assert c_dw.shape[0] - 1 <= HALO_B
    ts = min(ts, seq)
    rc = min(rc, ts)
    assert seq % ts == 0 and ts % rc == 0 and ts % SUBLANES == 0
    r_in, r_args, r_out_specs, r_out_shape, r_scratch, tm = _router_io(w_router, b_router, bsz, seq, ts, tm)
    kern = functools.partial(_mix1_kernel, alpha=alpha, ts=ts, tm=tm, d_rnn=d_rnn, rc=rc)
    return pl.pallas_call(
        kern,
        grid=(bsz, seq // ts),
        in_specs=[
            pl.BlockSpec((1, ts, d), lambda b, s: (b, s, 0)),
            _const_spec(w_in.shape), _const_spec((c_dw.shape[0], SUBLANES, d_rnn)), _const_spec((1, d_rnn)),
            _const_spec(w_gate_a.shape), _const_spec((1, d_rnn)),
            _const_spec(w_gate_x.shape), _const_spec((1, d_rnn)),
            _const_spec((1, d_rnn)), _const_spec(w_out.shape), _const_spec((1, d)), _const_spec((1, d)),
        ] + r_in,
        out_specs=[pl.BlockSpec((1, ts, d), lambda b, s: (b, s, 0))] + r_out_specs,
        out_shape=[jax.ShapeDtypeStruct(x.shape, F32)] + r_out_shape,
        scratch_shapes=[
            pltpu.VMEM((HALO_B + ts, d_rnn), F32),
            pltpu.VMEM((ts, d_rnn), F32),
            pltpu.VMEM((ts, d_rnn), F32),
            pltpu.VMEM((ts, d_rnn), F32),
            pltpu.VMEM((1, d_rnn), F32),
        ] + r_scratch,
        compiler_params=pltpu.CompilerParams(
            dimension_semantics=("arbitrary", "arbitrary"), vmem_limit_bytes=VMEM_LIMIT),
        name="mix1",
    )(x, w_in.astype(BF16), _tap_rows(c_dw), _row(c_dw_b), w_gate_a.astype(BF16), _row(b_gate_a),
      w_gate_x.astype(BF16), _row(b_gate_x), _row(lam), w_out.astype(BF16), _row(ln_g), _row(ln_b), *r_args)


def _first_argmax(vals):
    best, idx = vals[0], jnp.zeros(vals[0].shape, I32)
    for j in range(1, len(vals)):
        upd = vals[j] > best
        best = jnp.where(upd, vals[j], best)
        idx = jnp.where(upd, j, idx)
    return best, idx


def _route_tile(x, wrt_ref, br_ref, tri, base, tm):
    n_exp = wrt_ref.shape[0]
    tt = x.shape[0]
    w = wrt_ref[...]
    w_hi = w.astype(BF16)
    w_lo = (w - w_hi.astype(F32)).astype(BF16)
    x_hi = x.astype(BF16)
    x_lo = (x - x_hi.astype(F32)).astype(BF16)
    over_d = (((1,), (1,)), ((), ()))
    logits = (lax.dot_general(w_hi, x_hi, over_d, preferred_element_type=F32)
              + lax.dot_general(w_hi, x_lo, over_d, preferred_element_type=F32)
              + lax.dot_general(w_lo, x_hi, over_d, preferred_element_type=F32))
    logits = logits + br_ref[:, 0:1]
    e = jnp.exp(logits - jnp.max(logits, axis=0, keepdims=True))
    p = e / jnp.sum(e, axis=0, keepdims=True)
    rows = [p[j:j + 1, :] for j in range(n_exp)]

    n_grp = n_exp // EXPERTS_PER_GROUP
    scores = []
    for g in range(n_grp):
        a, b, c, d = rows[EXPERTS_PER_GROUP * g:EXPERTS_PER_GROUP * (g + 1)]
        scores.append(jnp.maximum(jnp.maximum(a, b) + jnp.maximum(c, d), jnp.maximum(a + b, c + d)))
    _, g_sel = _first_argmax(scores)

    p_in = []
    for j in range(EXPERTS_PER_GROUP):
        v = rows[j]
        for g in range(1, n_grp):
            v = jnp.where(g_sel == g, rows[EXPERTS_PER_GROUP * g + j], v)
        p_in.append(v)
    v1, i1 = _first_argmax(p_in)
    v2, i2 = _first_argmax([jnp.where(i1 == j, -1.0, p_in[j]) for j in range(EXPERTS_PER_GROUP)])
    den = v1 + v2
    e1 = g_sel * EXPERTS_PER_GROUP + i1
    e2 = g_sel * EXPERTS_PER_GROUP + i2

    eid = lax.broadcasted_iota(I32, (n_exp, tt), 0)
    hit1 = eid == e1
    hit2 = eid == e2
    member = jnp.where(hit1 | hit2, 1.0, 0.0)
    member_b = member.astype(BF16)
    lower = (lax.broadcasted_iota(I32, (n_exp, n_exp), 1) < lax.broadcasted_iota(I32, (n_exp, n_exp), 0))
    lower = jnp.where(lower, 1.0, 0.0).astype(BF16)
    lane = lax.broadcasted_iota(I32, (n_exp, LANES), 1)
    place, tabs = [], []
    for h in range(tt // tm):
        cols = slice(h * tm, (h + 1) * tm)
        incl = jnp.dot(member_b[:, cols], tri[...], preferred_element_type=F32)
        count = jnp.sum(member[:, cols], axis=1, keepdims=True)
        offset = jnp.dot(lower, jnp.broadcast_to(count, (n_exp, LANES)).astype(BF16),
                         preferred_element_type=F32)[:, 0:1]
        place.append(offset + incl - member[:, cols])
        tabs.append(jnp.where(lane == 0, base, jnp.where(lane == 1, count, 0.0)).astype(I32))
        base = base + count
    place = jnp.concatenate(place, axis=1)
    s1 = jnp.sum(jnp.where(hit1, place, 0.0), axis=0, keepdims=True)
    s2 = jnp.sum(jnp.where(hit2, place, 0.0), axis=0, keepdims=True)
    return (s1.astype(I32), s2.astype(I32)), (v1 / den, v2 / den), tabs, base


def _ln_and_route(v, lng_ref, lnb_ref, wrt_ref, br_ref, o_ref, slot_ref, gate_ref, tab_ref, tri, base, first, *, tm):
    @pl.when(first)
    def _():
        src = lax.broadcasted_iota(I32, (tm, tm), 0)
        dst = lax.broadcasted_iota(I32, (tm, tm), 1)
        tri[...] = jnp.where(src <= dst, 1.0, 0.0).astype(BF16)
        base[...] = jnp.zeros(base.shape, F32)

    x1 = _layer_norm(v, lng_ref[...], lnb_ref[...])
    o_ref[0] = x1
    slots, gates, tabs, new_base = _route_tile(x1, wrt_ref, br_ref, tri, base[...], tm)
    for h, tab in enumerate(tabs):
        for k in range(2):
            slot_ref[h, k:k + 1, :] = slots[k][:, h * tm:(h + 1) * tm]
            gate_ref[h, k:k + 1, :] = gates[k][:, h * tm:(h + 1) * tm]
        tab_ref[h] = tab
    base[...] = new_base


def _router_io(w_router, b_router, bsz, seq, ts, tm):
    d, n_exp = w_router.shape
    tm = min(tm, ts)
    assert ts % tm == 0 and n_exp % EXPERTS_PER_GROUP == 0 and tm <= 256
    t = bsz * seq
    per_seq = seq // ts
    in_specs = [_const_spec((n_exp, d)), _const_spec((n_exp, LANES))]
    args = (w_router.T.astype(F32), jnp.broadcast_to(b_router.astype(F32)[:, None], (n_exp, LANES)))
    per_tile = lambda b, s: (b * per_seq + s, 0, 0)
    out_specs = [
        pl.BlockSpec((ts // tm, 2, tm), per_tile),
        pl.BlockSpec((ts // tm, 2, tm), per_tile),
        pl.BlockSpec((ts // tm, n_exp, LANES), per_tile),
    ]
    out_shape = [
        jax.ShapeDtypeStruct((t // tm, 2, tm), I32),
        jax.ShapeDtypeStruct((t // tm, 2, tm), F32),
        jax.ShapeDtypeStruct((t // tm, n_exp, LANES), I32),
    ]
    scratch = [pltpu.VMEM((tm, tm), BF16), pltpu.VMEM((n_exp, LANES), F32)]
    return in_specs, args, out_specs, out_shape, scratch, tm


def _segment_copies(src_of, dst_of, length, max_len, sem, *, wait=False, priority=0):
    done = 0
    for bit in reversed(range(max_len.bit_length())):
        n = 1 << bit
        piece = length & n

        @pl.when(piece != 0)
        def _(done=done, n=n):
            copy = pltpu.make_async_copy(src_of(done, n), dst_of(done, n), sem)
            copy.wait() if wait else copy.start(priority)

        done = done + piece


def _token_rows(ref, lo, n):
    return ref.at[pl.ds(pl.multiple_of(lo * SUBLANES, SUBLANES), n * SUBLANES)]


def _store_token_major(ref, val):
    for j in range(SUBLANES):
        ref[pl.ds(j, val.shape[0], stride=SUBLANES), :] = val[:, j * LANES:(j + 1) * LANES]


def _load_token_major(ref, rows):
    return jnp.concatenate([ref[pl.ds(j, rows, stride=SUBLANES), :] for j in range(SUBLANES)], axis=1)


def _dispatch_kernel(seg_dst_ref, seg_len_ref, pad_dst_ref, pad_len_ref, n_used_ref, x_ref, slot_ref, xs_hbm,
                     grouped, zeros, sems, pad_sem, *, tm, n_exp, bm, group):
    i = pl.program_id(0)
    n_tiles = pl.num_programs(0) * group

    def wait_tile(b):
        pltpu.make_async_copy(grouped.at[b], _token_rows(xs_hbm, 0, 2 * tm), sems.at[b]).wait()

    def pad_copies(wait):
        for e in range(n_exp):
            _segment_copies(lambda lo, n: _token_rows(zeros, lo, n),
                            lambda lo, n, e=e: _token_rows(xs_hbm, pad_dst_ref[e] + lo, n),
                            pad_len_ref[e], bm - 1, pad_sem, wait=wait)
        n_blk = xs_hbm.shape[0] // (bm * SUBLANES)
        for blk in range(n_blk - n_exp, n_blk):
            @pl.when(blk >= n_used_ref[0])
            def _(blk=blk):
                copy = pltpu.make_async_copy(zeros, _token_rows(xs_hbm, blk * bm, bm), pad_sem)
                copy.wait() if wait else copy.start()

    @pl.when(i == 0)
    def _():
        zeros[...] = jnp.zeros(zeros.shape, F32)
        pad_copies(wait=False)

    def tile(h, carry):
        k = i * group + h
        buf = lax.rem(k, 2)

        @pl.when(k >= 2)
        def _():
            wait_tile(buf)

        slots = slot_ref[h]
        row = lax.broadcasted_iota(I32, (2 * tm, tm), 0)
        onehot = jnp.where((row == slots[0:1, :]) | (row == slots[1:2, :]), 1.0, 0.0).astype(BF16)
        xt = x_ref[pl.ds(pl.multiple_of(h * tm, tm), tm), :].astype(BF16)
        _store_token_major(grouped.at[buf], jnp.dot(onehot, xt, preferred_element_type=F32))

        start = 0
        for e in range(n_exp):
            length = seg_len_ref[k * n_exp + e]
            dst = seg_dst_ref[k * n_exp + e]
            _segment_copies(lambda lo, n, start=start: _token_rows(grouped.at[buf], start + lo, n),
                            lambda lo, n, dst=dst: _token_rows(xs_hbm, dst + lo, n),
                            length, tm, sems.at[buf])
            start = start + length
        return carry

    lax.fori_loop(0, group, tile, 0)

    @pl.when(i == pl.num_programs(0) - 1)
    def _():
        wait_tile(lax.rem(n_tiles - 1, 2))

        @pl.when(n_tiles > 1)
        def _():
            wait_tile(lax.rem(n_tiles, 2))

        pad_copies(wait=True)


def _dispatch(xf, slots, seg_dst, seg_len, pad_dst, pad_len, n_used, n_pad, *, tm, bm, group):
    t, d = xf.shape
    assert d == SUBLANES * LANES and t % (tm * group) == 0
    n_exp = pad_dst.shape[0]
    grid_spec = pltpu.PrefetchScalarGridSpec(
        num_scalar_prefetch=5,
        grid=(t // (tm * group),),
        in_specs=[
            pl.BlockSpec((group * tm, d), lambda i, *_: (i, 0)),
            pl.BlockSpec((group, 2, tm), lambda i, *_: (i, 0, 0)),
        ],
        out_specs=pl.BlockSpec(memory_space=pl.ANY),
        scratch_shapes=[
            pltpu.VMEM((2, 2 * tm * SUBLANES, LANES), F32),
            pltpu.VMEM((bm * SUBLANES, LANES), F32),
            pltpu.SemaphoreType.DMA((2,)),
            pltpu.SemaphoreType.DMA(()),
        ],
    )
    return pl.pallas_call(
        functools.partial(_dispatch_kernel, tm=tm, n_exp=n_exp, bm=bm, group=group),
        grid_spec=grid_spec,
        out_shape=jax.ShapeDtypeStruct((n_pad * SUBLANES, LANES), F32),
        compiler_params=pltpu.CompilerParams(
            dimension_semantics=("arbitrary",), vmem_limit_bytes=VMEM_LIMIT),
        name="dispatch",
    )(seg_dst, seg_len, pad_dst, pad_len, n_used, xf, slots)


def _experts_kernel(first_blk_ref, n_blk_ref, n_used_ref, xs_hbm, w1_ref, w3_ref, w2_ref, y_hbm,
                    w1b, w3b, w2b, xbuf, ybuf, in_sems, out_sems, *, bm, n_exp):
    e = pl.program_id(0)
    first = first_blk_ref[e]
    n = n_blk_ref[e]

    n_x = xbuf.shape[0]
    ahead = n_x - 1

    def x_copy(blk, slot):
        return pltpu.make_async_copy(_token_rows(xs_hbm, blk * bm, bm), xbuf.at[slot], in_sems.at[slot])

    def y_copy(blk, slot):
        return pltpu.make_async_copy(ybuf.at[slot], _token_rows(y_hbm, blk * bm, bm), out_sems.at[slot])

    def fetch_head(first_blk, count):
        for k in range(ahead):
            @pl.when(k < count)
            def _(k=k):
                x_copy(first_blk + k, k).start(BLOCK_DMA_PRIORITY)

    @pl.when(e == 0)
    def _():
        fetch_head(first, n)

    w1b[...] = w1_ref[0, 0].astype(BF16)
    w3b[...] = w3_ref[0, 0].astype(BF16)
    w2b[...] = w2_ref[0, 0].astype(BF16)

    def block(j, carry):
        slot = lax.rem(j, 2)
        x_slot = lax.rem(j, n_x)

        @pl.when(j + ahead < n)
        def _():
            x_copy(first + j + ahead, lax.rem(j + ahead, n_x)).start(BLOCK_DMA_PRIORITY)

        x_copy(first + j, x_slot).wait()

        @pl.when(j >= 2)
        def _():
            y_copy(first + j - 2, slot).wait()

        xb = _load_token_major(xbuf.at[x_slot], bm).astype(BF16)
        h1 = jnp.dot(xb, w1b[...], preferred_element_type=F32)
        h3 = jnp.dot(xb, w3b[...], preferred_element_type=F32)
        hb = (h1 * jax.nn.sigmoid(h1) * h3).astype(BF16)
        _store_token_major(ybuf.at[slot], jnp.dot(hb, w2b[...], preferred_element_type=F32))
        y_copy(first + j, slot).start(BLOCK_DMA_PRIORITY)
        return carry

    lax.fori_loop(0, n, block, 0)

    nxt = jnp.minimum(e + 1, n_exp - 1)

    @pl.when(e + 1 < n_exp)
    def _():
        fetch_head(first_blk_ref[nxt], n_blk_ref[nxt])

    for back in (2, 1):
        @pl.when(n >= back)
        def _(back=back):
            y_copy(first + n - back, lax.rem(n - back, 2)).wait()

    @pl.when(e == n_exp - 1)
    def _():
        ybuf[0] = jnp.zeros(ybuf.shape[1:], F32)
        n_all = y_hbm.shape[0] // (bm * SUBLANES)
        for blk in range(n_all - n_exp, n_all):
            @pl.when(blk >= n_used_ref[0])
            def _(blk=blk):
                y_copy(blk, 0).start(BLOCK_DMA_PRIORITY)
                y_copy(blk, 0).wait()


def _experts(xs, first_blk, n_blk, n_used, w1, w3, w2, layer, *, bm):
    n_exp, d, d_e = w1.shape[-3:]
    assert d == SUBLANES * LANES
    w_blk = lambda e, *_: (layer, e, 0, 0)
    grid_spec = pltpu.PrefetchScalarGridSpec(
        num_scalar_prefetch=3,
        grid=(n_exp,),
        in_specs=[
            pl.BlockSpec(memory_space=pl.ANY),
            pl.BlockSpec((1, 1, d, d_e), w_blk),
            pl.BlockSpec((1, 1, d, d_e), w_blk),
            pl.BlockSpec((1, 1, d_e, d), w_blk),
        ],
        out_specs=pl.BlockSpec(memory_space=pl.ANY),
        scratch_shapes=[
            pltpu.VMEM((d, d_e), BF16), pltpu.VMEM((d, d_e), BF16), pltpu.VMEM((d_e, d), BF16),
            pltpu.VMEM((EXPERT_IN_SLOTS, bm * SUBLANES, LANES), F32), pltpu.VMEM((2, bm * SUBLANES, LANES), F32),
            pltpu.SemaphoreType.DMA((EXPERT_IN_SLOTS,)), pltpu.SemaphoreType.DMA((2,)),
        ],
    )
    return pl.pallas_call(
        functools.partial(_experts_kernel, bm=bm, n_exp=n_exp),
        grid_spec=grid_spec,
        out_shape=jax.ShapeDtypeStruct(xs.shape, F32),
        compiler_params=pltpu.CompilerParams(
            dimension_semantics=("arbitrary",), vmem_limit_bytes=VMEM_LIMIT),
        name="experts",
    )(first_blk, n_blk, n_used, xs, w1, w3, w2)


def _combine_kernel(seg_src_ref, seg_len_ref, x_ref, slot_ref, gate_ref, lng_ref, lnb_ref, y_hbm, o_ref,
                    grouped, sems, *, alpha, tm, n_exp, group):
    i = pl.program_id(0)
    n_tiles = pl.num_programs(0) * group

    def fetch(tile, b):
        start = 0
        for e in range(n_exp):
            length = seg_len_ref[tile * n_exp + e]
            src = seg_src_ref[tile * n_exp + e]
            _segment_copies(lambda lo, n, src=src: _token_rows(y_hbm, src + lo, n),
                            lambda lo, n, start=start: _token_rows(grouped.at[b], start + lo, n),
                            length, tm, sems.at[b], priority=BLOCK_DMA_PRIORITY)
            start = start + length

    n_buf = grouped.shape[0]
    ahead = n_buf - 1

    @pl.when(i == 0)
    def _():
        for k0 in range(ahead):
            @pl.when(k0 < n_tiles)
            def _(k0=k0):
                fetch(k0, k0)

    def tile(h, carry):
        k = i * group + h
        buf = lax.rem(k, n_buf)

        @pl.when(k + ahead < n_tiles)
        def _():
            fetch(k + ahead, lax.rem(k + ahead, n_buf))

        pltpu.make_async_copy(_token_rows(y_hbm, 0, 2 * tm), grouped.at[buf], sems.at[buf]).wait()

        slots = slot_ref[h]
        gates = gate_ref[h]
        row = lax.broadcasted_iota(I32, (2 * tm, tm), 0)
        g = (jnp.where(row == slots[0:1, :], gates[0:1, :], 0.0)
             + jnp.where(row == slots[1:2, :], gates[1:2, :], 0.0))
        yb = _load_token_major(grouped.at[buf], 2 * tm).astype(BF16)
        f = lax.dot_general(g.astype(BF16), yb, (((0,), (0,)), ((), ())), preferred_element_type=F32)
        rows = pl.ds(pl.multiple_of(h * tm, tm), tm)
        o_ref[rows, :] = _layer_norm(alpha * x_ref[rows, :] + f, lng_ref[...], lnb_ref[...])
        return carry

    lax.fori_loop(0, group, tile, 0)


def _combine(xf, y, slots, gates, seg_src, seg_len, ln_g, ln_b, *, alpha, tm, n_exp, group):
    t, d = xf.shape
    assert t % (tm * group) == 0
    grid_spec = pltpu.PrefetchScalarGridSpec(
        num_scalar_prefetch=2,
        grid=(t // (tm * group),),
        in_specs=[
            pl.BlockSpec((group * tm, d), lambda i, *_: (i, 0)),
            pl.BlockSpec((group, 2, tm), lambda i, *_: (i, 0, 0)),
            pl.BlockSpec((group, 2, tm), lambda i, *_: (i, 0, 0)),
            pl.BlockSpec((1, d), lambda i, *_: (0, 0)),
            pl.BlockSpec((1, d), lambda i, *_: (0, 0)),
            pl.BlockSpec(memory_space=pl.ANY),
        ],
        out_specs=pl.BlockSpec((group * tm, d), lambda i, *_: (i, 0)),
        scratch_shapes=[pltpu.VMEM((COMBINE_SLOTS, 2 * tm * SUBLANES, LANES), F32),
                        pltpu.SemaphoreType.DMA((COMBINE_SLOTS,))],
    )
    return pl.pallas_call(
        functools.partial(_combine_kernel, alpha=alpha, tm=tm, n_exp=n_exp, group=group),
        grid_spec=grid_spec,
        out_shape=jax.ShapeDtypeStruct((t, d), F32),
        compiler_params=pltpu.CompilerParams(
            dimension_semantics=("arbitrary",), vmem_limit_bytes=VMEM_LIMIT),
        name="combine",
    )(seg_src, seg_len, xf, slots, gates, _row(ln_g), _row(ln_b), y)


def _moe_ln(x, slots, gates, tab, w1, w3, w2, layer, ln_g, ln_b, *, alpha, bm=EXPERT_BLOCK):
    bsz, seq, d = x.shape
    t = bsz * seq
    n_exp = tab.shape[1]
    tm = t // tab.shape[0]
    xf = x.reshape(t, d)
    before, in_tile = tab[:, :, 0], tab[:, :, 1]
    counts = before[-1] + in_tile[-1]
    padded = (counts + bm - 1) // bm * bm
    pend = jnp.cumsum(padded)
    pstart = pend - padded
    n_blk = -(-(2 * t) // bm) + n_exp
    n_used = (pend[-1] // bm).reshape(1).astype(I32)
    seg_pos = (pstart[None, :] + before).reshape(-1).astype(I32)
    seg_len = in_tile.reshape(-1).astype(I32)
    group = math.gcd(TILES_PER_STEP, t // tm)
    xs = _dispatch(xf, slots, seg_pos, seg_len, (pstart + counts).astype(I32), (padded - counts).astype(I32),
                   n_used, n_blk * bm, tm=tm, bm=bm, group=group)
    y = _experts(xs, (pstart // bm).astype(I32), (padded // bm).astype(I32), n_used, w1, w3, w2, layer, bm=bm)
    out = _combine(xf, y, slots, gates, seg_pos, seg_len, ln_g, ln_b, alpha=alpha, tm=tm, n_exp=n_exp, group=group)
    return out.reshape(bsz, seq, d)


def kernel(x, ln1_g, ln1_b, ln2_g, ln2_b, even_w_in, even_a_dw, even_a_dw_b, even_a_ln_g, even_a_ln_b, even_b_dw, even_w_out, odd_w_in, odd_c_dw, odd_c_dw_b, odd_w_gate_a, odd_b_gate_a, odd_w_gate_x, odd_b_gate_x, odd_lam, odd_w_out, w_router, b_router, moe_w1, moe_w3, moe_w2):
    depth = ln1_g.shape[0]
    alpha = (2.0 * depth) ** 0.25
    for layer in range(depth):
        j = layer // 2
        if layer % 2 == 0:
            x, slots, gates, tab = _mix0(
                x, even_w_in[j], even_a_dw[j], even_a_dw_b[j], even_a_ln_g[j], even_a_ln_b[j],
                even_b_dw[j], even_w_out[j], ln1_g[layer], ln1_b[layer], w_router, b_router, alpha=alpha)
        else:
            x, slots, gates, tab = _mix1(
                x, odd_w_in[j], odd_c_dw[j], odd_c_dw_b[j], odd_w_gate_a[j], odd_b_gate_a[j],
                odd_w_gate_x[j], odd_b_gate_x[j], odd_lam[j], odd_w_out[j], ln1_g[layer], ln1_b[layer],
                w_router, b_router, alpha=alpha)
        x = _moe_ln(x, slots, gates, tab, moe_w1, moe_w3, moe_w2, layer,
                    ln2_g[layer], ln2_b[layer], alpha=alpha)
    return x
```

```python
import functools
import math

import jax
import jax.numpy as jnp
from jax import lax
from jax.experimental import pallas as pl
from jax.experimental.pallas import tpu as pltpu

F32 = jnp.float32
BF16 = jnp.bfloat16
I32 = jnp.int32
LN_EPS = 1e-5
LRU_C = 8.0
SUBLANES = 8
LANES = 128
HALO_A = 32
HALO_B = 8
EXPERTS_PER_GROUP = 4
BLOCK_DMA_PRIORITY = 1
TILES_PER_STEP = 4
COMBINE_SLOTS = 3
EXPERT_IN_SLOTS = 4
SEQ_TILE = 512
TOKEN_TILE = 256
EXPERT_BLOCK = 256
VMEM_LIMIT = 56 * 1024 * 1024


def _layer_norm(v, g, b):
    mu = jnp.mean(v, axis=-1, keepdims=True)
    c = v - mu
    var = jnp.mean(c * c, axis=-1, keepdims=True)
    return c * lax.rsqrt(var + LN_EPS) * g + b


def _row(v):
    return v.reshape(1, -1).astype(F32)


def _tap_rows(w):
    return jnp.broadcast_to(w.astype(F32)[:, None, :], (w.shape[0], SUBLANES, w.shape[1]))


def _tap(w_ref, k, rows):
    return jnp.concatenate([w_ref[k]] * (rows // SUBLANES), axis=0)


def _const_spec(shape):
    return pl.BlockSpec(shape, lambda *_: (0,) * len(shape))


def _mix0_kernel(x_ref, win_ref, adw_ref, adwb_ref, alng_ref, alnb_ref, bdw_ref, wout_ref, lng_ref, lnb_ref,
                 wrt_ref, br_ref, o_ref, slot_ref, gate_ref, tab_ref, *scratch, alpha, ts, tm, d_a, d_b, rc, parts):
    tri, base = scratch[-2:]
    bufs = [scratch[5 * p:5 * p + 5] for p in range(parts)]
    first = (pl.program_id(0) == 0) & (pl.program_id(1) == 0)
    n_a = adw_ref.shape[0]
    n_b = bdw_ref.shape[0]
    hs = ts // parts

    @pl.when(pl.program_id(1) == 0)
    def _():
        bufs[0][0][0:HALO_A, :] = jnp.zeros((HALO_A, d_a), F32)
        bufs[0][2][0:HALO_B, :] = jnp.zeros((HALO_B, d_b), F32)

    x = x_ref[0]
    xb = x.astype(BF16)
    cw = 2 * LANES
    dot = functools.partial(jnp.dot, preferred_element_type=F32)

    def proj_units(p):
        abuf, ash, cbuf, bgate, cat = bufs[p]
        xp = xb[p * hs:(p + 1) * hs, :]
        units = []
        for c0 in range(0, d_a, cw):
            def unit(c0=c0):
                val = dot(xp, win_ref[:, c0:c0 + cw])
                gate = dot(xp, win_ref[:, d_a + c0:d_a + c0 + cw])
                abuf[HALO_A:HALO_A + hs, c0:c0 + cw] = val * jax.nn.sigmoid(gate)
            units.append(unit)
        for c0 in range(0, d_b, cw):
            def unit(c0=c0, o=2 * d_a):
                bgate[:, c0:c0 + cw] = dot(xp, win_ref[:, o + c0:o + c0 + cw])
                cbuf[HALO_B:HALO_B + hs, c0:c0 + cw] = (dot(xp, win_ref[:, o + d_b + c0:o + d_b + c0 + cw])
                                                        * dot(xp, win_ref[:, o + 2 * d_b + c0:o + 2 * d_b + c0 + cw]))
            units.append(unit)
        return units

    def conv_units(p):
        abuf, ash, cbuf, bgate, cat = bufs[p]

        def shifted():
            n_sh = hs + HALO_A - SUBLANES
            for j in range(1, SUBLANES):
                ash[j - 1, :, :] = abuf[j:j + n_sh, :]
        units = [shifted]
        for c in range(hs // rc):
            def unit(r0=c * rc):
                acc = adwb_ref[...]
                for k in range(n_a):
                    q, j = divmod(HALO_A - (n_a - 1) + k, SUBLANES)
                    lo = r0 + q * SUBLANES
                    src = abuf[lo:lo + rc, :] if j == 0 else ash[j - 1, lo:lo + rc, :]
                    acc = acc + _tap(adw_ref, k, rc) * src
                a = _layer_norm(acc, alng_ref[...], alnb_ref[...])
                cat[r0:r0 + rc, 0:d_a] = (a * jax.nn.sigmoid(a)).astype(BF16)
                bb = jnp.zeros((rc, d_b), F32)
                for k in range(n_b):
                    lo = r0 + HALO_B - (n_b - 1) + k
                    bb = bb + _tap(bdw_ref, k, rc) * cbuf[lo:lo + rc, :]
                cat[r0:r0 + rc, d_a:d_a + d_b] = (bgate[r0:r0 + rc, :] * bb).astype(BF16)
            units.append(unit)
        return units

    m_cols = [[] for _ in range(parts)]

    def out_units(p):
        cat = bufs[p][4]
        units = []
        for c0 in range(0, wout_ref.shape[1], cw):
            def unit(c0=c0):
                m_cols[p].append(dot(cat[...], wout_ref[:, c0:c0 + cw]))
            units.append(unit)
        return units

    def run_interleaved(*groups):
        order = sorted(((i + 0.5) / len(g), gi, i) for gi, g in enumerate(groups) for i in range(len(g)))
        for _, gi, i in order:
            groups[gi][i]()

    for s in range(parts + 2):
        groups = []
        if s < parts:
            groups.append(proj_units(s))
        if 0 <= s - 1 < parts:
            groups.append(conv_units(s - 1))
        if 0 <= s - 2 < parts:
            groups.append(out_units(s - 2))
        run_interleaved(*groups)
        if s + 1 < parts:
            bufs[s + 1][0][0:HALO_A, :] = bufs[s][0][hs:hs + HALO_A, :]
            bufs[s + 1][2][0:HALO_B, :] = bufs[s][2][hs:hs + HALO_B, :]

    bufs[0][0][0:HALO_A, :] = bufs[-1][0][hs:hs + HALO_A, :]
    bufs[0][2][0:HALO_B, :] = bufs[-1][2][hs:hs + HALO_B, :]

    m = jnp.concatenate([jnp.concatenate(cols, axis=1) for cols in m_cols], axis=0)
    _ln_and_route(alpha * x + m, lng_ref, lnb_ref, wrt_ref, br_ref, o_ref, slot_ref, gate_ref, tab_ref,
                  tri, base, first, tm=tm)


def _mix0(x, w_in, a_dw, a_dw_b, a_ln_g, a_ln_b, b_dw, w_out, ln_g, ln_b, w_router, b_router,
          *, alpha, ts=SEQ_TILE, tm=TOKEN_TILE, rc=32, parts=2):
    bsz, seq, d = x.shape
    d_a = a_dw.shape[-1]
    d_b = b_dw.shape[-1]
    assert a_dw.shape[0] - 1 <= HALO_A and b_dw.shape[0] - 1 <= HALO_B
    ts = min(ts, seq)
    hs = ts // parts
    rc = min(rc, hs)
    assert seq % ts == 0 and ts % parts == 0 and hs % rc == 0 and rc % (2 * SUBLANES) == 0
    r_in, r_args, r_out_specs, r_out_shape, r_scratch, tm = _router_io(w_router, b_router, bsz, seq, ts, tm)
    kern = functools.partial(_mix0_kernel, alpha=alpha, ts=ts, tm=tm, d_a=d_a, d_b=d_b, rc=rc, parts=parts)
    part_scratch = [
        pltpu.VMEM((HALO_A + hs, d_a), F32),
        pltpu.VMEM((SUBLANES - 1, HALO_A + hs - SUBLANES, d_a), F32),
        pltpu.VMEM((HALO_B + hs, d_b), F32),
        pltpu.VMEM((hs, d_b), F32),
        pltpu.VMEM((hs, d_a + d_b), BF16),
    ]
    return pl.pallas_call(
        kern,
        grid=(bsz, seq // ts),
        in_specs=[
            pl.BlockSpec((1, ts, d), lambda b, s: (b, s, 0)),
            _const_spec(w_in.shape), _const_spec((a_dw.shape[0], SUBLANES, d_a)), _const_spec((1, d_a)),
            _const_spec((1, d_a)), _const_spec((1, d_a)), _const_spec((b_dw.shape[0], SUBLANES, d_b)),
            _const_spec(w_out.shape),
            _const_spec((1, d)), _const_spec((1, d)),
        ] + r_in,
        out_specs=[pl.BlockSpec((1, ts, d), lambda b, s: (b, s, 0))] + r_out_specs,
        out_shape=[jax.ShapeDtypeStruct(x.shape, F32)] + r_out_shape,
        scratch_shapes=part_scratch * parts + r_scratch,
        compiler_params=pltpu.CompilerParams(
            dimension_semantics=("arbitrary", "arbitrary"), vmem_limit_bytes=VMEM_LIMIT),
        name="mix0",
    )(x, w_in.astype(BF16), _tap_rows(a_dw), _row(a_dw_b), _row(a_ln_g), _row(a_ln_b),
      _tap_rows(b_dw), w_out.astype(BF16), _row(ln_g), _row(ln_b), *r_args)


def _mix1_kernel(x_ref, win_ref, cdw_ref, cdwb_ref, wga_ref, bga_ref, wgx_ref, bgx_ref, lam_ref, wout_ref,
                 lng_ref, lnb_ref, wrt_ref, br_ref, o_ref, slot_ref, gate_ref, tab_ref,
                 cbuf, ybr, abuf, bbuf, hcar, tri, base, *, alpha, ts, tm, d_rnn, rc):
    first = (pl.program_id(0) == 0) & (pl.program_id(1) == 0)

    @pl.when(pl.program_id(1) == 0)
    def _():
        cbuf[0:HALO_B, :] = jnp.zeros((HALO_B, d_rnn), F32)
        hcar[...] = jnp.zeros((1, d_rnn), F32)

    x = x_ref[0] if len(x_ref.shape) == 3 else x_ref[...]
    xb = x.astype(BF16)
    ybr[...] = jnp.dot(xb, win_ref[:, 0:d_rnn], preferred_element_type=F32)
    cbuf[HALO_B:HALO_B + ts, :] = jnp.dot(xb, win_ref[:, d_rnn:2 * d_rnn], preferred_element_type=F32)

    n_heads, blk = wga_ref.shape[0], wga_ref.shape[1]
    n_c = cdw_ref.shape[0]
    neg_c_sp = -LRU_C * jax.nn.softplus(-lam_ref[...])
    for c in range(ts // rc):
        r0 = c * rc
        xr = cdwb_ref[...]
        for k in range(n_c):
            lo = r0 + HALO_B - (n_c - 1) + k
            xr = xr + _tap(cdw_ref, k, rc) * cbuf[lo:lo + rc, :]
        xrb = xr.astype(BF16)
        for h in range(n_heads):
            cs = slice(h * blk, (h + 1) * blk)
            r = jax.nn.sigmoid(jnp.dot(xrb[:, cs], wga_ref[h], preferred_element_type=F32) + bga_ref[:, cs])
            i = jax.nn.sigmoid(jnp.dot(xrb[:, cs], wgx_ref[h], preferred_element_type=F32) + bgx_ref[:, cs])
            log_a = neg_c_sp[:, cs] * r
            abuf[r0:r0 + rc, cs] = jnp.exp(log_a)
            th = jnp.tanh(log_a)
            bbuf[r0:r0 + rc, cs] = jnp.sqrt(-2.0 * th / (1.0 - th)) * (i * xr[:, cs])

    cbuf[0:HALO_B, :] = cbuf[ts:ts + HALO_B, :]

    rowid = lax.broadcasted_iota(I32, (SUBLANES, d_rnn), 0)

    def group(g, h_prev):
        r = pl.multiple_of(g * SUBLANES, SUBLANES)
        a = abuf[pl.ds(r, SUBLANES), :]
        b = bbuf[pl.ds(r, SUBLANES), :]
        step = 1
        while step < SUBLANES:
            keep = rowid >= step
            b = jnp.where(keep, a * pltpu.roll(b, step, axis=0) + b, b)
            a = jnp.where(keep, a * pltpu.roll(a, step, axis=0), a)
            step *= 2
        h = a * h_prev + b
        bbuf[pl.ds(r, SUBLANES), :] = h
        return h[SUBLANES - 1:SUBLANES, :]

    hcar[...] = lax.fori_loop(0, ts // SUBLANES, group, hcar[...], unroll=4)

    out = (jax.nn.gelu(ybr[...]) * bbuf[...]).astype(BF16)
    m = jnp.dot(out, wout_ref[...], preferred_element_type=F32)
    _ln_and_route(alpha * x + m, lng_ref, lnb_ref, wrt_ref, br_ref, o_ref, slot_ref, gate_ref, tab_ref,
                  tri, base, first, tm=tm)


def _mix1_after_moe_kernel(seg_src_ref, seg_len_ref, x_ref, cslot_ref, cgate_ref, clng_ref, clnb_ref, y_hbm, *rest,
                           alpha, ts, tm, d_rnn, rc, n_exp):
    *mix, xbuf, grouped, csems = rest
    group = ts // tm
    step = pl.program_id(0) * pl.num_programs(1) + pl.program_id(1)

    def store(rows, v):
        xbuf[rows, :] = v

    _combine_tiles(step, pl.num_programs(0) * pl.num_programs(1) * group, seg_src_ref, seg_len_ref,
                   lambda rows: x_ref[0, rows, :], cslot_ref, cgate_ref, clng_ref, clnb_ref, y_hbm, store,
                   grouped, csems, alpha=alpha, tm=tm, n_exp=n_exp, group=group)
    _mix1_kernel(xbuf, *mix, alpha=alpha, ts=ts, tm=tm, d_rnn=d_rnn, rc=rc)


def _mix1(x, w_in, c_dw, c_dw_b, w_gate_a, b_gate_a, w_gate_x, b_gate_x, lam, w_out, ln_g, ln_b,
          w_router, b_router, *, alpha, ts=SEQ_TILE, tm=TOKEN_TILE, rc=128, moe=None):
    bsz, seq, d = x.shape
    d_rnn = c_dw.shape[-1]
    assert c_dw.shape[0] - 1 <= HALO_B
    ts = min(ts, seq)
    rc = min(rc, ts)
    assert seq % ts == 0 and ts % rc == 0 and ts % SUBLANES == 0
    r_in, r_args, r_out_specs, r_out_shape, r_scratch, tm = _router_io(w_router, b_router, bsz, seq, ts, tm)
    kern = functools.partial(_mix1_kernel, alpha=alpha, ts=ts, tm=tm, d_rnn=d_rnn, rc=rc)
    in_specs = [
        pl.BlockSpec((1, ts, d), lambda b, s, *_: (b, s, 0)),
        _const_spec(w_in.shape), _const_spec((c_dw.shape[0], SUBLANES, d_rnn)), _const_spec((1, d_rnn)),
        _const_spec(w_gate_a.shape), _const_spec((1, d_rnn)),
        _const_spec(w_gate_x.shape), _const_spec((1, d_rnn)),
        _const_spec((1, d_rnn)), _const_spec(w_out.shape), _const_spec((1, d)), _const_spec((1, d)),
    ] + r_in
    scratch = [
        pltpu.VMEM((HALO_B + ts, d_rnn), F32),
        pltpu.VMEM((ts, d_rnn), F32),
        pltpu.VMEM((ts, d_rnn), F32),
        pltpu.VMEM((ts, d_rnn), F32),
        pltpu.VMEM((1, d_rnn), F32),
    ] + r_scratch
    args = (x, w_in.astype(BF16), _tap_rows(c_dw), _row(c_dw_b), w_gate_a.astype(BF16), _row(b_gate_a),
            w_gate_x.astype(BF16), _row(b_gate_x), _row(lam), w_out.astype(BF16), _row(ln_g), _row(ln_b), *r_args)
    prefetch = ()
    if moe is not None:
        y, m_slots, m_gates, seg_src, seg_len, m_ln_g, m_ln_b = moe
        n_exp = w_router.shape[1]
        assert m_slots.shape == (bsz * seq // tm, 2, tm) and d == SUBLANES * LANES
        kern = functools.partial(_mix1_after_moe_kernel, alpha=alpha, ts=ts, tm=tm, d_rnn=d_rnn, rc=rc, n_exp=n_exp)
        per_tile = lambda b, s, *_: (b * (seq // ts) + s, 0, 0)
        in_specs = [in_specs[0], pl.BlockSpec((ts // tm, 2, tm), per_tile), pl.BlockSpec((ts // tm, 2, tm), per_tile),
                    _const_spec((1, d)), _const_spec((1, d)), pl.BlockSpec(memory_space=pl.ANY)] + in_specs[1:]
        scratch = scratch + [pltpu.VMEM((ts, d), F32), pltpu.VMEM((COMBINE_SLOTS, 2 * tm * SUBLANES, LANES), F32),
                             pltpu.SemaphoreType.DMA((COMBINE_SLOTS,))]
        args = (x, m_slots, m_gates, _row(m_ln_g), _row(m_ln_b), y) + args[1:]
        prefetch = (seg_src, seg_len)
    grid_spec = pltpu.PrefetchScalarGridSpec(
        num_scalar_prefetch=len(prefetch),
        grid=(bsz, seq // ts),
        in_specs=in_specs,
        out_specs=[pl.BlockSpec((1, ts, d), lambda b, s, *_: (b, s, 0))] + r_out_specs,
        scratch_shapes=scratch,
    )
    return pl.pallas_call(
        kern,
        grid_spec=grid_spec,
        out_shape=[jax.ShapeDtypeStruct(x.shape, F32)] + r_out_shape,
        compiler_params=pltpu.CompilerParams(
            dimension_semantics=("arbitrary", "arbitrary"), vmem_limit_bytes=VMEM_LIMIT),
        name="mix1",
    )(*prefetch, *args)


def _first_argmax(vals):
    best, idx = vals[0], jnp.zeros(vals[0].shape, I32)
    for j in range(1, len(vals)):
        upd = vals[j] > best
        best = jnp.where(upd, vals[j], best)
        idx = jnp.where(upd, j, idx)
    return best, idx


def _route_tile(x, wrt_ref, br_ref, tri, base, tm):
    n_exp = wrt_ref.shape[0]
    tt = x.shape[0]
    w = wrt_ref[...]
    w_hi = w.astype(BF16)
    w_lo = (w - w_hi.astype(F32)).astype(BF16)
    x_hi = x.astype(BF16)
    x_lo = (x - x_hi.astype(F32)).astype(BF16)
    over_d = (((1,), (1,)), ((), ()))
    logits = (lax.dot_general(w_hi, x_hi, over_d, preferred_element_type=F32)
              + lax.dot_general(w_hi, x_lo, over_d, preferred_element_type=F32)
              + lax.dot_general(w_lo, x_hi, over_d, preferred_element_type=F32))
    logits = logits + br_ref[:, 0:1]
    e = jnp.exp(logits - jnp.max(logits, axis=0, keepdims=True))
    p = e / jnp.sum(e, axis=0, keepdims=True)
    rows = [p[j:j + 1, :] for j in range(n_exp)]

    n_grp = n_exp // EXPERTS_PER_GROUP
    scores = []
    for g in range(n_grp):
        a, b, c, d = rows[EXPERTS_PER_GROUP * g:EXPERTS_PER_GROUP * (g + 1)]
        scores.append(jnp.maximum(jnp.maximum(a, b) + jnp.maximum(c, d), jnp.maximum(a + b, c + d)))
    _, g_sel = _first_argmax(scores)

    p_in = []
    for j in range(EXPERTS_PER_GROUP):
        v = rows[j]
        for g in range(1, n_grp):
            v = jnp.where(g_sel == g, rows[EXPERTS_PER_GROUP * g + j], v)
        p_in.append(v)
    v1, i1 = _first_argmax(p_in)
    v2, i2 = _first_argmax([jnp.where(i1 == j, -1.0, p_in[j]) for j in range(EXPERTS_PER_GROUP)])
    den = v1 + v2
    e1 = g_sel * EXPERTS_PER_GROUP + i1
    e2 = g_sel * EXPERTS_PER_GROUP + i2

    eid = lax.broadcasted_iota(I32, (n_exp, tt), 0)
    hit1 = eid == e1
    hit2 = eid == e2
    member = jnp.where(hit1 | hit2, 1.0, 0.0)
    member_b = member.astype(BF16)
    lower = (lax.broadcasted_iota(I32, (n_exp, n_exp), 1) < lax.broadcasted_iota(I32, (n_exp, n_exp), 0))
    lower = jnp.where(lower, 1.0, 0.0).astype(BF16)
    lane = lax.broadcasted_iota(I32, (n_exp, LANES), 1)
    place, tabs = [], []
    for h in range(tt // tm):
        cols = slice(h * tm, (h + 1) * tm)
        incl = jnp.dot(member_b[:, cols], tri[...], preferred_element_type=F32)
        count = jnp.sum(member[:, cols], axis=1, keepdims=True)
        offset = jnp.dot(lower, jnp.broadcast_to(count, (n_exp, LANES)).astype(BF16),
                         preferred_element_type=F32)[:, 0:1]
        place.append(offset + incl - member[:, cols])
        tabs.append(jnp.where(lane == 0, base, jnp.where(lane == 1, count, 0.0)).astype(I32))
        base = base + count
    place = jnp.concatenate(place, axis=1)
    s1 = jnp.sum(jnp.where(hit1, place, 0.0), axis=0, keepdims=True)
    s2 = jnp.sum(jnp.where(hit2, place, 0.0), axis=0, keepdims=True)
    return (s1.astype(I32), s2.astype(I32)), (v1 / den, v2 / den), tabs, base


def _ln_and_route(v, lng_ref, lnb_ref, wrt_ref, br_ref, o_ref, slot_ref, gate_ref, tab_ref, tri, base, first, *, tm):
    @pl.when(first)
    def _():
        src = lax.broadcasted_iota(I32, (tm, tm), 0)
        dst = lax.broadcasted_iota(I32, (tm, tm), 1)
        tri[...] = jnp.where(src <= dst, 1.0, 0.0).astype(BF16)
        base[...] = jnp.zeros(base.shape, F32)

    x1 = _layer_norm(v, lng_ref[...], lnb_ref[...])
    o_ref[0] = x1
    slots, gates, tabs, new_base = _route_tile(x1, wrt_ref, br_ref, tri, base[...], tm)
    for h, tab in enumerate(tabs):
        for k in range(2):
            slot_ref[h, k:k + 1, :] = slots[k][:, h * tm:(h + 1) * tm]
            gate_ref[h, k:k + 1, :] = gates[k][:, h * tm:(h + 1) * tm]
        tab_ref[h] = tab
    base[...] = new_base


def _router_io(w_router, b_router, bsz, seq, ts, tm):
    d, n_exp = w_router.shape
    tm = min(tm, ts)
    assert ts % tm == 0 and n_exp % EXPERTS_PER_GROUP == 0 and tm <= 256
    t = bsz * seq
    per_seq = seq // ts
    in_specs = [_const_spec((n_exp, d)), _const_spec((n_exp, LANES))]
    args = (w_router.T.astype(F32), jnp.broadcast_to(b_router.astype(F32)[:, None], (n_exp, LANES)))
    per_tile = lambda b, s, *_: (b * per_seq + s, 0, 0)
    out_specs = [
        pl.BlockSpec((ts // tm, 2, tm), per_tile),
        pl.BlockSpec((ts // tm, 2, tm), per_tile),
        pl.BlockSpec((ts // tm, n_exp, LANES), per_tile),
    ]
    out_shape = [
        jax.ShapeDtypeStruct((t // tm, 2, tm), I32),
        jax.ShapeDtypeStruct((t // tm, 2, tm), F32),
        jax.ShapeDtypeStruct((t // tm, n_exp, LANES), I32),
    ]
    scratch = [pltpu.VMEM((tm, tm), BF16), pltpu.VMEM((n_exp, LANES), F32)]
    return in_specs, args, out_specs, out_shape, scratch, tm


def _segment_copies(src_of, dst_of, length, max_len, sem, *, wait=False, priority=0):
    done = 0
    for bit in reversed(range(max_len.bit_length())):
        n = 1 << bit
        piece = length & n

        @pl.when(piece != 0)
        def _(done=done, n=n):
            copy = pltpu.make_async_copy(src_of(done, n), dst_of(done, n), sem)
            copy.wait() if wait else copy.start(priority)

        done = done + piece


def _token_rows(ref, lo, n):
    return ref.at[pl.ds(pl.multiple_of(lo * SUBLANES, SUBLANES), n * SUBLANES)]


def _store_token_major(ref, val):
    for j in range(SUBLANES):
        ref[pl.ds(j, val.shape[0], stride=SUBLANES), :] = val[:, j * LANES:(j + 1) * LANES]


def _load_token_major(ref, rows):
    return jnp.concatenate([ref[pl.ds(j, rows, stride=SUBLANES), :] for j in range(SUBLANES)], axis=1)


def _dispatch_kernel(seg_dst_ref, seg_len_ref, pad_dst_ref, pad_len_ref, n_used_ref, x_ref, slot_ref, xs_hbm,
                     grouped, zeros, sems, pad_sem, *, tm, n_exp, bm, group):
    i = pl.program_id(0)
    n_tiles = pl.num_programs(0) * group

    def wait_tile(b):
        pltpu.make_async_copy(grouped.at[b], _token_rows(xs_hbm, 0, 2 * tm), sems.at[b]).wait()

    def pad_copies(wait):
        for e in range(n_exp):
            _segment_copies(lambda lo, n: _token_rows(zeros, lo, n),
                            lambda lo, n, e=e: _token_rows(xs_hbm, pad_dst_ref[e] + lo, n),
                            pad_len_ref[e], bm - 1, pad_sem, wait=wait)
        n_blk = xs_hbm.shape[0] // (bm * SUBLANES)
        for blk in range(n_blk - n_exp, n_blk):
            @pl.when(blk >= n_used_ref[0])
            def _(blk=blk):
                copy = pltpu.make_async_copy(zeros, _token_rows(xs_hbm, blk * bm, bm), pad_sem)
                copy.wait() if wait else copy.start()

    @pl.when(i == 0)
    def _():
        zeros[...] = jnp.zeros(zeros.shape, F32)
        pad_copies(wait=False)

    def tile(h, carry):
        k = i * group + h
        buf = lax.rem(k, 2)

        @pl.when(k >= 2)
        def _():
            wait_tile(buf)

        slots = slot_ref[h]
        row = lax.broadcasted_iota(I32, (2 * tm, tm), 0)
        onehot = jnp.where((row == slots[0:1, :]) | (row == slots[1:2, :]), 1.0, 0.0).astype(BF16)
        xt = x_ref[pl.ds(pl.multiple_of(h * tm, tm), tm), :].astype(BF16)
        _store_token_major(grouped.at[buf], jnp.dot(onehot, xt, preferred_element_type=F32))

        start = 0
        for e in range(n_exp):
            length = seg_len_ref[k * n_exp + e]
            dst = seg_dst_ref[k * n_exp + e]
            _segment_copies(lambda lo, n, start=start: _token_rows(grouped.at[buf], start + lo, n),
                            lambda lo, n, dst=dst: _token_rows(xs_hbm, dst + lo, n),
                            length, tm, sems.at[buf])
            start = start + length
        return carry

    lax.fori_loop(0, group, tile, 0)

    @pl.when(i == pl.num_programs(0) - 1)
    def _():
        wait_tile(lax.rem(n_tiles - 1, 2))

        @pl.when(n_tiles > 1)
        def _():
            wait_tile(lax.rem(n_tiles, 2))

        pad_copies(wait=True)


def _dispatch(xf, slots, seg_dst, seg_len, pad_dst, pad_len, n_used, n_pad, *, tm, bm, group):
    t, d = xf.shape
    assert d == SUBLANES * LANES and t % (tm * group) == 0
    n_exp = pad_dst.shape[0]
    grid_spec = pltpu.PrefetchScalarGridSpec(
        num_scalar_prefetch=5,
        grid=(t // (tm * group),),
        in_specs=[
            pl.BlockSpec((group * tm, d), lambda i, *_: (i, 0)),
            pl.BlockSpec((group, 2, tm), lambda i, *_: (i, 0, 0)),
        ],
        out_specs=pl.BlockSpec(memory_space=pl.ANY),
        scratch_shapes=[
            pltpu.VMEM((2, 2 * tm * SUBLANES, LANES), F32),
            pltpu.VMEM((bm * SUBLANES, LANES), F32),
            pltpu.SemaphoreType.DMA((2,)),
            pltpu.SemaphoreType.DMA(()),
        ],
    )
    return pl.pallas_call(
        functools.partial(_dispatch_kernel, tm=tm, n_exp=n_exp, bm=bm, group=group),
        grid_spec=grid_spec,
        out_shape=jax.ShapeDtypeStruct((n_pad * SUBLANES, LANES), F32),
        compiler_params=pltpu.CompilerParams(
            dimension_semantics=("arbitrary",), vmem_limit_bytes=VMEM_LIMIT),
        name="dispatch",
    )(seg_dst, seg_len, pad_dst, pad_len, n_used, xf, slots)


def _experts_kernel(first_blk_ref, n_blk_ref, n_used_ref, xs_hbm, w1_ref, w3_ref, w2_ref, y_hbm,
                    w1b, w3b, w2b, xbuf, ybuf, in_sems, out_sems, *, bm, n_exp):
    e = pl.program_id(0)
    first = first_blk_ref[e]
    n = n_blk_ref[e]

    n_x = xbuf.shape[0]
    ahead = n_x - 1

    def x_copy(blk, slot):
        return pltpu.make_async_copy(_token_rows(xs_hbm, blk * bm, bm), xbuf.at[slot], in_sems.at[slot])

    def y_copy(blk, slot):
        return pltpu.make_async_copy(ybuf.at[slot], _token_rows(y_hbm, blk * bm, bm), out_sems.at[slot])

    def fetch_head(first_blk, count):
        for k in range(ahead):
            @pl.when(k < count)
            def _(k=k):
                x_copy(first_blk + k, k).start(BLOCK_DMA_PRIORITY)

    @pl.when(e == 0)
    def _():
        fetch_head(first, n)

    w1b[...] = w1_ref[0, 0].astype(BF16)
    w3b[...] = w3_ref[0, 0].astype(BF16)
    w2b[...] = w2_ref[0, 0].astype(BF16)

    def block(j, carry):
        slot = lax.rem(j, 2)
        x_slot = lax.rem(j, n_x)

        @pl.when(j + ahead < n)
        def _():
            x_copy(first + j + ahead, lax.rem(j + ahead, n_x)).start(BLOCK_DMA_PRIORITY)

        x_copy(first + j, x_slot).wait()

        @pl.when(j >= 2)
        def _():
            y_copy(first + j - 2, slot).wait()

        xb = _load_token_major(xbuf.at[x_slot], bm).astype(BF16)
        h1 = jnp.dot(xb, w1b[...], preferred_element_type=F32)
        h3 = jnp.dot(xb, w3b[...], preferred_element_type=F32)
        hb = (h1 * jax.nn.sigmoid(h1) * h3).astype(BF16)
        _store_token_major(ybuf.at[slot], jnp.dot(hb, w2b[...], preferred_element_type=F32))
        y_copy(first + j, slot).start(BLOCK_DMA_PRIORITY)
        return carry

    lax.fori_loop(0, n, block, 0)

    nxt = jnp.minimum(e + 1, n_exp - 1)

    @pl.when(e + 1 < n_exp)
    def _():
        fetch_head(first_blk_ref[nxt], n_blk_ref[nxt])

    for back in (2, 1):
        @pl.when(n >= back)
        def _(back=back):
            y_copy(first + n - back, lax.rem(n - back, 2)).wait()

    @pl.when(e == n_exp - 1)
    def _():
        ybuf[0] = jnp.zeros(ybuf.shape[1:], F32)
        n_all = y_hbm.shape[0] // (bm * SUBLANES)
        for blk in range(n_all - n_exp, n_all):
            @pl.when(blk >= n_used_ref[0])
            def _(blk=blk):
                y_copy(blk, 0).start(BLOCK_DMA_PRIORITY)
                y_copy(blk, 0).wait()


def _experts(xs, first_blk, n_blk, n_used, w1, w3, w2, layer, *, bm):
    n_exp, d, d_e = w1.shape[-3:]
    assert d == SUBLANES * LANES
    w_blk = lambda e, *_: (layer, e, 0, 0)
    grid_spec = pltpu.PrefetchScalarGridSpec(
        num_scalar_prefetch=3,
        grid=(n_exp,),
        in_specs=[
            pl.BlockSpec(memory_space=pl.ANY),
            pl.BlockSpec((1, 1, d, d_e), w_blk),
            pl.BlockSpec((1, 1, d, d_e), w_blk),
            pl.BlockSpec((1, 1, d_e, d), w_blk),
        ],
        out_specs=pl.BlockSpec(memory_space=pl.ANY),
        scratch_shapes=[
            pltpu.VMEM((d, d_e), BF16), pltpu.VMEM((d, d_e), BF16), pltpu.VMEM((d_e, d), BF16),
            pltpu.VMEM((EXPERT_IN_SLOTS, bm * SUBLANES, LANES), F32), pltpu.VMEM((2, bm * SUBLANES, LANES), F32),
            pltpu.SemaphoreType.DMA((EXPERT_IN_SLOTS,)), pltpu.SemaphoreType.DMA((2,)),
        ],
    )
    return pl.pallas_call(
        functools.partial(_experts_kernel, bm=bm, n_exp=n_exp),
        grid_spec=grid_spec,
        out_shape=jax.ShapeDtypeStruct(xs.shape, F32),
        compiler_params=pltpu.CompilerParams(
            dimension_semantics=("arbitrary",), vmem_limit_bytes=VMEM_LIMIT),
        name="experts",
    )(first_blk, n_blk, n_used, xs, w1, w3, w2)


def _combine_tiles(step, n_tiles, seg_src_ref, seg_len_ref, load_x, slot_ref, gate_ref, lng_ref, lnb_ref, y_hbm,
                   store, grouped, sems, *, alpha, tm, n_exp, group):
    def fetch(tile, b, live=True):
        start = 0
        for e in range(n_exp):
            length = jnp.where(live, seg_len_ref[tile * n_exp + e], 0)
            src = seg_src_ref[tile * n_exp + e]
            _segment_copies(lambda lo, n, src=src: _token_rows(y_hbm, src + lo, n),
                            lambda lo, n, start=start: _token_rows(grouped.at[b], start + lo, n),
                            length, tm, sems.at[b], priority=BLOCK_DMA_PRIORITY)
            start = start + length

    n_buf = grouped.shape[0]
    ahead = n_buf - 1

    @pl.when(step == 0)
    def _():
        for k0 in range(ahead):
            @pl.when(k0 < n_tiles)
            def _(k0=k0):
                fetch(k0, k0)

    def tile(h, carry):
        k = step * group + h
        buf = lax.rem(k, n_buf)

        pltpu.make_async_copy(_token_rows(y_hbm, 0, 2 * tm), grouped.at[buf], sems.at[buf]).wait()

        slots = slot_ref[h]
        gates = gate_ref[h]
        row = lax.broadcasted_iota(I32, (2 * tm, tm), 0)
        g = (jnp.where(row == slots[0:1, :], gates[0:1, :], 0.0)
             + jnp.where(row == slots[1:2, :], gates[1:2, :], 0.0))
        yb = _load_token_major(grouped.at[buf], 2 * tm).astype(BF16)
        f = lax.dot_general(g.astype(BF16), yb, (((0,), (0,)), ((), ())), preferred_element_type=F32)

        nxt = k + ahead
        fetch(jnp.minimum(nxt, n_tiles - 1), lax.rem(nxt, n_buf), live=nxt < n_tiles)

        rows = pl.ds(pl.multiple_of(h * tm, tm), tm)
        store(rows, _layer_norm(alpha * load_x(rows) + f, lng_ref[...], lnb_ref[...]))
        return carry

    lax.fori_loop(0, group, tile, 0)


def _combine_kernel(seg_src_ref, seg_len_ref, x_ref, slot_ref, gate_ref, lng_ref, lnb_ref, y_hbm, o_ref,
                    grouped, sems, *, alpha, tm, n_exp, group):
    def store(rows, v):
        o_ref[rows, :] = v

    _combine_tiles(pl.program_id(0), pl.num_programs(0) * group, seg_src_ref, seg_len_ref,
                   lambda rows: x_ref[rows, :], slot_ref, gate_ref, lng_ref, lnb_ref, y_hbm, store, grouped, sems,
                   alpha=alpha, tm=tm, n_exp=n_exp, group=group)


def _combine(xf, y, slots, gates, seg_src, seg_len, ln_g, ln_b, *, alpha, tm, n_exp, group):
    t, d = xf.shape
    assert t % (tm * group) == 0
    grid_spec = pltpu.PrefetchScalarGridSpec(
        num_scalar_prefetch=2,
        grid=(t // (tm * group),),
        in_specs=[
            pl.BlockSpec((group * tm, d), lambda i, *_: (i, 0)),
            pl.BlockSpec((group, 2, tm), lambda i, *_: (i, 0, 0)),
            pl.BlockSpec((group, 2, tm), lambda i, *_: (i, 0, 0)),
            pl.BlockSpec((1, d), lambda i, *_: (0, 0)),
            pl.BlockSpec((1, d), lambda i, *_: (0, 0)),
            pl.BlockSpec(memory_space=pl.ANY),
        ],
        out_specs=pl.BlockSpec((group * tm, d), lambda i, *_: (i, 0)),
        scratch_shapes=[pltpu.VMEM((COMBINE_SLOTS, 2 * tm * SUBLANES, LANES), F32),
                        pltpu.SemaphoreType.DMA((COMBINE_SLOTS,))],
    )
    return pl.pallas_call(
        functools.partial(_combine_kernel, alpha=alpha, tm=tm, n_exp=n_exp, group=group),
        grid_spec=grid_spec,
        out_shape=jax.ShapeDtypeStruct((t, d), F32),
        compiler_params=pltpu.CompilerParams(
            dimension_semantics=("arbitrary",), vmem_limit_bytes=VMEM_LIMIT),
        name="combine",
    )(seg_src, seg_len, xf, slots, gates, _row(ln_g), _row(ln_b), y)


def _moe_rows(x, slots, tab, w1, w3, w2, layer, *, bm=EXPERT_BLOCK):
    bsz, seq, d = x.shape
    t = bsz * seq
    n_exp = tab.shape[1]
    tm = t // tab.shape[0]
    xf = x.reshape(t, d)
    before, in_tile = tab[:, :, 0], tab[:, :, 1]
    counts = before[-1] + in_tile[-1]
    padded = (counts + bm - 1) // bm * bm
    pend = jnp.cumsum(padded)
    pstart = pend - padded
    n_blk = -(-(2 * t) // bm) + n_exp
    n_used = (pend[-1] // bm).reshape(1).astype(I32)
    seg_pos = (pstart[None, :] + before).reshape(-1).astype(I32)
    seg_len = in_tile.reshape(-1).astype(I32)
    group = math.gcd(TILES_PER_STEP, t // tm)
    xs = _dispatch(xf, slots, seg_pos, seg_len, (pstart + counts).astype(I32), (padded - counts).astype(I32),
                   n_used, n_blk * bm, tm=tm, bm=bm, group=group)
    y = _experts(xs, (pstart // bm).astype(I32), (padded // bm).astype(I32), n_used, w1, w3, w2, layer, bm=bm)
    return y, seg_pos, seg_len


def _moe_ln(x, y, slots, gates, seg_pos, seg_len, ln_g, ln_b, *, alpha):
    bsz, seq, d = x.shape
    t = bsz * seq
    n_tiles, _, tm = slots.shape
    group = math.gcd(TILES_PER_STEP, n_tiles)
    out = _combine(x.reshape(t, d), y, slots, gates, seg_pos, seg_len, ln_g, ln_b,
                   alpha=alpha, tm=tm, n_exp=seg_len.shape[0] // n_tiles, group=group)
    return out.reshape(bsz, seq, d)


def kernel(x, ln1_g, ln1_b, ln2_g, ln2_b, even_w_in, even_a_dw, even_a_dw_b, even_a_ln_g, even_a_ln_b, even_b_dw, even_w_out, odd_w_in, odd_c_dw, odd_c_dw_b, odd_w_gate_a, odd_b_gate_a, odd_w_gate_x, odd_b_gate_x, odd_lam, odd_w_out, w_router, b_router, moe_w1, moe_w3, moe_w2):
    depth = ln1_g.shape[0]
    alpha = (2.0 * depth) ** 0.25
    moe = None
    for layer in range(depth):
        j = layer // 2
        if layer % 2 == 0:
            if moe is not None:
                x = _moe_ln(x, *moe, alpha=alpha)
            x, slots, gates, tab = _mix0(
                x, even_w_in[j], even_a_dw[j], even_a_dw_b[j], even_a_ln_g[j], even_a_ln_b[j],
                even_b_dw[j], even_w_out[j], ln1_g[layer], ln1_b[layer], w_router, b_router, alpha=alpha)
        else:
            x, slots, gates, tab = _mix1(
                x, odd_w_in[j], odd_c_dw[j], odd_c_dw_b[j], odd_w_gate_a[j], odd_b_gate_a[j],
                odd_w_gate_x[j], odd_b_gate_x[j], odd_lam[j], odd_w_out[j], ln1_g[layer], ln1_b[layer],
                w_router, b_router, alpha=alpha, moe=moe)
        y, seg_pos, seg_len = _moe_rows(x, slots, tab, moe_w1, moe_w3, moe_w2, layer)
        moe = (y, slots, gates, seg_pos, seg_len, ln2_g[layer], ln2_b[layer])
    return x if moe is None else _moe_ln(x, *moe, alpha=alpha)
```

```python
import functools
import math

import jax
import jax.numpy as jnp
from jax import lax
from jax.experimental import pallas as pl
from jax.experimental.pallas import tpu as pltpu

F32 = jnp.float32
BF16 = jnp.bfloat16
I32 = jnp.int32
LN_EPS = 1e-5
LRU_C = 8.0
SUBLANES = 8
LANES = 128
HALO_A = 32
HALO_B = 8
EXPERTS_PER_GROUP = 4
BLOCK_DMA_PRIORITY = 1
TILES_PER_STEP = 4
COMBINE_SLOTS = 3
DISPATCH_SLOTS = 3
EXPERT_IN_SLOTS = 4
SEQ_TILE = 512
TOKEN_TILE = 256
EXPERT_BLOCK = 256
VMEM_LIMIT = 56 * 1024 * 1024


def _layer_norm(v, g, b):
    mu = jnp.mean(v, axis=-1, keepdims=True)
    c = v - mu
    var = jnp.mean(c * c, axis=-1, keepdims=True)
    return c * lax.rsqrt(var + LN_EPS) * g + b


def _row(v):
    return v.reshape(1, -1).astype(F32)


def _tap_rows(w):
    return jnp.broadcast_to(w.astype(F32)[:, None, :], (w.shape[0], SUBLANES, w.shape[1]))


def _tap(w_ref, k, rows):
    return jnp.concatenate([w_ref[k]] * (rows // SUBLANES), axis=0)


def _const_spec(shape):
    return pl.BlockSpec(shape, lambda *_: (0,) * len(shape))


def _mix0_kernel(x_ref, win_ref, adw_ref, adwb_ref, alng_ref, alnb_ref, bdw_ref, wout_ref, lng_ref, lnb_ref,
                 wrt_ref, br_ref, o_ref, slot_ref, gate_ref, tab_ref, *scratch, alpha, ts, tm, d_a, d_b, rc, parts):
    tri, base = scratch[-2:]
    bufs = [scratch[5 * p:5 * p + 5] for p in range(parts)]
    first = (pl.program_id(0) == 0) & (pl.program_id(1) == 0)
    n_a = adw_ref.shape[0]
    n_b = bdw_ref.shape[0]
    hs = ts // parts

    @pl.when(pl.program_id(1) == 0)
    def _():
        bufs[0][0][0:HALO_A, :] = jnp.zeros((HALO_A, d_a), F32)
        bufs[0][2][0:HALO_B, :] = jnp.zeros((HALO_B, d_b), F32)

    x = x_ref[0]
    xb = x.astype(BF16)
    cw = 2 * LANES
    dot = functools.partial(jnp.dot, preferred_element_type=F32)

    def proj_units(p):
        abuf, ash, cbuf, bgate, cat = bufs[p]
        xp = xb[p * hs:(p + 1) * hs, :]
        units = []
        for c0 in range(0, d_a, cw):
            def unit(c0=c0):
                val = dot(xp, win_ref[:, c0:c0 + cw])
                gate = dot(xp, win_ref[:, d_a + c0:d_a + c0 + cw])
                abuf[HALO_A:HALO_A + hs, c0:c0 + cw] = val * jax.nn.sigmoid(gate)
            units.append(unit)
        for c0 in range(0, d_b, cw):
            def unit(c0=c0, o=2 * d_a):
                bgate[:, c0:c0 + cw] = dot(xp, win_ref[:, o + c0:o + c0 + cw])
                cbuf[HALO_B:HALO_B + hs, c0:c0 + cw] = (dot(xp, win_ref[:, o + d_b + c0:o + d_b + c0 + cw])
                                                        * dot(xp, win_ref[:, o + 2 * d_b + c0:o + 2 * d_b + c0 + cw]))
            units.append(unit)
        return units

    def conv_units(p):
        abuf, ash, cbuf, bgate, cat = bufs[p]

        def shifted():
            n_sh = hs + HALO_A - SUBLANES
            for j in range(1, SUBLANES):
                ash[j - 1, :, :] = abuf[j:j + n_sh, :]
        units = [shifted]
        for c in range(hs // rc):
            def unit(r0=c * rc):
                acc = adwb_ref[...]
                for k in range(n_a):
                    q, j = divmod(HALO_A - (n_a - 1) + k, SUBLANES)
                    lo = r0 + q * SUBLANES
                    src = abuf[lo:lo + rc, :] if j == 0 else ash[j - 1, lo:lo + rc, :]
                    acc = acc + _tap(adw_ref, k, rc) * src
                a = _layer_norm(acc, alng_ref[...], alnb_ref[...])
                cat[r0:r0 + rc, 0:d_a] = (a * jax.nn.sigmoid(a)).astype(BF16)
                bb = jnp.zeros((rc, d_b), F32)
                for k in range(n_b):
                    lo = r0 + HALO_B - (n_b - 1) + k
                    bb = bb + _tap(bdw_ref, k, rc) * cbuf[lo:lo + rc, :]
                cat[r0:r0 + rc, d_a:d_a + d_b] = (bgate[r0:r0 + rc, :] * bb).astype(BF16)
            units.append(unit)
        return units

    m_cols = [[] for _ in range(parts)]

    def out_units(p):
        cat = bufs[p][4]
        units = []
        for c0 in range(0, wout_ref.shape[1], cw):
            def unit(c0=c0):
                m_cols[p].append(dot(cat[...], wout_ref[:, c0:c0 + cw]))
            units.append(unit)
        return units

    def run_interleaved(*groups):
        order = sorted(((i + 0.5) / len(g), gi, i) for gi, g in enumerate(groups) for i in range(len(g)))
        for _, gi, i in order:
            groups[gi][i]()

    for s in range(parts + 2):
        groups = []
        if s < parts:
            groups.append(proj_units(s))
        if 0 <= s - 1 < parts:
            groups.append(conv_units(s - 1))
        if 0 <= s - 2 < parts:
            groups.append(out_units(s - 2))
        run_interleaved(*groups)
        if s + 1 < parts:
            bufs[s + 1][0][0:HALO_A, :] = bufs[s][0][hs:hs + HALO_A, :]
            bufs[s + 1][2][0:HALO_B, :] = bufs[s][2][hs:hs + HALO_B, :]

    bufs[0][0][0:HALO_A, :] = bufs[-1][0][hs:hs + HALO_A, :]
    bufs[0][2][0:HALO_B, :] = bufs[-1][2][hs:hs + HALO_B, :]

    m = jnp.concatenate([jnp.concatenate(cols, axis=1) for cols in m_cols], axis=0)
    _ln_and_route(alpha * x + m, lng_ref, lnb_ref, wrt_ref, br_ref, o_ref, slot_ref, gate_ref, tab_ref,
                  tri, base, first, tm=tm)


def _mix0(x, w_in, a_dw, a_dw_b, a_ln_g, a_ln_b, b_dw, w_out, ln_g, ln_b, w_router, b_router,
          *, alpha, ts=SEQ_TILE, tm=TOKEN_TILE, rc=32, parts=2):
    bsz, seq, d = x.shape
    d_a = a_dw.shape[-1]
    d_b = b_dw.shape[-1]
    assert a_dw.shape[0] - 1 <= HALO_A and b_dw.shape[0] - 1 <= HALO_B
    ts = min(ts, seq)
    hs = ts // parts
    rc = min(rc, hs)
    assert seq % ts == 0 and ts % parts == 0 and hs % rc == 0 and rc % (2 * SUBLANES) == 0
    r_in, r_args, r_out_specs, r_out_shape, r_scratch, tm = _router_io(w_router, b_router, bsz, seq, ts, tm)
    kern = functools.partial(_mix0_kernel, alpha=alpha, ts=ts, tm=tm, d_a=d_a, d_b=d_b, rc=rc, parts=parts)
    part_scratch = [
        pltpu.VMEM((HALO_A + hs, d_a), F32),
        pltpu.VMEM((SUBLANES - 1, HALO_A + hs - SUBLANES, d_a), F32),
        pltpu.VMEM((HALO_B + hs, d_b), F32),
        pltpu.VMEM((hs, d_b), F32),
        pltpu.VMEM((hs, d_a + d_b), BF16),
    ]
    return pl.pallas_call(
        kern,
        grid=(bsz, seq // ts),
        in_specs=[
            pl.BlockSpec((1, ts, d), lambda b, s: (b, s, 0)),
            _const_spec(w_in.shape), _const_spec((a_dw.shape[0], SUBLANES, d_a)), _const_spec((1, d_a)),
            _const_spec((1, d_a)), _const_spec((1, d_a)), _const_spec((b_dw.shape[0], SUBLANES, d_b)),
            _const_spec(w_out.shape),
            _const_spec((1, d)), _const_spec((1, d)),
        ] + r_in,
        out_specs=[pl.BlockSpec((1, ts, d), lambda b, s: (b, s, 0))] + r_out_specs,
        out_shape=[jax.ShapeDtypeStruct(x.shape, F32)] + r_out_shape,
        scratch_shapes=part_scratch * parts + r_scratch,
        compiler_params=pltpu.CompilerParams(
            dimension_semantics=("arbitrary", "arbitrary"), vmem_limit_bytes=VMEM_LIMIT),
        name="mix0",
    )(x, w_in.astype(BF16), _tap_rows(a_dw), _row(a_dw_b), _row(a_ln_g), _row(a_ln_b),
      _tap_rows(b_dw), w_out.astype(BF16), _row(ln_g), _row(ln_b), *r_args)


def _mix1_kernel(x_ref, win_ref, cdw_ref, cdwb_ref, wga_ref, bga_ref, wgx_ref, bgx_ref, lam_ref, wout_ref,
                 lng_ref, lnb_ref, wrt_ref, br_ref, o_ref, slot_ref, gate_ref, tab_ref,
                 cbuf, ybr, abuf, bbuf, hcar, tri, base, *, alpha, ts, tm, d_rnn, rc):
    first = (pl.program_id(0) == 0) & (pl.program_id(1) == 0)

    @pl.when(pl.program_id(1) == 0)
    def _():
        cbuf[0:HALO_B, :] = jnp.zeros((HALO_B, d_rnn), F32)
        hcar[...] = jnp.zeros((1, d_rnn), F32)

    x = x_ref[0] if len(x_ref.shape) == 3 else x_ref[...]
    xb = x.astype(BF16)
    ybr[...] = jnp.dot(xb, win_ref[:, 0:d_rnn], preferred_element_type=F32)
    cbuf[HALO_B:HALO_B + ts, :] = jnp.dot(xb, win_ref[:, d_rnn:2 * d_rnn], preferred_element_type=F32)

    n_heads, blk = wga_ref.shape[0], wga_ref.shape[1]
    n_c = cdw_ref.shape[0]
    neg_c_sp = -LRU_C * jax.nn.softplus(-lam_ref[...])
    for c in range(ts // rc):
        r0 = c * rc
        xr = cdwb_ref[...]
        for k in range(n_c):
            lo = r0 + HALO_B - (n_c - 1) + k
            xr = xr + _tap(cdw_ref, k, rc) * cbuf[lo:lo + rc, :]
        xrb = xr.astype(BF16)
        for h in range(n_heads):
            cs = slice(h * blk, (h + 1) * blk)
            r = jax.nn.sigmoid(jnp.dot(xrb[:, cs], wga_ref[h], preferred_element_type=F32) + bga_ref[:, cs])
            i = jax.nn.sigmoid(jnp.dot(xrb[:, cs], wgx_ref[h], preferred_element_type=F32) + bgx_ref[:, cs])
            log_a = neg_c_sp[:, cs] * r
            abuf[r0:r0 + rc, cs] = jnp.exp(log_a)
            th = jnp.tanh(log_a)
            bbuf[r0:r0 + rc, cs] = jnp.sqrt(-2.0 * th / (1.0 - th)) * (i * xr[:, cs])

    cbuf[0:HALO_B, :] = cbuf[ts:ts + HALO_B, :]

    rowid = lax.broadcasted_iota(I32, (SUBLANES, d_rnn), 0)

    def group(g, h_prev):
        r = pl.multiple_of(g * SUBLANES, SUBLANES)
        a = abuf[pl.ds(r, SUBLANES), :]
        b = bbuf[pl.ds(r, SUBLANES), :]
        step = 1
        while step < SUBLANES:
            keep = rowid >= step
            b = jnp.where(keep, a * pltpu.roll(b, step, axis=0) + b, b)
            a = jnp.where(keep, a * pltpu.roll(a, step, axis=0), a)
            step *= 2
        h = a * h_prev + b
        bbuf[pl.ds(r, SUBLANES), :] = h
        return h[SUBLANES - 1:SUBLANES, :]

    hcar[...] = lax.fori_loop(0, ts // SUBLANES, group, hcar[...], unroll=4)

    out = (jax.nn.gelu(ybr[...]) * bbuf[...]).astype(BF16)
    m = jnp.dot(out, wout_ref[...], preferred_element_type=F32)
    _ln_and_route(alpha * x + m, lng_ref, lnb_ref, wrt_ref, br_ref, o_ref, slot_ref, gate_ref, tab_ref,
                  tri, base, first, tm=tm)


def _mix1_after_moe_kernel(seg_src_ref, seg_len_ref, x_ref, cslot_ref, cgate_ref, clng_ref, clnb_ref, y_hbm, *rest,
                           alpha, ts, tm, d_rnn, rc, n_exp):
    *mix, xbuf, grouped, csems = rest
    group = ts // tm
    step = pl.program_id(0) * pl.num_programs(1) + pl.program_id(1)

    def store(rows, v):
        xbuf[rows, :] = v

    _combine_tiles(step, pl.num_programs(0) * pl.num_programs(1) * group, seg_src_ref, seg_len_ref,
                   lambda rows: x_ref[0, rows, :], cslot_ref, cgate_ref, clng_ref, clnb_ref, y_hbm, store,
                   grouped, csems, alpha=alpha, tm=tm, n_exp=n_exp, group=group)
    _mix1_kernel(xbuf, *mix, alpha=alpha, ts=ts, tm=tm, d_rnn=d_rnn, rc=rc)


def _mix1(x, w_in, c_dw, c_dw_b, w_gate_a, b_gate_a, w_gate_x, b_gate_x, lam, w_out, ln_g, ln_b,
          w_router, b_router, *, alpha, ts=SEQ_TILE, tm=TOKEN_TILE, rc=128, moe=None):
    bsz, seq, d = x.shape
    d_rnn = c_dw.shape[-1]
    assert c_dw.shape[0] - 1 <= HALO_B
    ts = min(ts, seq)
    rc = min(rc, ts)
    assert seq % ts == 0 and ts % rc == 0 and ts % SUBLANES == 0
    r_in, r_args, r_out_specs, r_out_shape, r_scratch, tm = _router_io(w_router, b_router, bsz, seq, ts, tm)
    kern = functools.partial(_mix1_kernel, alpha=alpha, ts=ts, tm=tm, d_rnn=d_rnn, rc=rc)
    in_specs = [
        pl.BlockSpec((1, ts, d), lambda b, s, *_: (b, s, 0)),
        _const_spec(w_in.shape), _const_spec((c_dw.shape[0], SUBLANES, d_rnn)), _const_spec((1, d_rnn)),
        _const_spec(w_gate_a.shape), _const_spec((1, d_rnn)),
        _const_spec(w_gate_x.shape), _const_spec((1, d_rnn)),
        _const_spec((1, d_rnn)), _const_spec(w_out.shape), _const_spec((1, d)), _const_spec((1, d)),
    ] + r_in
    scratch = [
        pltpu.VMEM((HALO_B + ts, d_rnn), F32),
        pltpu.VMEM((ts, d_rnn), F32),
        pltpu.VMEM((ts, d_rnn), F32),
        pltpu.VMEM((ts, d_rnn), F32),
        pltpu.VMEM((1, d_rnn), F32),
    ] + r_scratch
    args = (x, w_in.astype(BF16), _tap_rows(c_dw), _row(c_dw_b), w_gate_a.astype(BF16), _row(b_gate_a),
            w_gate_x.astype(BF16), _row(b_gate_x), _row(lam), w_out.astype(BF16), _row(ln_g), _row(ln_b), *r_args)
    prefetch = ()
    if moe is not None:
        y, m_slots, m_gates, seg_src, seg_len, m_ln_g, m_ln_b = moe
        n_exp = w_router.shape[1]
        assert m_slots.shape == (bsz * seq // tm, 2, tm) and d == SUBLANES * LANES
        kern = functools.partial(_mix1_after_moe_kernel, alpha=alpha, ts=ts, tm=tm, d_rnn=d_rnn, rc=rc, n_exp=n_exp)
        per_tile = lambda b, s, *_: (b * (seq // ts) + s, 0, 0)
        in_specs = [in_specs[0], pl.BlockSpec((ts // tm, 2, tm), per_tile), pl.BlockSpec((ts // tm, 2, tm), per_tile),
                    _const_spec((1, d)), _const_spec((1, d)), pl.BlockSpec(memory_space=pl.ANY)] + in_specs[1:]
        scratch = scratch + [pltpu.VMEM((ts, d), F32), pltpu.VMEM((COMBINE_SLOTS, 2 * tm * SUBLANES, LANES), F32),
                             pltpu.SemaphoreType.DMA((COMBINE_SLOTS,))]
        args = (x, m_slots, m_gates, _row(m_ln_g), _row(m_ln_b), y) + args[1:]
        prefetch = (seg_src, seg_len)
    grid_spec = pltpu.PrefetchScalarGridSpec(
        num_scalar_prefetch=len(prefetch),
        grid=(bsz, seq // ts),
        in_specs=in_specs,
        out_specs=[pl.BlockSpec((1, ts, d), lambda b, s, *_: (b, s, 0))] + r_out_specs,
        scratch_shapes=scratch,
    )
    return pl.pallas_call(
        kern,
        grid_spec=grid_spec,
        out_shape=[jax.ShapeDtypeStruct(x.shape, F32)] + r_out_shape,
        compiler_params=pltpu.CompilerParams(
            dimension_semantics=("arbitrary", "arbitrary"), vmem_limit_bytes=VMEM_LIMIT),
        name="mix1",
    )(*prefetch, *args)


def _first_argmax(vals):
    best, idx = vals[0], jnp.zeros(vals[0].shape, I32)
    for j in range(1, len(vals)):
        upd = vals[j] > best
        best = jnp.where(upd, vals[j], best)
        idx = jnp.where(upd, j, idx)
    return best, idx


def _route_tile(x, wrt_ref, br_ref, tri, base, tm):
    n_exp = wrt_ref.shape[0]
    tt = x.shape[0]
    w = wrt_ref[...]
    w_hi = w.astype(BF16)
    w_lo = (w - w_hi.astype(F32)).astype(BF16)
    x_hi = x.astype(BF16)
    x_lo = (x - x_hi.astype(F32)).astype(BF16)
    over_d = (((1,), (1,)), ((), ()))
    logits = (lax.dot_general(w_hi, x_hi, over_d, preferred_element_type=F32)
              + lax.dot_general(w_hi, x_lo, over_d, preferred_element_type=F32)
              + lax.dot_general(w_lo, x_hi, over_d, preferred_element_type=F32))
    logits = logits + br_ref[:, 0:1]
    e = jnp.exp(logits - jnp.max(logits, axis=0, keepdims=True))
    p = e / jnp.sum(e, axis=0, keepdims=True)
    rows = [p[j:j + 1, :] for j in range(n_exp)]

    n_grp = n_exp // EXPERTS_PER_GROUP
    scores = []
    for g in range(n_grp):
        a, b, c, d = rows[EXPERTS_PER_GROUP * g:EXPERTS_PER_GROUP * (g + 1)]
        scores.append(jnp.maximum(jnp.maximum(a, b) + jnp.maximum(c, d), jnp.maximum(a + b, c + d)))
    _, g_sel = _first_argmax(scores)

    p_in = []
    for j in range(EXPERTS_PER_GROUP):
        v = rows[j]
        for g in range(1, n_grp):
            v = jnp.where(g_sel == g, rows[EXPERTS_PER_GROUP * g + j], v)
        p_in.append(v)
    v1, i1 = _first_argmax(p_in)
    v2, i2 = _first_argmax([jnp.where(i1 == j, -1.0, p_in[j]) for j in range(EXPERTS_PER_GROUP)])
    den = v1 + v2
    e1 = g_sel * EXPERTS_PER_GROUP + i1
    e2 = g_sel * EXPERTS_PER_GROUP + i2

    eid = lax.broadcasted_iota(I32, (n_exp, tt), 0)
    hit1 = eid == e1
    hit2 = eid == e2
    member = jnp.where(hit1 | hit2, 1.0, 0.0)
    member_b = member.astype(BF16)
    lower = (lax.broadcasted_iota(I32, (n_exp, n_exp), 1) < lax.broadcasted_iota(I32, (n_exp, n_exp), 0))
    lower = jnp.where(lower, 1.0, 0.0).astype(BF16)
    lane = lax.broadcasted_iota(I32, (n_exp, LANES), 1)
    place, tabs = [], []
    for h in range(tt // tm):
        cols = slice(h * tm, (h + 1) * tm)
        incl = jnp.dot(member_b[:, cols], tri[...], preferred_element_type=F32)
        count = jnp.sum(member[:, cols], axis=1, keepdims=True)
        offset = jnp.dot(lower, jnp.broadcast_to(count, (n_exp, LANES)).astype(BF16),
                         preferred_element_type=F32)[:, 0:1]
        place.append(offset + incl - member[:, cols])
        tabs.append(jnp.where(lane == 0, base, jnp.where(lane == 1, count, 0.0)).astype(I32))
        base = base + count
    place = jnp.concatenate(place, axis=1)
    s1 = jnp.sum(jnp.where(hit1, place, 0.0), axis=0, keepdims=True)
    s2 = jnp.sum(jnp.where(hit2, place, 0.0), axis=0, keepdims=True)
    return (s1.astype(I32), s2.astype(I32)), (v1 / den, v2 / den), tabs, base


def _ln_and_route(v, lng_ref, lnb_ref, wrt_ref, br_ref, o_ref, slot_ref, gate_ref, tab_ref, tri, base, first, *, tm):
    @pl.when(first)
    def _():
        src = lax.broadcasted_iota(I32, (tm, tm), 0)
        dst = lax.broadcasted_iota(I32, (tm, tm), 1)
        tri[...] = jnp.where(src <= dst, 1.0, 0.0).astype(BF16)
        base[...] = jnp.zeros(base.shape, F32)

    x1 = _layer_norm(v, lng_ref[...], lnb_ref[...])
    o_ref[0] = x1
    slots, gates, tabs, new_base = _route_tile(x1, wrt_ref, br_ref, tri, base[...], tm)
    for h, tab in enumerate(tabs):
        for k in range(2):
            slot_ref[h, k:k + 1, :] = slots[k][:, h * tm:(h + 1) * tm]
            gate_ref[h, k:k + 1, :] = gates[k][:, h * tm:(h + 1) * tm]
        tab_ref[h] = tab
    base[...] = new_base


def _router_io(w_router, b_router, bsz, seq, ts, tm):
    d, n_exp = w_router.shape
    tm = min(tm, ts)
    assert ts % tm == 0 and n_exp % EXPERTS_PER_GROUP == 0 and tm <= 256
    t = bsz * seq
    per_seq = seq // ts
    in_specs = [_const_spec((n_exp, d)), _const_spec((n_exp, LANES))]
    args = (w_router.T.astype(F32), jnp.broadcast_to(b_router.astype(F32)[:, None], (n_exp, LANES)))
    per_tile = lambda b, s, *_: (b * per_seq + s, 0, 0)
    out_specs = [
        pl.BlockSpec((ts // tm, 2, tm), per_tile),
        pl.BlockSpec((ts // tm, 2, tm), per_tile),
        pl.BlockSpec((ts // tm, n_exp, LANES), per_tile),
    ]
    out_shape = [
        jax.ShapeDtypeStruct((t // tm, 2, tm), I32),
        jax.ShapeDtypeStruct((t // tm, 2, tm), F32),
        jax.ShapeDtypeStruct((t // tm, n_exp, LANES), I32),
    ]
    scratch = [pltpu.VMEM((tm, tm), BF16), pltpu.VMEM((n_exp, LANES), F32)]
    return in_specs, args, out_specs, out_shape, scratch, tm


def _segment_copies(src_of, dst_of, length, max_len, sem, *, wait=False, priority=0):
    done = 0
    for bit in reversed(range(max_len.bit_length())):
        n = 1 << bit
        piece = length & n

        @pl.when(piece != 0)
        def _(done=done, n=n):
            copy = pltpu.make_async_copy(src_of(done, n), dst_of(done, n), sem)
            copy.wait() if wait else copy.start(priority)

        done = done + piece


def _token_rows(ref, lo, n):
    return ref.at[pl.ds(pl.multiple_of(lo * SUBLANES, SUBLANES), n * SUBLANES)]


def _store_token_major(ref, val):
    for j in range(SUBLANES):
        ref[pl.ds(j, val.shape[0], stride=SUBLANES), :] = val[:, j * LANES:(j + 1) * LANES]


def _load_token_major(ref, rows):
    return jnp.concatenate([ref[pl.ds(j, rows, stride=SUBLANES), :] for j in range(SUBLANES)], axis=1)


def _dispatch_kernel(seg_dst_ref, seg_len_ref, pad_dst_ref, pad_len_ref, n_used_ref, x_ref, slot_ref, xs_hbm,
                     grouped, zeros, sems, pad_sem, *, tm, n_exp, bm, group):
    i = pl.program_id(0)
    n_tiles = pl.num_programs(0) * group

    def wait_tile(b):
        pltpu.make_async_copy(grouped.at[b], _token_rows(xs_hbm, 0, 2 * tm), sems.at[b]).wait()

    def pad_copies(wait):
        for e in range(n_exp):
            _segment_copies(lambda lo, n: _token_rows(zeros, lo, n),
                            lambda lo, n, e=e: _token_rows(xs_hbm, pad_dst_ref[e] + lo, n),
                            pad_len_ref[e], bm - 1, pad_sem, wait=wait)
        n_blk = xs_hbm.shape[0] // (bm * SUBLANES)
        for blk in range(n_blk - n_exp, n_blk):
            @pl.when(blk >= n_used_ref[0])
            def _(blk=blk):
                copy = pltpu.make_async_copy(zeros, _token_rows(xs_hbm, blk * bm, bm), pad_sem)
                copy.wait() if wait else copy.start()

    @pl.when(i == 0)
    def _():
        zeros[...] = jnp.zeros(zeros.shape, F32)
        pad_copies(wait=False)

    n_buf = grouped.shape[0]

    def send(tile, live=True):
        b = lax.rem(tile, n_buf)
        start = 0
        for e in range(n_exp):
            length = jnp.where(live, seg_len_ref[tile * n_exp + e], 0)
            dst = seg_dst_ref[tile * n_exp + e]
            _segment_copies(lambda lo, n, start=start: _token_rows(grouped.at[b], start + lo, n),
                            lambda lo, n, dst=dst: _token_rows(xs_hbm, dst + lo, n),
                            length, tm, sems.at[b])
            start = start + length

    def tile(h, carry):
        k = i * group + h
        buf = lax.rem(k, n_buf)

        @pl.when(k >= n_buf)
        def _():
            wait_tile(buf)

        slots = slot_ref[h]
        row = lax.broadcasted_iota(I32, (2 * tm, tm), 0)
        onehot = jnp.where((row == slots[0:1, :]) | (row == slots[1:2, :]), 1.0, 0.0).astype(BF16)
        xt = x_ref[pl.ds(pl.multiple_of(h * tm, tm), tm), :].astype(BF16)
        rows = jnp.dot(onehot, xt, preferred_element_type=F32)
        send(jnp.maximum(k - 1, 0), live=k >= 1)
        _store_token_major(grouped.at[buf], rows)
        return carry

    lax.fori_loop(0, group, tile, 0)

    @pl.when(i == pl.num_programs(0) - 1)
    def _():
        send(n_tiles - 1)
        for back in range(n_buf):
            @pl.when(n_tiles > back)
            def _(back=back):
                wait_tile(lax.rem(n_tiles - 1 - back, n_buf))

        pad_copies(wait=True)


def _dispatch(xf, slots, seg_dst, seg_len, pad_dst, pad_len, n_used, n_pad, *, tm, bm, group):
    t, d = xf.shape
    assert d == SUBLANES * LANES and t % (tm * group) == 0
    n_exp = pad_dst.shape[0]
    grid_spec = pltpu.PrefetchScalarGridSpec(
        num_scalar_prefetch=5,
        grid=(t // (tm * group),),
        in_specs=[
            pl.BlockSpec((group * tm, d), lambda i, *_: (i, 0)),
            pl.BlockSpec((group, 2, tm), lambda i, *_: (i, 0, 0)),
        ],
        out_specs=pl.BlockSpec(memory_space=pl.ANY),
        scratch_shapes=[
            pltpu.VMEM((DISPATCH_SLOTS, 2 * tm * SUBLANES, LANES), F32),
            pltpu.VMEM((bm * SUBLANES, LANES), F32),
            pltpu.SemaphoreType.DMA((DISPATCH_SLOTS,)),
            pltpu.SemaphoreType.DMA(()),
        ],
    )
    return pl.pallas_call(
        functools.partial(_dispatch_kernel, tm=tm, n_exp=n_exp, bm=bm, group=group),
        grid_spec=grid_spec,
        out_shape=jax.ShapeDtypeStruct((n_pad * SUBLANES, LANES), F32),
        compiler_params=pltpu.CompilerParams(
            dimension_semantics=("arbitrary",), vmem_limit_bytes=VMEM_LIMIT),
        name="dispatch",
    )(seg_dst, seg_len, pad_dst, pad_len, n_used, xf, slots)


def _experts_kernel(first_blk_ref, n_blk_ref, n_used_ref, xs_hbm, w1_ref, w3_ref, w2_ref, y_hbm,
                    w1b, w3b, w2b, xbuf, ybuf, in_sems, out_sems, *, bm, n_exp):
    e = pl.program_id(0)
    first = first_blk_ref[e]
    n = n_blk_ref[e]

    n_x = xbuf.shape[0]
    ahead = n_x - 1

    def x_copy(blk, slot):
        return pltpu.make_async_copy(_token_rows(xs_hbm, blk * bm, bm), xbuf.at[slot], in_sems.at[slot])

    def y_copy(blk, slot):
        return pltpu.make_async_copy(ybuf.at[slot], _token_rows(y_hbm, blk * bm, bm), out_sems.at[slot])

    def fetch_head(first_blk, count):
        for k in range(ahead):
            @pl.when(k < count)
            def _(k=k):
                x_copy(first_blk + k, k).start(BLOCK_DMA_PRIORITY)

    @pl.when(e == 0)
    def _():
        fetch_head(first, n)

    w1b[...] = w1_ref[0, 0].astype(BF16)
    w3b[...] = w3_ref[0, 0].astype(BF16)
    w2b[...] = w2_ref[0, 0].astype(BF16)

    def block(j, carry):
        slot = lax.rem(j, 2)
        x_slot = lax.rem(j, n_x)

        @pl.when(j + ahead < n)
        def _():
            x_copy(first + j + ahead, lax.rem(j + ahead, n_x)).start(BLOCK_DMA_PRIORITY)

        x_copy(first + j, x_slot).wait()

        @pl.when(j >= 2)
        def _():
            y_copy(first + j - 2, slot).wait()

        xb = _load_token_major(xbuf.at[x_slot], bm).astype(BF16)
        h1 = jnp.dot(xb, w1b[...], preferred_element_type=F32)
        h3 = jnp.dot(xb, w3b[...], preferred_element_type=F32)
        hb = (h1 * jax.nn.sigmoid(h1) * h3).astype(BF16)
        _store_token_major(ybuf.at[slot], jnp.dot(hb, w2b[...], preferred_element_type=F32))
        y_copy(first + j, slot).start(BLOCK_DMA_PRIORITY)
        return carry

    lax.fori_loop(0, n, block, 0)

    nxt = jnp.minimum(e + 1, n_exp - 1)

    @pl.when(e + 1 < n_exp)
    def _():
        fetch_head(first_blk_ref[nxt], n_blk_ref[nxt])

    for back in (2, 1):
        @pl.when(n >= back)
        def _(back=back):
            y_copy(first + n - back, lax.rem(n - back, 2)).wait()

    @pl.when(e == n_exp - 1)
    def _():
        ybuf[0] = jnp.zeros(ybuf.shape[1:], F32)
        n_all = y_hbm.shape[0] // (bm * SUBLANES)
        for blk in range(n_all - n_exp, n_all):
            @pl.when(blk >= n_used_ref[0])
            def _(blk=blk):
                y_copy(blk, 0).start(BLOCK_DMA_PRIORITY)
                y_copy(blk, 0).wait()


def _experts(xs, first_blk, n_blk, n_used, w1, w3, w2, layer, *, bm):
    n_exp, d, d_e = w1.shape[-3:]
    assert d == SUBLANES * LANES
    w_blk = lambda e, *_: (layer, e, 0, 0)
    grid_spec = pltpu.PrefetchScalarGridSpec(
        num_scalar_prefetch=3,
        grid=(n_exp,),
        in_specs=[
            pl.BlockSpec(memory_space=pl.ANY),
            pl.BlockSpec((1, 1, d, d_e), w_blk),
            pl.BlockSpec((1, 1, d, d_e), w_blk),
            pl.BlockSpec((1, 1, d_e, d), w_blk),
        ],
        out_specs=pl.BlockSpec(memory_space=pl.ANY),
        scratch_shapes=[
            pltpu.VMEM((d, d_e), BF16), pltpu.VMEM((d, d_e), BF16), pltpu.VMEM((d_e, d), BF16),
            pltpu.VMEM((EXPERT_IN_SLOTS, bm * SUBLANES, LANES), F32), pltpu.VMEM((2, bm * SUBLANES, LANES), F32),
            pltpu.SemaphoreType.DMA((EXPERT_IN_SLOTS,)), pltpu.SemaphoreType.DMA((2,)),
        ],
    )
    return pl.pallas_call(
        functools.partial(_experts_kernel, bm=bm, n_exp=n_exp),
        grid_spec=grid_spec,
        out_shape=jax.ShapeDtypeStruct(xs.shape, F32),
        compiler_params=pltpu.CompilerParams(
            dimension_semantics=("arbitrary",), vmem_limit_bytes=VMEM_LIMIT),
        name="experts",
    )(first_blk, n_blk, n_used, xs, w1, w3, w2)


def _combine_tiles(step, n_tiles, seg_src_ref, seg_len_ref, load_x, slot_ref, gate_ref, lng_ref, lnb_ref, y_hbm,
                   store, grouped, sems, *, alpha, tm, n_exp, group):
    def fetch(tile, b, live=True):
        start = 0
        for e in range(n_exp):
            length = jnp.where(live, seg_len_ref[tile * n_exp + e], 0)
            src = seg_src_ref[tile * n_exp + e]
            _segment_copies(lambda lo, n, src=src: _token_rows(y_hbm, src + lo, n),
                            lambda lo, n, start=start: _token_rows(grouped.at[b], start + lo, n),
                            length, tm, sems.at[b], priority=BLOCK_DMA_PRIORITY)
            start = start + length

    n_buf = grouped.shape[0]
    ahead = n_buf - 1

    @pl.when(step == 0)
    def _():
        for k0 in range(ahead):
            @pl.when(k0 < n_tiles)
            def _(k0=k0):
                fetch(k0, k0)

    def tile(h, carry):
        k = step * group + h
        buf = lax.rem(k, n_buf)

        pltpu.make_async_copy(_token_rows(y_hbm, 0, 2 * tm), grouped.at[buf], sems.at[buf]).wait()

        slots = slot_ref[h]
        gates = gate_ref[h]
        row = lax.broadcasted_iota(I32, (2 * tm, tm), 0)
        g = (jnp.where(row == slots[0:1, :], gates[0:1, :], 0.0)
             + jnp.where(row == slots[1:2, :], gates[1:2, :], 0.0))
        yb = _load_token_major(grouped.at[buf], 2 * tm).astype(BF16)
        f = lax.dot_general(g.astype(BF16), yb, (((0,), (0,)), ((), ())), preferred_element_type=F32)

        nxt = k + ahead
        fetch(jnp.minimum(nxt, n_tiles - 1), lax.rem(nxt, n_buf), live=nxt < n_tiles)

        rows = pl.ds(pl.multiple_of(h * tm, tm), tm)
        store(rows, _layer_norm(alpha * load_x(rows) + f, lng_ref[...], lnb_ref[...]))
        return carry

    lax.fori_loop(0, group, tile, 0)


def _combine_kernel(seg_src_ref, seg_len_ref, x_ref, slot_ref, gate_ref, lng_ref, lnb_ref, y_hbm, o_ref,
                    grouped, sems, *, alpha, tm, n_exp, group):
    def store(rows, v):
        o_ref[rows, :] = v

    _combine_tiles(pl.program_id(0), pl.num_programs(0) * group, seg_src_ref, seg_len_ref,
                   lambda rows: x_ref[rows, :], slot_ref, gate_ref, lng_ref, lnb_ref, y_hbm, store, grouped, sems,
                   alpha=alpha, tm=tm, n_exp=n_exp, group=group)


def _combine(xf, y, slots, gates, seg_src, seg_len, ln_g, ln_b, *, alpha, tm, n_exp, group):
    t, d = xf.shape
    assert t % (tm * group) == 0
    grid_spec = pltpu.PrefetchScalarGridSpec(
        num_scalar_prefetch=2,
        grid=(t // (tm * group),),
        in_specs=[
            pl.BlockSpec((group * tm, d), lambda i, *_: (i, 0)),
            pl.BlockSpec((group, 2, tm), lambda i, *_: (i, 0, 0)),
            pl.BlockSpec((group, 2, tm), lambda i, *_: (i, 0, 0)),
            pl.BlockSpec((1, d), lambda i, *_: (0, 0)),
            pl.BlockSpec((1, d), lambda i, *_: (0, 0)),
            pl.BlockSpec(memory_space=pl.ANY),
        ],
        out_specs=pl.BlockSpec((group * tm, d), lambda i, *_: (i, 0)),
        scratch_shapes=[pltpu.VMEM((COMBINE_SLOTS, 2 * tm * SUBLANES, LANES), F32),
                        pltpu.SemaphoreType.DMA((COMBINE_SLOTS,))],
    )
    return pl.pallas_call(
        functools.partial(_combine_kernel, alpha=alpha, tm=tm, n_exp=n_exp, group=group),
        grid_spec=grid_spec,
        out_shape=jax.ShapeDtypeStruct((t, d), F32),
        compiler_params=pltpu.CompilerParams(
            dimension_semantics=("arbitrary",), vmem_limit_bytes=VMEM_LIMIT),
        name="combine",
    )(seg_src, seg_len, xf, slots, gates, _row(ln_g), _row(ln_b), y)


def _moe_rows(x, slots, tab, w1, w3, w2, layer, *, bm=EXPERT_BLOCK):
    bsz, seq, d = x.shape
    t = bsz * seq
    n_exp = tab.shape[1]
    tm = t // tab.shape[0]
    xf = x.reshape(t, d)
    before, in_tile = tab[:, :, 0], tab[:, :, 1]
    counts = before[-1] + in_tile[-1]
    padded = (counts + bm - 1) // bm * bm
    pend = jnp.cumsum(padded)
    pstart = pend - padded
    n_blk = -(-(2 * t) // bm) + n_exp
    n_used = (pend[-1] // bm).reshape(1).astype(I32)
    seg_pos = (pstart[None, :] + before).reshape(-1).astype(I32)
    seg_len = in_tile.reshape(-1).astype(I32)
    group = math.gcd(TILES_PER_STEP, t // tm)
    xs = _dispatch(xf, slots, seg_pos, seg_len, (pstart + counts).astype(I32), (padded - counts).astype(I32),
                   n_used, n_blk * bm, tm=tm, bm=bm, group=group)
    y = _experts(xs, (pstart // bm).astype(I32), (padded // bm).astype(I32), n_used, w1, w3, w2, layer, bm=bm)
    return y, seg_pos, seg_len


def _moe_ln(x, y, slots, gates, seg_pos, seg_len, ln_g, ln_b, *, alpha):
    bsz, seq, d = x.shape
    t = bsz * seq
    n_tiles, _, tm = slots.shape
    group = math.gcd(TILES_PER_STEP, n_tiles)
    out = _combine(x.reshape(t, d), y, slots, gates, seg_pos, seg_len, ln_g, ln_b,
                   alpha=alpha, tm=tm, n_exp=seg_len.shape[0] // n_tiles, group=group)
    return out.reshape(bsz, seq, d)


def kernel(x, ln1_g, ln1_b, ln2_g, ln2_b, even_w_in, even_a_dw, even_a_dw_b, even_a_ln_g, even_a_ln_b, even_b_dw, even_w_out, odd_w_in, odd_c_dw, odd_c_dw_b, odd_w_gate_a, odd_b_gate_a, odd_w_gate_x, odd_b_gate_x, odd_lam, odd_w_out, w_router, b_router, moe_w1, moe_w3, moe_w2):
    depth = ln1_g.shape[0]
    alpha = (2.0 * depth) ** 0.25
    moe = None
    for layer in range(depth):
        j = layer // 2
        if layer % 2 == 0:
            if moe is not None:
                x = _moe_ln(x, *moe, alpha=alpha)
            x, slots, gates, tab = _mix0(
                x, even_w_in[j], even_a_dw[j], even_a_dw_b[j], even_a_ln_g[j], even_a_ln_b[j],
                even_b_dw[j], even_w_out[j], ln1_g[layer], ln1_b[layer], w_router, b_router, alpha=alpha)
        else:
            x, slots, gates, tab = _mix1(
                x, odd_w_in[j], odd_c_dw[j], odd_c_dw_b[j], odd_w_gate_a[j], odd_b_gate_a[j],
                odd_w_gate_x[j], odd_b_gate_x[j], odd_lam[j], odd_w_out[j], ln1_g[layer], ln1_b[layer],
                w_router, b_router, alpha=alpha, moe=moe)
        y, seg_pos, seg_len = _moe_rows(x, slots, tab, moe_w1, moe_w3, moe_w2, layer)
        moe = (y, slots, gates, seg_pos, seg_len, ln2_g[layer], ln2_b[layer])
    return x if moe is None else _moe_ln(x, *moe, alpha=alpha)
```

```python
import functools
import math

import jax
import jax.numpy as jnp
from jax import lax
from jax.experimental import pallas as pl
from jax.experimental.pallas import tpu as pltpu

F32 = jnp.float32
BF16 = jnp.bfloat16
I32 = jnp.int32
LN_EPS = 1e-5
LRU_C = 8.0
SUBLANES = 8
LANES = 128
HALO_A = 32
HALO_B = 8
EXPERTS_PER_GROUP = 4
BLOCK_DMA_PRIORITY = 1
TILES_PER_STEP = 4
COMBINE_SLOTS = 3
DISPATCH_SLOTS = 3
EXPERT_IN_SLOTS = 4
SEQ_TILE = 512
TOKEN_TILE = 256
EXPERT_BLOCK = 256
VMEM_LIMIT = 56 * 1024 * 1024


def _layer_norm(v, g, b):
    mu = jnp.mean(v, axis=-1, keepdims=True)
    c = v - mu
    var = jnp.mean(c * c, axis=-1, keepdims=True)
    return c * lax.rsqrt(var + LN_EPS) * g + b


def _row(v):
    return v.reshape(1, -1).astype(F32)


def _tap_rows(w):
    return jnp.broadcast_to(w.astype(F32)[:, None, :], (w.shape[0], SUBLANES, w.shape[1]))


def _tap(w_ref, k, rows):
    return jnp.concatenate([w_ref[k]] * (rows // SUBLANES), axis=0)


def _const_spec(shape):
    return pl.BlockSpec(shape, lambda *_: (0,) * len(shape))


def _mix0_kernel(x_ref, win_ref, adw_ref, adwb_ref, alng_ref, alnb_ref, bdw_ref, wout_ref, lng_ref, lnb_ref,
                 wrt_ref, br_ref, o_ref, slot_ref, gate_ref, tab_ref, xb_ref, *scratch,
                 alpha, ts, tm, d_a, d_b, rc, parts):
    tri, base = scratch[-2:]
    bufs = [scratch[5 * p:5 * p + 5] for p in range(parts)]
    first = (pl.program_id(0) == 0) & (pl.program_id(1) == 0)
    n_a = adw_ref.shape[0]
    n_b = bdw_ref.shape[0]
    hs = ts // parts

    @pl.when(pl.program_id(1) == 0)
    def _():
        bufs[0][0][0:HALO_A, :] = jnp.zeros((HALO_A, d_a), F32)
        bufs[0][2][0:HALO_B, :] = jnp.zeros((HALO_B, d_b), F32)

    x = x_ref[0]
    xb = x.astype(BF16)
    cw = 2 * LANES
    dot = functools.partial(jnp.dot, preferred_element_type=F32)

    def proj_units(p):
        abuf, ash, cbuf, bgate, cat = bufs[p]
        xp = xb[p * hs:(p + 1) * hs, :]
        units = []
        for c0 in range(0, d_a, cw):
            def unit(c0=c0):
                val = dot(xp, win_ref[:, c0:c0 + cw])
                gate = dot(xp, win_ref[:, d_a + c0:d_a + c0 + cw])
                abuf[HALO_A:HALO_A + hs, c0:c0 + cw] = val * jax.nn.sigmoid(gate)
            units.append(unit)
        for c0 in range(0, d_b, cw):
            def unit(c0=c0, o=2 * d_a):
                bgate[:, c0:c0 + cw] = dot(xp, win_ref[:, o + c0:o + c0 + cw])
                cbuf[HALO_B:HALO_B + hs, c0:c0 + cw] = (dot(xp, win_ref[:, o + d_b + c0:o + d_b + c0 + cw])
                                                        * dot(xp, win_ref[:, o + 2 * d_b + c0:o + 2 * d_b + c0 + cw]))
            units.append(unit)
        return units

    def conv_units(p):
        abuf, ash, cbuf, bgate, cat = bufs[p]

        def shifted():
            n_sh = hs + HALO_A - SUBLANES
            for j in range(1, SUBLANES):
                ash[j - 1, :, :] = abuf[j:j + n_sh, :]
        units = [shifted]
        for c in range(hs // rc):
            def unit(r0=c * rc):
                acc = adwb_ref[...]
                for k in range(n_a):
                    q, j = divmod(HALO_A - (n_a - 1) + k, SUBLANES)
                    lo = r0 + q * SUBLANES
                    src = abuf[lo:lo + rc, :] if j == 0 else ash[j - 1, lo:lo + rc, :]
                    acc = acc + _tap(adw_ref, k, rc) * src
                a = _layer_norm(acc, alng_ref[...], alnb_ref[...])
                cat[r0:r0 + rc, 0:d_a] = (a * jax.nn.sigmoid(a)).astype(BF16)
                bb = jnp.zeros((rc, d_b), F32)
                for k in range(n_b):
                    lo = r0 + HALO_B - (n_b - 1) + k
                    bb = bb + _tap(bdw_ref, k, rc) * cbuf[lo:lo + rc, :]
                cat[r0:r0 + rc, d_a:d_a + d_b] = (bgate[r0:r0 + rc, :] * bb).astype(BF16)
            units.append(unit)
        return units

    m_cols = [[] for _ in range(parts)]

    def out_units(p):
        cat = bufs[p][4]
        units = []
        for c0 in range(0, wout_ref.shape[1], cw):
            def unit(c0=c0):
                m_cols[p].append(dot(cat[...], wout_ref[:, c0:c0 + cw]))
            units.append(unit)
        return units

    def run_interleaved(*groups):
        order = sorted(((i + 0.5) / len(g), gi, i) for gi, g in enumerate(groups) for i in range(len(g)))
        for _, gi, i in order:
            groups[gi][i]()

    for s in range(parts + 2):
        groups = []
        if s < parts:
            groups.append(proj_units(s))
        if 0 <= s - 1 < parts:
            groups.append(conv_units(s - 1))
        if 0 <= s - 2 < parts:
            groups.append(out_units(s - 2))
        run_interleaved(*groups)
        if s + 1 < parts:
            bufs[s + 1][0][0:HALO_A, :] = bufs[s][0][hs:hs + HALO_A, :]
            bufs[s + 1][2][0:HALO_B, :] = bufs[s][2][hs:hs + HALO_B, :]

    bufs[0][0][0:HALO_A, :] = bufs[-1][0][hs:hs + HALO_A, :]
    bufs[0][2][0:HALO_B, :] = bufs[-1][2][hs:hs + HALO_B, :]

    m = jnp.concatenate([jnp.concatenate(cols, axis=1) for cols in m_cols], axis=0)
    _ln_and_route(alpha * x + m, lng_ref, lnb_ref, wrt_ref, br_ref, o_ref, slot_ref, gate_ref, tab_ref, xb_ref,
                  tri, base, first, tm=tm)


def _mix0(x, w_in, a_dw, a_dw_b, a_ln_g, a_ln_b, b_dw, w_out, ln_g, ln_b, w_router, b_router,
          *, alpha, ts=SEQ_TILE, tm=TOKEN_TILE, rc=32, parts=2):
    bsz, seq, d = x.shape
    d_a = a_dw.shape[-1]
    d_b = b_dw.shape[-1]
    assert a_dw.shape[0] - 1 <= HALO_A and b_dw.shape[0] - 1 <= HALO_B
    ts = min(ts, seq)
    hs = ts // parts
    rc = min(rc, hs)
    assert seq % ts == 0 and ts % parts == 0 and hs % rc == 0 and rc % (2 * SUBLANES) == 0
    r_in, r_args, r_out_specs, r_out_shape, r_scratch, tm = _router_io(w_router, b_router, bsz, seq, ts, tm)
    kern = functools.partial(_mix0_kernel, alpha=alpha, ts=ts, tm=tm, d_a=d_a, d_b=d_b, rc=rc, parts=parts)
    part_scratch = [
        pltpu.VMEM((HALO_A + hs, d_a), F32),
        pltpu.VMEM((SUBLANES - 1, HALO_A + hs - SUBLANES, d_a), F32),
        pltpu.VMEM((HALO_B + hs, d_b), F32),
        pltpu.VMEM((hs, d_b), F32),
        pltpu.VMEM((hs, d_a + d_b), BF16),
    ]
    return pl.pallas_call(
        kern,
        grid=(bsz, seq // ts),
        in_specs=[
            pl.BlockSpec((1, ts, d), lambda b, s: (b, s, 0)),
            _const_spec(w_in.shape), _const_spec((a_dw.shape[0], SUBLANES, d_a)), _const_spec((1, d_a)),
            _const_spec((1, d_a)), _const_spec((1, d_a)), _const_spec((b_dw.shape[0], SUBLANES, d_b)),
            _const_spec(w_out.shape),
            _const_spec((1, d)), _const_spec((1, d)),
        ] + r_in,
        out_specs=[pl.BlockSpec((1, ts, d), lambda b, s: (b, s, 0))] + r_out_specs,
        out_shape=[jax.ShapeDtypeStruct(x.shape, F32)] + r_out_shape,
        scratch_shapes=part_scratch * parts + r_scratch,
        compiler_params=pltpu.CompilerParams(
            dimension_semantics=("arbitrary", "arbitrary"), vmem_limit_bytes=VMEM_LIMIT),
        name="mix0",
    )(x, w_in.astype(BF16), _tap_rows(a_dw), _row(a_dw_b), _row(a_ln_g), _row(a_ln_b),
      _tap_rows(b_dw), w_out.astype(BF16), _row(ln_g), _row(ln_b), *r_args)


def _mix1_kernel(x_ref, win_ref, cdw_ref, cdwb_ref, wga_ref, bga_ref, wgx_ref, bgx_ref, lam_ref, wout_ref,
                 lng_ref, lnb_ref, wrt_ref, br_ref, o_ref, slot_ref, gate_ref, tab_ref, xb_ref,
                 cbuf, ybr, abuf, bbuf, hcar, tri, base, *, alpha, ts, tm, d_rnn, rc):
    first = (pl.program_id(0) == 0) & (pl.program_id(1) == 0)

    @pl.when(pl.program_id(1) == 0)
    def _():
        cbuf[0:HALO_B, :] = jnp.zeros((HALO_B, d_rnn), F32)
        hcar[...] = jnp.zeros((1, d_rnn), F32)

    x = x_ref[0] if len(x_ref.shape) == 3 else x_ref[...]
    xb = x.astype(BF16)
    ybr[...] = jnp.dot(xb, win_ref[:, 0:d_rnn], preferred_element_type=F32)
    cbuf[HALO_B:HALO_B + ts, :] = jnp.dot(xb, win_ref[:, d_rnn:2 * d_rnn], preferred_element_type=F32)

    n_heads, blk = wga_ref.shape[0], wga_ref.shape[1]
    n_c = cdw_ref.shape[0]
    neg_c_sp = -LRU_C * jax.nn.softplus(-lam_ref[...])
    for c in range(ts // rc):
        r0 = c * rc
        xr = cdwb_ref[...]
        for k in range(n_c):
            lo = r0 + HALO_B - (n_c - 1) + k
            xr = xr + _tap(cdw_ref, k, rc) * cbuf[lo:lo + rc, :]
        xrb = xr.astype(BF16)
        for h in range(n_heads):
            cs = slice(h * blk, (h + 1) * blk)
            r = jax.nn.sigmoid(jnp.dot(xrb[:, cs], wga_ref[h], preferred_element_type=F32) + bga_ref[:, cs])
            i = jax.nn.sigmoid(jnp.dot(xrb[:, cs], wgx_ref[h], preferred_element_type=F32) + bgx_ref[:, cs])
            log_a = neg_c_sp[:, cs] * r
            abuf[r0:r0 + rc, cs] = jnp.exp(log_a)
            th = jnp.tanh(log_a)
            bbuf[r0:r0 + rc, cs] = jnp.sqrt(-2.0 * th / (1.0 - th)) * (i * xr[:, cs])

    cbuf[0:HALO_B, :] = cbuf[ts:ts + HALO_B, :]

    rowid = lax.broadcasted_iota(I32, (SUBLANES, d_rnn), 0)

    def group(g, h_prev):
        r = pl.multiple_of(g * SUBLANES, SUBLANES)
        a = abuf[pl.ds(r, SUBLANES), :]
        b = bbuf[pl.ds(r, SUBLANES), :]
        step = 1
        while step < SUBLANES:
            keep = rowid >= step
            b = jnp.where(keep, a * pltpu.roll(b, step, axis=0) + b, b)
            a = jnp.where(keep, a * pltpu.roll(a, step, axis=0), a)
            step *= 2
        h = a * h_prev + b
        bbuf[pl.ds(r, SUBLANES), :] = h
        return h[SUBLANES - 1:SUBLANES, :]

    hcar[...] = lax.fori_loop(0, ts // SUBLANES, group, hcar[...], unroll=4)

    out = (jax.nn.gelu(ybr[...]) * bbuf[...]).astype(BF16)
    m = jnp.dot(out, wout_ref[...], preferred_element_type=F32)
    _ln_and_route(alpha * x + m, lng_ref, lnb_ref, wrt_ref, br_ref, o_ref, slot_ref, gate_ref, tab_ref, xb_ref,
                  tri, base, first, tm=tm)


def _mix1_after_moe_kernel(seg_src_ref, seg_len_ref, x_ref, cslot_ref, cgate_ref, clng_ref, clnb_ref, y_hbm, *rest,
                           alpha, ts, tm, d_rnn, rc, n_exp):
    *mix, xbuf, grouped, csems = rest
    group = ts // tm
    step = pl.program_id(0) * pl.num_programs(1) + pl.program_id(1)

    def store(rows, v):
        xbuf[rows, :] = v

    _combine_tiles(step, pl.num_programs(0) * pl.num_programs(1) * group, seg_src_ref, seg_len_ref,
                   lambda rows: x_ref[0, rows, :], cslot_ref, cgate_ref, clng_ref, clnb_ref, y_hbm, store,
                   grouped, csems, alpha=alpha, tm=tm, n_exp=n_exp, group=group)
    _mix1_kernel(xbuf, *mix, alpha=alpha, ts=ts, tm=tm, d_rnn=d_rnn, rc=rc)


def _mix1(x, w_in, c_dw, c_dw_b, w_gate_a, b_gate_a, w_gate_x, b_gate_x, lam, w_out, ln_g, ln_b,
          w_router, b_router, *, alpha, ts=SEQ_TILE, tm=TOKEN_TILE, rc=128, moe=None):
    bsz, seq, d = x.shape
    d_rnn = c_dw.shape[-1]
    assert c_dw.shape[0] - 1 <= HALO_B
    ts = min(ts, seq)
    rc = min(rc, ts)
    assert seq % ts == 0 and ts % rc == 0 and ts % SUBLANES == 0
    r_in, r_args, r_out_specs, r_out_shape, r_scratch, tm = _router_io(w_router, b_router, bsz, seq, ts, tm)
    kern = functools.partial(_mix1_kernel, alpha=alpha, ts=ts, tm=tm, d_rnn=d_rnn, rc=rc)
    in_specs = [
        pl.BlockSpec((1, ts, d), lambda b, s, *_: (b, s, 0)),
        _const_spec(w_in.shape), _const_spec((c_dw.shape[0], SUBLANES, d_rnn)), _const_spec((1, d_rnn)),
        _const_spec(w_gate_a.shape), _const_spec((1, d_rnn)),
        _const_spec(w_gate_x.shape), _const_spec((1, d_rnn)),
        _const_spec((1, d_rnn)), _const_spec(w_out.shape), _const_spec((1, d)), _const_spec((1, d)),
    ] + r_in
    scratch = [
        pltpu.VMEM((HALO_B + ts, d_rnn), F32),
        pltpu.VMEM((ts, d_rnn), F32),
        pltpu.VMEM((ts, d_rnn), F32),
        pltpu.VMEM((ts, d_rnn), F32),
        pltpu.VMEM((1, d_rnn), F32),
    ] + r_scratch
    args = (x, w_in.astype(BF16), _tap_rows(c_dw), _row(c_dw_b), w_gate_a.astype(BF16), _row(b_gate_a),
            w_gate_x.astype(BF16), _row(b_gate_x), _row(lam), w_out.astype(BF16), _row(ln_g), _row(ln_b), *r_args)
    prefetch = ()
    if moe is not None:
        y, m_slots, m_gates, seg_src, seg_len, m_ln_g, m_ln_b = moe
        n_exp = w_router.shape[1]
        assert m_slots.shape == (bsz * seq // tm, 2, tm) and d == SUBLANES * LANES
        kern = functools.partial(_mix1_after_moe_kernel, alpha=alpha, ts=ts, tm=tm, d_rnn=d_rnn, rc=rc, n_exp=n_exp)
        per_tile = lambda b, s, *_: (b * (seq // ts) + s, 0, 0)
        in_specs = [in_specs[0], pl.BlockSpec((ts // tm, 2, tm), per_tile), pl.BlockSpec((ts // tm, 2, tm), per_tile),
                    _const_spec((1, d)), _const_spec((1, d)), pl.BlockSpec(memory_space=pl.ANY)] + in_specs[1:]
        scratch = scratch + [pltpu.VMEM((ts, d), F32), pltpu.VMEM((COMBINE_SLOTS, 2 * tm * SUBLANES, LANES), F32),
                             pltpu.SemaphoreType.DMA((COMBINE_SLOTS,))]
        args = (x, m_slots, m_gates, _row(m_ln_g), _row(m_ln_b), y) + args[1:]
        prefetch = (seg_src, seg_len)
    grid_spec = pltpu.PrefetchScalarGridSpec(
        num_scalar_prefetch=len(prefetch),
        grid=(bsz, seq // ts),
        in_specs=in_specs,
        out_specs=[pl.BlockSpec((1, ts, d), lambda b, s, *_: (b, s, 0))] + r_out_specs,
        scratch_shapes=scratch,
    )
    return pl.pallas_call(
        kern,
        grid_spec=grid_spec,
        out_shape=[jax.ShapeDtypeStruct(x.shape, F32)] + r_out_shape,
        compiler_params=pltpu.CompilerParams(
            dimension_semantics=("arbitrary", "arbitrary"), vmem_limit_bytes=VMEM_LIMIT),
        name="mix1",
    )(*prefetch, *args)


def _first_argmax(vals):
    best, idx = vals[0], jnp.zeros(vals[0].shape, I32)
    for j in range(1, len(vals)):
        upd = vals[j] > best
        best = jnp.where(upd, vals[j], best)
        idx = jnp.where(upd, j, idx)
    return best, idx


def _route_tile(x, wrt_ref, br_ref, tri, base, tm):
    n_exp = wrt_ref.shape[0]
    tt = x.shape[0]
    w = wrt_ref[...]
    w_hi = w.astype(BF16)
    w_lo = (w - w_hi.astype(F32)).astype(BF16)
    x_hi = x.astype(BF16)
    x_lo = (x - x_hi.astype(F32)).astype(BF16)
    over_d = (((1,), (1,)), ((), ()))
    logits = (lax.dot_general(w_hi, x_hi, over_d, preferred_element_type=F32)
              + lax.dot_general(w_hi, x_lo, over_d, preferred_element_type=F32)
              + lax.dot_general(w_lo, x_hi, over_d, preferred_element_type=F32))
    logits = logits + br_ref[:, 0:1]
    e = jnp.exp(logits - jnp.max(logits, axis=0, keepdims=True))
    p = e / jnp.sum(e, axis=0, keepdims=True)
    rows = [p[j:j + 1, :] for j in range(n_exp)]

    n_grp = n_exp // EXPERTS_PER_GROUP
    scores = []
    for g in range(n_grp):
        a, b, c, d = rows[EXPERTS_PER_GROUP * g:EXPERTS_PER_GROUP * (g + 1)]
        scores.append(jnp.maximum(jnp.maximum(a, b) + jnp.maximum(c, d), jnp.maximum(a + b, c + d)))
    _, g_sel = _first_argmax(scores)

    p_in = []
    for j in range(EXPERTS_PER_GROUP):
        v = rows[j]
        for g in range(1, n_grp):
            v = jnp.where(g_sel == g, rows[EXPERTS_PER_GROUP * g + j], v)
        p_in.append(v)
    v1, i1 = _first_argmax(p_in)
    v2, i2 = _first_argmax([jnp.where(i1 == j, -1.0, p_in[j]) for j in range(EXPERTS_PER_GROUP)])
    den = v1 + v2
    e1 = g_sel * EXPERTS_PER_GROUP + i1
    e2 = g_sel * EXPERTS_PER_GROUP + i2

    eid = lax.broadcasted_iota(I32, (n_exp, tt), 0)
    hit1 = eid == e1
    hit2 = eid == e2
    member = jnp.where(hit1 | hit2, 1.0, 0.0)
    member_b = member.astype(BF16)
    lower = (lax.broadcasted_iota(I32, (n_exp, n_exp), 1) < lax.broadcasted_iota(I32, (n_exp, n_exp), 0))
    lower = jnp.where(lower, 1.0, 0.0).astype(BF16)
    lane = lax.broadcasted_iota(I32, (n_exp, LANES), 1)
    place, tabs = [], []
    for h in range(tt // tm):
        cols = slice(h * tm, (h + 1) * tm)
        incl = jnp.dot(member_b[:, cols], tri[...], preferred_element_type=F32)
        count = jnp.sum(member[:, cols], axis=1, keepdims=True)
        offset = jnp.dot(lower, jnp.broadcast_to(count, (n_exp, LANES)).astype(BF16),
                         preferred_element_type=F32)[:, 0:1]
        place.append(offset + incl - member[:, cols])
        tabs.append(jnp.where(lane == 0, base, jnp.where(lane == 1, count, 0.0)).astype(I32))
        base = base + count
    place = jnp.concatenate(place, axis=1)
    s1 = jnp.sum(jnp.where(hit1, place, 0.0), axis=0, keepdims=True)
    s2 = jnp.sum(jnp.where(hit2, place, 0.0), axis=0, keepdims=True)
    return (s1.astype(I32), s2.astype(I32)), (v1 / den, v2 / den), tabs, base


def _ln_and_route(v, lng_ref, lnb_ref, wrt_ref, br_ref, o_ref, slot_ref, gate_ref, tab_ref, xb_ref, tri, base, first,
                  *, tm):
    @pl.when(first)
    def _():
        src = lax.broadcasted_iota(I32, (tm, tm), 0)
        dst = lax.broadcasted_iota(I32, (tm, tm), 1)
        tri[...] = jnp.where(src <= dst, 1.0, 0.0).astype(BF16)
        base[...] = jnp.zeros(base.shape, F32)

    x1 = _layer_norm(v, lng_ref[...], lnb_ref[...])
    o_ref[0] = x1
    xb_ref[0] = x1.astype(BF16)
    slots, gates, tabs, new_base = _route_tile(x1, wrt_ref, br_ref, tri, base[...], tm)
    for h, tab in enumerate(tabs):
        for k in range(2):
            slot_ref[h, k:k + 1, :] = slots[k][:, h * tm:(h + 1) * tm]
            gate_ref[h, k:k + 1, :] = gates[k][:, h * tm:(h + 1) * tm]
        tab_ref[h] = tab
    base[...] = new_base


def _router_io(w_router, b_router, bsz, seq, ts, tm):
    d, n_exp = w_router.shape
    tm = min(tm, ts)
    assert ts % tm == 0 and n_exp % EXPERTS_PER_GROUP == 0 and tm <= 256
    t = bsz * seq
    per_seq = seq // ts
    in_specs = [_const_spec((n_exp, d)), _const_spec((n_exp, LANES))]
    args = (w_router.T.astype(F32), jnp.broadcast_to(b_router.astype(F32)[:, None], (n_exp, LANES)))
    per_tile = lambda b, s, *_: (b * per_seq + s, 0, 0)
    out_specs = [
        pl.BlockSpec((ts // tm, 2, tm), per_tile),
        pl.BlockSpec((ts // tm, 2, tm), per_tile),
        pl.BlockSpec((ts // tm, n_exp, LANES), per_tile),
        pl.BlockSpec((1, ts, d), lambda b, s, *_: (b, s, 0)),
    ]
    out_shape = [
        jax.ShapeDtypeStruct((t // tm, 2, tm), I32),
        jax.ShapeDtypeStruct((t // tm, 2, tm), F32),
        jax.ShapeDtypeStruct((t // tm, n_exp, LANES), I32),
        jax.ShapeDtypeStruct((bsz, seq, d), BF16),
    ]
    scratch = [pltpu.VMEM((tm, tm), BF16), pltpu.VMEM((n_exp, LANES), F32)]
    return in_specs, args, out_specs, out_shape, scratch, tm


def _segment_copies(src_of, dst_of, length, max_len, sem, *, wait=False, priority=0):
    done = 0
    for bit in reversed(range(max_len.bit_length())):
        n = 1 << bit
        piece = length & n

        @pl.when(piece != 0)
        def _(done=done, n=n):
            copy = pltpu.make_async_copy(src_of(done, n), dst_of(done, n), sem)
            copy.wait() if wait else copy.start(priority)

        done = done + piece


def _token_rows(ref, lo, n):
    return ref.at[pl.ds(pl.multiple_of(lo * SUBLANES, SUBLANES), n * SUBLANES)]


def _store_token_major(ref, val):
    for j in range(SUBLANES):
        ref[pl.ds(j, val.shape[0], stride=SUBLANES), :] = val[:, j * LANES:(j + 1) * LANES]


def _load_token_major(ref, rows):
    return jnp.concatenate([ref[pl.ds(j, rows, stride=SUBLANES), :] for j in range(SUBLANES)], axis=1)


def _dispatch_kernel(seg_dst_ref, seg_len_ref, pad_dst_ref, pad_len_ref, n_used_ref, x_ref, slot_ref, xs_hbm,
                     grouped, zeros, sems, pad_sem, *, tm, n_exp, bm, group):
    i = pl.program_id(0)
    n_tiles = pl.num_programs(0) * group

    def wait_tile(b):
        pltpu.make_async_copy(grouped.at[b], _token_rows(xs_hbm, 0, 2 * tm), sems.at[b]).wait()

    def pad_copies(wait):
        for e in range(n_exp):
            _segment_copies(lambda lo, n: _token_rows(zeros, lo, n),
                            lambda lo, n, e=e: _token_rows(xs_hbm, pad_dst_ref[e] + lo, n),
                            pad_len_ref[e], bm - 1, pad_sem, wait=wait)
        n_blk = xs_hbm.shape[0] // (bm * SUBLANES)
        for blk in range(n_blk - n_exp, n_blk):
            @pl.when(blk >= n_used_ref[0])
            def _(blk=blk):
                copy = pltpu.make_async_copy(zeros, _token_rows(xs_hbm, blk * bm, bm), pad_sem)
                copy.wait() if wait else copy.start()

    @pl.when(i == 0)
    def _():
        zeros[...] = jnp.zeros(zeros.shape, F32)
        pad_copies(wait=False)

    n_buf = grouped.shape[0]

    def send(tile, live=True):
        b = lax.rem(tile, n_buf)
        start = 0
        for e in range(n_exp):
            length = jnp.where(live, seg_len_ref[tile * n_exp + e], 0)
            dst = seg_dst_ref[tile * n_exp + e]
            _segment_copies(lambda lo, n, start=start: _token_rows(grouped.at[b], start + lo, n),
                            lambda lo, n, dst=dst: _token_rows(xs_hbm, dst + lo, n),
                            length, tm, sems.at[b])
            start = start + length

    def tile(h, carry):
        k = i * group + h
        buf = lax.rem(k, n_buf)

        @pl.when(k >= n_buf)
        def _():
            wait_tile(buf)

        slots = slot_ref[h]
        row = lax.broadcasted_iota(I32, (2 * tm, tm), 0)
        onehot = jnp.where((row == slots[0:1, :]) | (row == slots[1:2, :]), 1.0, 0.0).astype(BF16)
        xt = x_ref[pl.ds(pl.multiple_of(h * tm, tm), tm), :].astype(BF16)
        rows = jnp.dot(onehot, xt, preferred_element_type=F32)
        send(jnp.maximum(k - 1, 0), live=k >= 1)
        _store_token_major(grouped.at[buf], rows)
        return carry

    lax.fori_loop(0, group, tile, 0)

    @pl.when(i == pl.num_programs(0) - 1)
    def _():
        send(n_tiles - 1)
        for back in range(n_buf):
            @pl.when(n_tiles > back)
            def _(back=back):
                wait_tile(lax.rem(n_tiles - 1 - back, n_buf))

        pad_copies(wait=True)


def _dispatch(xf, slots, seg_dst, seg_len, pad_dst, pad_len, n_used, n_pad, *, tm, bm, group):
    t, d = xf.shape
    assert d == SUBLANES * LANES and t % (tm * group) == 0
    n_exp = pad_dst.shape[0]
    grid_spec = pltpu.PrefetchScalarGridSpec(
        num_scalar_prefetch=5,
        grid=(t // (tm * group),),
        in_specs=[
            pl.BlockSpec((group * tm, d), lambda i, *_: (i, 0)),
            pl.BlockSpec((group, 2, tm), lambda i, *_: (i, 0, 0)),
        ],
        out_specs=pl.BlockSpec(memory_space=pl.ANY),
        scratch_shapes=[
            pltpu.VMEM((DISPATCH_SLOTS, 2 * tm * SUBLANES, LANES), F32),
            pltpu.VMEM((bm * SUBLANES, LANES), F32),
            pltpu.SemaphoreType.DMA((DISPATCH_SLOTS,)),
            pltpu.SemaphoreType.DMA(()),
        ],
    )
    return pl.pallas_call(
        functools.partial(_dispatch_kernel, tm=tm, n_exp=n_exp, bm=bm, group=group),
        grid_spec=grid_spec,
        out_shape=jax.ShapeDtypeStruct((n_pad * SUBLANES, LANES), F32),
        compiler_params=pltpu.CompilerParams(
            dimension_semantics=("arbitrary",), vmem_limit_bytes=VMEM_LIMIT),
        name="dispatch",
    )(seg_dst, seg_len, pad_dst, pad_len, n_used, xf, slots)


def _experts_kernel(first_blk_ref, n_blk_ref, n_used_ref, xs_hbm, w1_ref, w3_ref, w2_ref, y_hbm,
                    w1b, w3b, w2b, xbuf, ybuf, in_sems, out_sems, *, bm, n_exp):
    e = pl.program_id(0)
    first = first_blk_ref[e]
    n = n_blk_ref[e]

    n_x = xbuf.shape[0]
    ahead = n_x - 1

    def x_copy(blk, slot):
        return pltpu.make_async_copy(_token_rows(xs_hbm, blk * bm, bm), xbuf.at[slot], in_sems.at[slot])

    def y_copy(blk, slot):
        return pltpu.make_async_copy(ybuf.at[slot], _token_rows(y_hbm, blk * bm, bm), out_sems.at[slot])

    def fetch_head(first_blk, count):
        for k in range(ahead):
            @pl.when(k < count)
            def _(k=k):
                x_copy(first_blk + k, k).start(BLOCK_DMA_PRIORITY)

    @pl.when(e == 0)
    def _():
        fetch_head(first, n)

    w1b[...] = w1_ref[0, 0].astype(BF16)
    w3b[...] = w3_ref[0, 0].astype(BF16)
    w2b[...] = w2_ref[0, 0].astype(BF16)

    def block(j, carry):
        slot = lax.rem(j, 2)
        x_slot = lax.rem(j, n_x)

        @pl.when(j + ahead < n)
        def _():
            x_copy(first + j + ahead, lax.rem(j + ahead, n_x)).start(BLOCK_DMA_PRIORITY)

        x_copy(first + j, x_slot).wait()

        @pl.when(j >= 2)
        def _():
            y_copy(first + j - 2, slot).wait()

        xb = _load_token_major(xbuf.at[x_slot], bm).astype(BF16)
        h1 = jnp.dot(xb, w1b[...], preferred_element_type=F32)
        h3 = jnp.dot(xb, w3b[...], preferred_element_type=F32)
        hb = (h1 * jax.nn.sigmoid(h1) * h3).astype(BF16)
        _store_token_major(ybuf.at[slot], jnp.dot(hb, w2b[...], preferred_element_type=F32))
        y_copy(first + j, slot).start(BLOCK_DMA_PRIORITY)
        return carry

    lax.fori_loop(0, n, block, 0)

    nxt = jnp.minimum(e + 1, n_exp - 1)

    @pl.when(e + 1 < n_exp)
    def _():
        fetch_head(first_blk_ref[nxt], n_blk_ref[nxt])

    for back in (2, 1):
        @pl.when(n >= back)
        def _(back=back):
            y_copy(first + n - back, lax.rem(n - back, 2)).wait()

    @pl.when(e == n_exp - 1)
    def _():
        ybuf[0] = jnp.zeros(ybuf.shape[1:], F32)
        n_all = y_hbm.shape[0] // (bm * SUBLANES)
        for blk in range(n_all - n_exp, n_all):
            @pl.when(blk >= n_used_ref[0])
            def _(blk=blk):
                y_copy(blk, 0).start(BLOCK_DMA_PRIORITY)
                y_copy(blk, 0).wait()


def _experts(xs, first_blk, n_blk, n_used, w1, w3, w2, layer, *, bm):
    n_exp, d, d_e = w1.shape[-3:]
    assert d == SUBLANES * LANES
    w_blk = lambda e, *_: (layer, e, 0, 0)
    grid_spec = pltpu.PrefetchScalarGridSpec(
        num_scalar_prefetch=3,
        grid=(n_exp,),
        in_specs=[
            pl.BlockSpec(memory_space=pl.ANY),
            pl.BlockSpec((1, 1, d, d_e), w_blk),
            pl.BlockSpec((1, 1, d, d_e), w_blk),
            pl.BlockSpec((1, 1, d_e, d), w_blk),
        ],
        out_specs=pl.BlockSpec(memory_space=pl.ANY),
        scratch_shapes=[
            pltpu.VMEM((d, d_e), BF16), pltpu.VMEM((d, d_e), BF16), pltpu.VMEM((d_e, d), BF16),
            pltpu.VMEM((EXPERT_IN_SLOTS, bm * SUBLANES, LANES), F32), pltpu.VMEM((2, bm * SUBLANES, LANES), F32),
            pltpu.SemaphoreType.DMA((EXPERT_IN_SLOTS,)), pltpu.SemaphoreType.DMA((2,)),
        ],
    )
    return pl.pallas_call(
        functools.partial(_experts_kernel, bm=bm, n_exp=n_exp),
        grid_spec=grid_spec,
        out_shape=jax.ShapeDtypeStruct(xs.shape, F32),
        compiler_params=pltpu.CompilerParams(
            dimension_semantics=("arbitrary",), vmem_limit_bytes=VMEM_LIMIT),
        name="experts",
    )(first_blk, n_blk, n_used, xs, w1, w3, w2)


def _combine_tiles(step, n_tiles, seg_src_ref, seg_len_ref, load_x, slot_ref, gate_ref, lng_ref, lnb_ref, y_hbm,
                   store, grouped, sems, *, alpha, tm, n_exp, group):
    def fetch(tile, b, live=True):
        start = 0
        for e in range(n_exp):
            length = jnp.where(live, seg_len_ref[tile * n_exp + e], 0)
            src = seg_src_ref[tile * n_exp + e]
            _segment_copies(lambda lo, n, src=src: _token_rows(y_hbm, src + lo, n),
                            lambda lo, n, start=start: _token_rows(grouped.at[b], start + lo, n),
                            length, tm, sems.at[b], priority=BLOCK_DMA_PRIORITY)
            start = start + length

    n_buf = grouped.shape[0]
    ahead = n_buf - 1

    @pl.when(step == 0)
    def _():
        for k0 in range(ahead):
            @pl.when(k0 < n_tiles)
            def _(k0=k0):
                fetch(k0, k0)

    def tile(h, carry):
        k = step * group + h
        buf = lax.rem(k, n_buf)

        pltpu.make_async_copy(_token_rows(y_hbm, 0, 2 * tm), grouped.at[buf], sems.at[buf]).wait()

        slots = slot_ref[h]
        gates = gate_ref[h]
        row = lax.broadcasted_iota(I32, (2 * tm, tm), 0)
        g = (jnp.where(row == slots[0:1, :], gates[0:1, :], 0.0)
             + jnp.where(row == slots[1:2, :], gates[1:2, :], 0.0))
        yb = _load_token_major(grouped.at[buf], 2 * tm).astype(BF16)
        f = lax.dot_general(g.astype(BF16), yb, (((0,), (0,)), ((), ())), preferred_element_type=F32)

        nxt = k + ahead
        fetch(jnp.minimum(nxt, n_tiles - 1), lax.rem(nxt, n_buf), live=nxt < n_tiles)

        rows = pl.ds(pl.multiple_of(h * tm, tm), tm)
        store(rows, _layer_norm(alpha * load_x(rows) + f, lng_ref[...], lnb_ref[...]))
        return carry

    lax.fori_loop(0, group, tile, 0)


def _combine_kernel(seg_src_ref, seg_len_ref, x_ref, slot_ref, gate_ref, lng_ref, lnb_ref, y_hbm, o_ref,
                    grouped, sems, *, alpha, tm, n_exp, group):
    def store(rows, v):
        o_ref[rows, :] = v

    _combine_tiles(pl.program_id(0), pl.num_programs(0) * group, seg_src_ref, seg_len_ref,
                   lambda rows: x_ref[rows, :], slot_ref, gate_ref, lng_ref, lnb_ref, y_hbm, store, grouped, sems,
                   alpha=alpha, tm=tm, n_exp=n_exp, group=group)


def _combine(xf, y, slots, gates, seg_src, seg_len, ln_g, ln_b, *, alpha, tm, n_exp, group):
    t, d = xf.shape
    assert t % (tm * group) == 0
    grid_spec = pltpu.PrefetchScalarGridSpec(
        num_scalar_prefetch=2,
        grid=(t // (tm * group),),
        in_specs=[
            pl.BlockSpec((group * tm, d), lambda i, *_: (i, 0)),
            pl.BlockSpec((group, 2, tm), lambda i, *_: (i, 0, 0)),
            pl.BlockSpec((group, 2, tm), lambda i, *_: (i, 0, 0)),
            pl.BlockSpec((1, d), lambda i, *_: (0, 0)),
            pl.BlockSpec((1, d), lambda i, *_: (0, 0)),
            pl.BlockSpec(memory_space=pl.ANY),
        ],
        out_specs=pl.BlockSpec((group * tm, d), lambda i, *_: (i, 0)),
        scratch_shapes=[pltpu.VMEM((COMBINE_SLOTS, 2 * tm * SUBLANES, LANES), F32),
                        pltpu.SemaphoreType.DMA((COMBINE_SLOTS,))],
    )
    return pl.pallas_call(
        functools.partial(_combine_kernel, alpha=alpha, tm=tm, n_exp=n_exp, group=group),
        grid_spec=grid_spec,
        out_shape=jax.ShapeDtypeStruct((t, d), F32),
        compiler_params=pltpu.CompilerParams(
            dimension_semantics=("arbitrary",), vmem_limit_bytes=VMEM_LIMIT),
        name="combine",
    )(seg_src, seg_len, xf, slots, gates, _row(ln_g), _row(ln_b), y)


def _moe_rows(x, slots, tab, w1, w3, w2, layer, *, bm=EXPERT_BLOCK):
    bsz, seq, d = x.shape
    t = bsz * seq
    n_exp = tab.shape[1]
    tm = t // tab.shape[0]
    xf = x.reshape(t, d)
    before, in_tile = tab[:, :, 0], tab[:, :, 1]
    counts = before[-1] + in_tile[-1]
    padded = (counts + bm - 1) // bm * bm
    pend = jnp.cumsum(padded)
    pstart = pend - padded
    n_blk = -(-(2 * t) // bm) + n_exp
    n_used = (pend[-1] // bm).reshape(1).astype(I32)
    seg_pos = (pstart[None, :] + before).reshape(-1).astype(I32)
    seg_len = in_tile.reshape(-1).astype(I32)
    group = math.gcd(TILES_PER_STEP, t // tm)
    xs = _dispatch(xf, slots, seg_pos, seg_len, (pstart + counts).astype(I32), (padded - counts).astype(I32),
                   n_used, n_blk * bm, tm=tm, bm=bm, group=group)
    y = _experts(xs, (pstart // bm).astype(I32), (padded // bm).astype(I32), n_used, w1, w3, w2, layer, bm=bm)
    return y, seg_pos, seg_len


def _moe_ln(x, y, slots, gates, seg_pos, seg_len, ln_g, ln_b, *, alpha):
    bsz, seq, d = x.shape
    t = bsz * seq
    n_tiles, _, tm = slots.shape
    group = math.gcd(TILES_PER_STEP, n_tiles)
    out = _combine(x.reshape(t, d), y, slots, gates, seg_pos, seg_len, ln_g, ln_b,
                   alpha=alpha, tm=tm, n_exp=seg_len.shape[0] // n_tiles, group=group)
    return out.reshape(bsz, seq, d)


def kernel(x, ln1_g, ln1_b, ln2_g, ln2_b, even_w_in, even_a_dw, even_a_dw_b, even_a_ln_g, even_a_ln_b, even_b_dw, even_w_out, odd_w_in, odd_c_dw, odd_c_dw_b, odd_w_gate_a, odd_b_gate_a, odd_w_gate_x, odd_b_gate_x, odd_lam, odd_w_out, w_router, b_router, moe_w1, moe_w3, moe_w2):
    depth = ln1_g.shape[0]
    alpha = (2.0 * depth) ** 0.25
    moe = None
    for layer in range(depth):
        j = layer // 2
        if layer % 2 == 0:
            if moe is not None:
                x = _moe_ln(x, *moe, alpha=alpha)
            x, slots, gates, tab, xb = _mix0(
                x, even_w_in[j], even_a_dw[j], even_a_dw_b[j], even_a_ln_g[j], even_a_ln_b[j],
                even_b_dw[j], even_w_out[j], ln1_g[layer], ln1_b[layer], w_router, b_router, alpha=alpha)
        else:
            x, slots, gates, tab, xb = _mix1(
                x, odd_w_in[j], odd_c_dw[j], odd_c_dw_b[j], odd_w_gate_a[j], odd_b_gate_a[j],
                odd_w_gate_x[j], odd_b_gate_x[j], odd_lam[j], odd_w_out[j], ln1_g[layer], ln1_b[layer],
                w_router, b_router, alpha=alpha, moe=moe)
        y, seg_pos, seg_len = _moe_rows(xb, slots, tab, moe_w1, moe_w3, moe_w2, layer)
        moe = (y, slots, gates, seg_pos, seg_len, ln2_g[layer], ln2_b[layer])
    return x if moe is None else _moe_ln(x, *moe, alpha=alpha)
```

```python
import functools
import math

import jax
import jax.numpy as jnp
from jax import lax
from jax.experimental import pallas as pl
from jax.experimental.pallas import tpu as pltpu

F32 = jnp.float32
BF16 = jnp.bfloat16
I32 = jnp.int32
LN_EPS = 1e-5
LRU_C = 8.0
SUBLANES = 8
LANES = 128
HALO_A = 32
HALO_B = 8
EXPERTS_PER_GROUP = 4
BLOCK_DMA_PRIORITY = 1
TILES_PER_STEP = 4
COMBINE_SLOTS = 3
DISPATCH_SLOTS = 3
EXPERT_IN_SLOTS = 4
SEQ_TILE = 512
TOKEN_TILE = 256
EXPERT_BLOCK = 256
VMEM_LIMIT = 56 * 1024 * 1024


def _layer_norm(v, g, b):
    mu = jnp.mean(v, axis=-1, keepdims=True)
    c = v - mu
    var = jnp.mean(c * c, axis=-1, keepdims=True)
    return c * lax.rsqrt(var + LN_EPS) * g + b


def _row(v):
    return v.reshape(1, -1).astype(F32)


def _tap_rows(w):
    return jnp.broadcast_to(w.astype(F32)[:, None, :], (w.shape[0], SUBLANES, w.shape[1]))


def _tap(w_ref, k, rows):
    return jnp.concatenate([w_ref[k]] * (rows // SUBLANES), axis=0)


def _const_spec(shape):
    return pl.BlockSpec(shape, lambda *_: (0,) * len(shape))


def _mix0_kernel(x_ref, win_ref, adw_ref, adwb_ref, alng_ref, alnb_ref, bdw_ref, wout_ref, lng_ref, lnb_ref,
                 wrt_ref, br_ref, o_ref, slot_ref, gate_ref, tab_ref, xb_ref, *scratch,
                 alpha, ts, tm, d_a, d_b, rc, parts):
    tri, base = scratch[-2:]
    bufs = [scratch[5 * p:5 * p + 5] for p in range(parts)]
    first = (pl.program_id(0) == 0) & (pl.program_id(1) == 0)
    n_a = adw_ref.shape[0]
    n_b = bdw_ref.shape[0]
    hs = ts // parts

    @pl.when(pl.program_id(1) == 0)
    def _():
        bufs[0][0][0:HALO_A, :] = jnp.zeros((HALO_A, d_a), F32)
        bufs[0][2][0:HALO_B, :] = jnp.zeros((HALO_B, d_b), F32)

    x = x_ref[0]
    xb = x.astype(BF16)
    cw = 2 * LANES
    dot = functools.partial(jnp.dot, preferred_element_type=F32)

    def proj_units(p):
        abuf, ash, cbuf, bgate, cat = bufs[p]
        xp = xb[p * hs:(p + 1) * hs, :]
        units = []
        for c0 in range(0, d_a, cw):
            def unit(c0=c0):
                val = dot(xp, win_ref[:, c0:c0 + cw])
                gate = dot(xp, win_ref[:, d_a + c0:d_a + c0 + cw])
                abuf[HALO_A:HALO_A + hs, c0:c0 + cw] = val * jax.nn.sigmoid(gate)
            units.append(unit)
        for c0 in range(0, d_b, cw):
            def unit(c0=c0, o=2 * d_a):
                bgate[:, c0:c0 + cw] = dot(xp, win_ref[:, o + c0:o + c0 + cw])
                cbuf[HALO_B:HALO_B + hs, c0:c0 + cw] = (dot(xp, win_ref[:, o + d_b + c0:o + d_b + c0 + cw])
                                                        * dot(xp, win_ref[:, o + 2 * d_b + c0:o + 2 * d_b + c0 + cw]))
            units.append(unit)
        return units

    def conv_units(p):
        abuf, ash, cbuf, bgate, cat = bufs[p]

        def shifted():
            n_sh = hs + HALO_A - SUBLANES
            for j in range(1, SUBLANES):
                ash[j - 1, :, :] = abuf[j:j + n_sh, :]
        units = [shifted]
        for c in range(hs // rc):
            def unit(r0=c * rc):
                acc = adwb_ref[...]
                for k in range(n_a):
                    q, j = divmod(HALO_A - (n_a - 1) + k, SUBLANES)
                    lo = r0 + q * SUBLANES
                    src = abuf[lo:lo + rc, :] if j == 0 else ash[j - 1, lo:lo + rc, :]
                    acc = acc + _tap(adw_ref, k, rc) * src
                a = _layer_norm(acc, alng_ref[...], alnb_ref[...])
                cat[r0:r0 + rc, 0:d_a] = (a * jax.nn.sigmoid(a)).astype(BF16)
                bb = jnp.zeros((rc, d_b), F32)
                for k in range(n_b):
                    lo = r0 + HALO_B - (n_b - 1) + k
                    bb = bb + _tap(bdw_ref, k, rc) * cbuf[lo:lo + rc, :]
                cat[r0:r0 + rc, d_a:d_a + d_b] = (bgate[r0:r0 + rc, :] * bb).astype(BF16)
            units.append(unit)
        return units

    m_cols = [[] for _ in range(parts)]

    def out_units(p):
        cat = bufs[p][4]
        units = []
        for c0 in range(0, wout_ref.shape[1], cw):
            def unit(c0=c0):
                m_cols[p].append(dot(cat[...], wout_ref[:, c0:c0 + cw]))
            units.append(unit)
        return units

    def run_interleaved(*groups):
        order = sorted(((i + 0.5) / len(g), gi, i) for gi, g in enumerate(groups) for i in range(len(g)))
        for _, gi, i in order:
            groups[gi][i]()

    for s in range(parts + 2):
        groups = []
        if s < parts:
            groups.append(proj_units(s))
        if 0 <= s - 1 < parts:
            groups.append(conv_units(s - 1))
        if 0 <= s - 2 < parts:
            groups.append(out_units(s - 2))
        run_interleaved(*groups)
        if s + 1 < parts:
            bufs[s + 1][0][0:HALO_A, :] = bufs[s][0][hs:hs + HALO_A, :]
            bufs[s + 1][2][0:HALO_B, :] = bufs[s][2][hs:hs + HALO_B, :]

    bufs[0][0][0:HALO_A, :] = bufs[-1][0][hs:hs + HALO_A, :]
    bufs[0][2][0:HALO_B, :] = bufs[-1][2][hs:hs + HALO_B, :]

    m = jnp.concatenate([jnp.concatenate(cols, axis=1) for cols in m_cols], axis=0)
    _ln_and_route(alpha * x + m, lng_ref, lnb_ref, wrt_ref, br_ref, o_ref, slot_ref, gate_ref, tab_ref, xb_ref,
                  tri, base, first, tm=tm)


def _mix0(x, w_in, a_dw, a_dw_b, a_ln_g, a_ln_b, b_dw, w_out, ln_g, ln_b, w_router, b_router,
          *, alpha, ts=SEQ_TILE, tm=TOKEN_TILE, rc=32, parts=2):
    bsz, seq, d = x.shape
    d_a = a_dw.shape[-1]
    d_b = b_dw.shape[-1]
    assert a_dw.shape[0] - 1 <= HALO_A and b_dw.shape[0] - 1 <= HALO_B
    ts = min(ts, seq)
    hs = ts // parts
    rc = min(rc, hs)
    assert seq % ts == 0 and ts % parts == 0 and hs % rc == 0 and rc % (2 * SUBLANES) == 0
    r_in, r_args, r_out_specs, r_out_shape, r_scratch, tm = _router_io(w_router, b_router, bsz, seq, ts, tm)
    kern = functools.partial(_mix0_kernel, alpha=alpha, ts=ts, tm=tm, d_a=d_a, d_b=d_b, rc=rc, parts=parts)
    part_scratch = [
        pltpu.VMEM((HALO_A + hs, d_a), F32),
        pltpu.VMEM((SUBLANES - 1, HALO_A + hs - SUBLANES, d_a), F32),
        pltpu.VMEM((HALO_B + hs, d_b), F32),
        pltpu.VMEM((hs, d_b), F32),
        pltpu.VMEM((hs, d_a + d_b), BF16),
    ]
    return pl.pallas_call(
        kern,
        grid=(bsz, seq // ts),
        in_specs=[
            pl.BlockSpec((1, ts, d), lambda b, s: (b, s, 0)),
            _const_spec(w_in.shape), _const_spec((a_dw.shape[0], SUBLANES, d_a)), _const_spec((1, d_a)),
            _const_spec((1, d_a)), _const_spec((1, d_a)), _const_spec((b_dw.shape[0], SUBLANES, d_b)),
            _const_spec(w_out.shape),
            _const_spec((1, d)), _const_spec((1, d)),
        ] + r_in,
        out_specs=[pl.BlockSpec((1, ts, d), lambda b, s: (b, s, 0))] + r_out_specs,
        out_shape=[jax.ShapeDtypeStruct(x.shape, F32)] + r_out_shape,
        scratch_shapes=part_scratch * parts + r_scratch,
        compiler_params=pltpu.CompilerParams(
            dimension_semantics=("arbitrary", "arbitrary"), vmem_limit_bytes=VMEM_LIMIT),
        name="mix0",
    )(x, w_in.astype(BF16), _tap_rows(a_dw), _row(a_dw_b), _row(a_ln_g), _row(a_ln_b),
      _tap_rows(b_dw), w_out.astype(BF16), _row(ln_g), _row(ln_b), *r_args)


def _mix1_kernel(x_ref, win_ref, cdw_ref, cdwb_ref, wga_ref, bga_ref, wgx_ref, bgx_ref, lam_ref, wout_ref,
                 lng_ref, lnb_ref, wrt_ref, br_ref, o_ref, slot_ref, gate_ref, tab_ref, xb_ref,
                 cbuf, ybr, abuf, bbuf, hcar, tri, base, *, alpha, ts, tm, d_rnn, rc):
    first = (pl.program_id(0) == 0) & (pl.program_id(1) == 0)

    @pl.when(pl.program_id(1) == 0)
    def _():
        cbuf[0:HALO_B, :] = jnp.zeros((HALO_B, d_rnn), F32)
        hcar[...] = jnp.zeros((1, d_rnn), F32)

    x = x_ref[0] if len(x_ref.shape) == 3 else x_ref[...]
    xb = x.astype(BF16)
    ybr[...] = jnp.dot(xb, win_ref[:, 0:d_rnn], preferred_element_type=F32)
    cbuf[HALO_B:HALO_B + ts, :] = jnp.dot(xb, win_ref[:, d_rnn:2 * d_rnn], preferred_element_type=F32)

    n_heads, blk = wga_ref.shape[0], wga_ref.shape[1]
    n_c = cdw_ref.shape[0]
    neg_c_sp = -LRU_C * jax.nn.softplus(-lam_ref[...])
    for c in range(ts // rc):
        r0 = c * rc
        xr = cdwb_ref[...]
        for k in range(n_c):
            lo = r0 + HALO_B - (n_c - 1) + k
            xr = xr + _tap(cdw_ref, k, rc) * cbuf[lo:lo + rc, :]
        xrb = xr.astype(BF16)
        for h in range(n_heads):
            cs = slice(h * blk, (h + 1) * blk)
            r = jax.nn.sigmoid(jnp.dot(xrb[:, cs], wga_ref[h], preferred_element_type=F32) + bga_ref[:, cs])
            i = jax.nn.sigmoid(jnp.dot(xrb[:, cs], wgx_ref[h], preferred_element_type=F32) + bgx_ref[:, cs])
            log_a = neg_c_sp[:, cs] * r
            abuf[r0:r0 + rc, cs] = jnp.exp(log_a)
            th = jnp.tanh(log_a)
            bbuf[r0:r0 + rc, cs] = jnp.sqrt(-2.0 * th / (1.0 - th)) * (i * xr[:, cs])

    cbuf[0:HALO_B, :] = cbuf[ts:ts + HALO_B, :]

    rowid = lax.broadcasted_iota(I32, (SUBLANES, d_rnn), 0)

    def group(g, h_prev):
        r = pl.multiple_of(g * SUBLANES, SUBLANES)
        a = abuf[pl.ds(r, SUBLANES), :]
        b = bbuf[pl.ds(r, SUBLANES), :]
        step = 1
        while step < SUBLANES:
            keep = rowid >= step
            b = jnp.where(keep, a * pltpu.roll(b, step, axis=0) + b, b)
            a = jnp.where(keep, a * pltpu.roll(a, step, axis=0), a)
            step *= 2
        h = a * h_prev + b
        bbuf[pl.ds(r, SUBLANES), :] = h
        return h[SUBLANES - 1:SUBLANES, :]

    hcar[...] = lax.fori_loop(0, ts // SUBLANES, group, hcar[...], unroll=4)

    out = (jax.nn.gelu(ybr[...]) * bbuf[...]).astype(BF16)
    m = jnp.dot(out, wout_ref[...], preferred_element_type=F32)
    _ln_and_route(alpha * x + m, lng_ref, lnb_ref, wrt_ref, br_ref, o_ref, slot_ref, gate_ref, tab_ref, xb_ref,
                  tri, base, first, tm=tm)


def _mix1_after_moe_kernel(seg_src_ref, seg_len_ref, x_ref, cslot_ref, cgate_ref, clng_ref, clnb_ref, y_hbm, *rest,
                           alpha, ts, tm, d_rnn, rc, n_exp):
    *mix, xbuf, grouped, csems = rest
    group = ts // tm
    step = pl.program_id(0) * pl.num_programs(1) + pl.program_id(1)

    def store(rows, v):
        xbuf[rows, :] = v

    _combine_tiles(step, pl.num_programs(0) * pl.num_programs(1) * group, seg_src_ref, seg_len_ref,
                   lambda rows: x_ref[0, rows, :], cslot_ref, cgate_ref, clng_ref, clnb_ref, y_hbm, store,
                   grouped, csems, alpha=alpha, tm=tm, n_exp=n_exp, group=group)
    _mix1_kernel(xbuf, *mix, alpha=alpha, ts=ts, tm=tm, d_rnn=d_rnn, rc=rc)


def _mix1(x, w_in, c_dw, c_dw_b, w_gate_a, b_gate_a, w_gate_x, b_gate_x, lam, w_out, ln_g, ln_b,
          w_router, b_router, *, alpha, ts=SEQ_TILE, tm=TOKEN_TILE, rc=128, moe=None):
    bsz, seq, d = x.shape
    d_rnn = c_dw.shape[-1]
    assert c_dw.shape[0] - 1 <= HALO_B
    ts = min(ts, seq)
    rc = min(rc, ts)
    assert seq % ts == 0 and ts % rc == 0 and ts % SUBLANES == 0
    r_in, r_args, r_out_specs, r_out_shape, r_scratch, tm = _router_io(w_router, b_router, bsz, seq, ts, tm)
    kern = functools.partial(_mix1_kernel, alpha=alpha, ts=ts, tm=tm, d_rnn=d_rnn, rc=rc)
    in_specs = [
        pl.BlockSpec((1, ts, d), lambda b, s, *_: (b, s, 0)),
        _const_spec(w_in.shape), _const_spec((c_dw.shape[0], SUBLANES, d_rnn)), _const_spec((1, d_rnn)),
        _const_spec(w_gate_a.shape), _const_spec((1, d_rnn)),
        _const_spec(w_gate_x.shape), _const_spec((1, d_rnn)),
        _const_spec((1, d_rnn)), _const_spec(w_out.shape), _const_spec((1, d)), _const_spec((1, d)),
    ] + r_in
    scratch = [
        pltpu.VMEM((HALO_B + ts, d_rnn), F32),
        pltpu.VMEM((ts, d_rnn), F32),
        pltpu.VMEM((ts, d_rnn), F32),
        pltpu.VMEM((ts, d_rnn), F32),
        pltpu.VMEM((1, d_rnn), F32),
    ] + r_scratch
    args = (x, w_in.astype(BF16), _tap_rows(c_dw), _row(c_dw_b), w_gate_a.astype(BF16), _row(b_gate_a),
            w_gate_x.astype(BF16), _row(b_gate_x), _row(lam), w_out.astype(BF16), _row(ln_g), _row(ln_b), *r_args)
    prefetch = ()
    if moe is not None:
        y, m_slots, m_gates, seg_src, seg_len, m_ln_g, m_ln_b = moe
        n_exp = w_router.shape[1]
        assert m_slots.shape == (bsz * seq // tm, 2, tm) and d == SUBLANES * LANES
        kern = functools.partial(_mix1_after_moe_kernel, alpha=alpha, ts=ts, tm=tm, d_rnn=d_rnn, rc=rc, n_exp=n_exp)
        per_tile = lambda b, s, *_: (b * (seq // ts) + s, 0, 0)
        in_specs = [in_specs[0], pl.BlockSpec((ts // tm, 2, tm), per_tile), pl.BlockSpec((ts // tm, 2, tm), per_tile),
                    _const_spec((1, d)), _const_spec((1, d)), pl.BlockSpec(memory_space=pl.ANY)] + in_specs[1:]
        scratch = scratch + [pltpu.VMEM((ts, d), F32), pltpu.VMEM((COMBINE_SLOTS, 2 * tm * SUBLANES, LANES), F32),
                             pltpu.SemaphoreType.DMA((COMBINE_SLOTS,))]
        args = (x, m_slots, m_gates, _row(m_ln_g), _row(m_ln_b), y) + args[1:]
        prefetch = (seg_src, seg_len)
    grid_spec = pltpu.PrefetchScalarGridSpec(
        num_scalar_prefetch=len(prefetch),
        grid=(bsz, seq // ts),
        in_specs=in_specs,
        out_specs=[pl.BlockSpec((1, ts, d), lambda b, s, *_: (b, s, 0))] + r_out_specs,
        scratch_shapes=scratch,
    )
    return pl.pallas_call(
        kern,
        grid_spec=grid_spec,
        out_shape=[jax.ShapeDtypeStruct(x.shape, F32)] + r_out_shape,
        compiler_params=pltpu.CompilerParams(
            dimension_semantics=("arbitrary", "arbitrary"), vmem_limit_bytes=VMEM_LIMIT),
        name="mix1",
    )(*prefetch, *args)


def _first_argmax(vals):
    best, idx = vals[0], jnp.zeros(vals[0].shape, I32)
    for j in range(1, len(vals)):
        upd = vals[j] > best
        best = jnp.where(upd, vals[j], best)
        idx = jnp.where(upd, j, idx)
    return best, idx


def _route_tile(x, wrt_ref, br_ref, tri, base, tm):
    n_exp = wrt_ref.shape[0]
    tt = x.shape[0]
    w = wrt_ref[...]
    w_hi = w.astype(BF16)
    w_lo = (w - w_hi.astype(F32)).astype(BF16)
    x_hi = x.astype(BF16)
    x_lo = (x - x_hi.astype(F32)).astype(BF16)
    over_d = (((1,), (1,)), ((), ()))
    logits = (lax.dot_general(w_hi, x_hi, over_d, preferred_element_type=F32)
              + lax.dot_general(w_hi, x_lo, over_d, preferred_element_type=F32)
              + lax.dot_general(w_lo, x_hi, over_d, preferred_element_type=F32))
    logits = logits + br_ref[:, 0:1]
    e = jnp.exp(logits - jnp.max(logits, axis=0, keepdims=True))
    p = e / jnp.sum(e, axis=0, keepdims=True)
    rows = [p[j:j + 1, :] for j in range(n_exp)]

    n_grp = n_exp // EXPERTS_PER_GROUP
    scores = []
    for g in range(n_grp):
        a, b, c, d = rows[EXPERTS_PER_GROUP * g:EXPERTS_PER_GROUP * (g + 1)]
        scores.append(jnp.maximum(jnp.maximum(a, b) + jnp.maximum(c, d), jnp.maximum(a + b, c + d)))
    _, g_sel = _first_argmax(scores)

    p_in = []
    for j in range(EXPERTS_PER_GROUP):
        v = rows[j]
        for g in range(1, n_grp):
            v = jnp.where(g_sel == g, rows[EXPERTS_PER_GROUP * g + j], v)
        p_in.append(v)
    v1, i1 = _first_argmax(p_in)
    v2, i2 = _first_argmax([jnp.where(i1 == j, -1.0, p_in[j]) for j in range(EXPERTS_PER_GROUP)])
    den = v1 + v2
    e1 = g_sel * EXPERTS_PER_GROUP + i1
    e2 = g_sel * EXPERTS_PER_GROUP + i2

    eid = lax.broadcasted_iota(I32, (n_exp, tt), 0)
    hit1 = eid == e1
    hit2 = eid == e2
    member = jnp.where(hit1 | hit2, 1.0, 0.0)
    member_b = member.astype(BF16)
    lower = (lax.broadcasted_iota(I32, (n_exp, n_exp), 1) < lax.broadcasted_iota(I32, (n_exp, n_exp), 0))
    lower = jnp.where(lower, 1.0, 0.0).astype(BF16)
    lane = lax.broadcasted_iota(I32, (n_exp, LANES), 1)
    place, tabs = [], []
    for h in range(tt // tm):
        cols = slice(h * tm, (h + 1) * tm)
        incl = jnp.dot(member_b[:, cols], tri[...], preferred_element_type=F32)
        count = jnp.sum(member[:, cols], axis=1, keepdims=True)
        offset = jnp.dot(lower, jnp.broadcast_to(count, (n_exp, LANES)).astype(BF16),
                         preferred_element_type=F32)[:, 0:1]
        place.append(offset + incl - member[:, cols])
        tabs.append(jnp.where(lane == 0, base, jnp.where(lane == 1, count, 0.0)).astype(I32))
        base = base + count
    place = jnp.concatenate(place, axis=1)
    s1 = jnp.sum(jnp.where(hit1, place, 0.0), axis=0, keepdims=True)
    s2 = jnp.sum(jnp.where(hit2, place, 0.0), axis=0, keepdims=True)
    return (s1.astype(I32), s2.astype(I32)), (v1 / den, v2 / den), tabs, base


def _ln_and_route(v, lng_ref, lnb_ref, wrt_ref, br_ref, o_ref, slot_ref, gate_ref, tab_ref, xb_ref, tri, base, first,
                  *, tm):
    @pl.when(first)
    def _():
        src = lax.broadcasted_iota(I32, (tm, tm), 0)
        dst = lax.broadcasted_iota(I32, (tm, tm), 1)
        tri[...] = jnp.where(src <= dst, 1.0, 0.0).astype(BF16)
        base[...] = jnp.zeros(base.shape, F32)

    x1 = _layer_norm(v, lng_ref[...], lnb_ref[...])
    o_ref[0] = x1
    xb_ref[0] = x1.astype(BF16)
    slots, gates, tabs, new_base = _route_tile(x1, wrt_ref, br_ref, tri, base[...], tm)
    for h, tab in enumerate(tabs):
        for k in range(2):
            slot_ref[h, k:k + 1, :] = slots[k][:, h * tm:(h + 1) * tm]
            gate_ref[h, k:k + 1, :] = gates[k][:, h * tm:(h + 1) * tm]
        tab_ref[h] = tab
    base[...] = new_base


def _router_io(w_router, b_router, bsz, seq, ts, tm):
    d, n_exp = w_router.shape
    tm = min(tm, ts)
    assert ts % tm == 0 and n_exp % EXPERTS_PER_GROUP == 0 and tm <= 256
    t = bsz * seq
    per_seq = seq // ts
    in_specs = [_const_spec((n_exp, d)), _const_spec((n_exp, LANES))]
    args = (w_router.T.astype(F32), jnp.broadcast_to(b_router.astype(F32)[:, None], (n_exp, LANES)))
    per_tile = lambda b, s, *_: (b * per_seq + s, 0, 0)
    out_specs = [
        pl.BlockSpec((ts // tm, 2, tm), per_tile),
        pl.BlockSpec((ts // tm, 2, tm), per_tile),
        pl.BlockSpec((ts // tm, n_exp, LANES), per_tile),
        pl.BlockSpec((1, ts, d), lambda b, s, *_: (b, s, 0)),
    ]
    out_shape = [
        jax.ShapeDtypeStruct((t // tm, 2, tm), I32),
        jax.ShapeDtypeStruct((t // tm, 2, tm), F32),
        jax.ShapeDtypeStruct((t // tm, n_exp, LANES), I32),
        jax.ShapeDtypeStruct((bsz, seq, d), BF16),
    ]
    scratch = [pltpu.VMEM((tm, tm), BF16), pltpu.VMEM((n_exp, LANES), F32)]
    return in_specs, args, out_specs, out_shape, scratch, tm


def _segment_copies(src_of, dst_of, length, max_len, sem, *, wait=False, priority=0):
    done = 0
    for bit in reversed(range(max_len.bit_length())):
        n = 1 << bit
        piece = length & n

        @pl.when(piece != 0)
        def _(done=done, n=n):
            copy = pltpu.make_async_copy(src_of(done, n), dst_of(done, n), sem)
            copy.wait() if wait else copy.start(priority)

        done = done + piece


def _token_rows(ref, lo, n):
    return ref.at[pl.ds(pl.multiple_of(lo * SUBLANES, SUBLANES), n * SUBLANES)]


def _store_token_major(ref, val):
    for j in range(SUBLANES):
        ref[pl.ds(j, val.shape[0], stride=SUBLANES), :] = val[:, j * LANES:(j + 1) * LANES]


def _load_token_major(ref, rows):
    return jnp.concatenate([ref[pl.ds(j, rows, stride=SUBLANES), :] for j in range(SUBLANES)], axis=1)


def _dispatch_kernel(seg_dst_ref, seg_len_ref, pad_dst_ref, pad_len_ref, n_used_ref, x_ref, slot_ref, xs_hbm,
                     grouped, zeros, sems, pad_sem, *, tm, n_exp, bm, group):
    i = pl.program_id(0)
    n_tiles = pl.num_programs(0) * group

    def wait_tile(b):
        pltpu.make_async_copy(grouped.at[b], _token_rows(xs_hbm, 0, 2 * tm), sems.at[b]).wait()

    def pad_copies(wait):
        for e in range(n_exp):
            _segment_copies(lambda lo, n: _token_rows(zeros, lo, n),
                            lambda lo, n, e=e: _token_rows(xs_hbm, pad_dst_ref[e] + lo, n),
                            pad_len_ref[e], bm - 1, pad_sem, wait=wait)
        n_blk = xs_hbm.shape[0] // (bm * SUBLANES)
        for blk in range(n_blk - n_exp, n_blk):
            @pl.when(blk >= n_used_ref[0])
            def _(blk=blk):
                copy = pltpu.make_async_copy(zeros, _token_rows(xs_hbm, blk * bm, bm), pad_sem)
                copy.wait() if wait else copy.start()

    @pl.when(i == 0)
    def _():
        zeros[...] = jnp.zeros(zeros.shape, F32)
        pad_copies(wait=False)

    n_buf = grouped.shape[0]

    def send(tile, live=True):
        b = lax.rem(tile, n_buf)
        start = 0
        for e in range(n_exp):
            length = jnp.where(live, seg_len_ref[tile * n_exp + e], 0)
            dst = seg_dst_ref[tile * n_exp + e]
            _segment_copies(lambda lo, n, start=start: _token_rows(grouped.at[b], start + lo, n),
                            lambda lo, n, dst=dst: _token_rows(xs_hbm, dst + lo, n),
                            length, tm, sems.at[b])
            start = start + length

    def tile(h, carry):
        k = i * group + h
        buf = lax.rem(k, n_buf)

        @pl.when(k >= n_buf)
        def _():
            wait_tile(buf)

        slots = slot_ref[h]
        row = lax.broadcasted_iota(I32, (2 * tm, tm), 0)
        onehot = jnp.where((row == slots[0:1, :]) | (row == slots[1:2, :]), 1.0, 0.0).astype(BF16)
        xt = x_ref[pl.ds(pl.multiple_of(h * tm, tm), tm), :].astype(BF16)
        rows = jnp.dot(onehot, xt, preferred_element_type=F32)
        send(jnp.maximum(k - 1, 0), live=k >= 1)
        _store_token_major(grouped.at[buf], rows)
        return carry

    lax.fori_loop(0, group, tile, 0)

    @pl.when(i == pl.num_programs(0) - 1)
    def _():
        send(n_tiles - 1)
        for back in range(n_buf):
            @pl.when(n_tiles > back)
            def _(back=back):
                wait_tile(lax.rem(n_tiles - 1 - back, n_buf))

        pad_copies(wait=True)


def _dispatch(xf, slots, seg_dst, seg_len, pad_dst, pad_len, n_used, n_pad, *, tm, bm, group):
    t, d = xf.shape
    assert d == SUBLANES * LANES and t % (tm * group) == 0
    n_exp = pad_dst.shape[0]
    grid_spec = pltpu.PrefetchScalarGridSpec(
        num_scalar_prefetch=5,
        grid=(t // (tm * group),),
        in_specs=[
            pl.BlockSpec((group * tm, d), lambda i, *_: (i, 0)),
            pl.BlockSpec((group, 2, tm), lambda i, *_: (i, 0, 0)),
        ],
        out_specs=pl.BlockSpec(memory_space=pl.ANY),
        scratch_shapes=[
            pltpu.VMEM((DISPATCH_SLOTS, 2 * tm * SUBLANES, LANES), F32),
            pltpu.VMEM((bm * SUBLANES, LANES), F32),
            pltpu.SemaphoreType.DMA((DISPATCH_SLOTS,)),
            pltpu.SemaphoreType.DMA(()),
        ],
    )
    return pl.pallas_call(
        functools.partial(_dispatch_kernel, tm=tm, n_exp=n_exp, bm=bm, group=group),
        grid_spec=grid_spec,
        out_shape=jax.ShapeDtypeStruct((n_pad * SUBLANES, LANES), F32),
        compiler_params=pltpu.CompilerParams(
            dimension_semantics=("arbitrary",), vmem_limit_bytes=VMEM_LIMIT),
        name="dispatch",
    )(seg_dst, seg_len, pad_dst, pad_len, n_used, xf, slots)


def _experts_kernel(first_blk_ref, n_blk_ref, n_used_ref, xs_hbm, w1_ref, w3_ref, w2_ref, y_hbm,
                    w1b, w3b, w2b, xbuf, ybuf, in_sems, out_sems, *, bm, n_exp):
    e = pl.program_id(0)
    first = first_blk_ref[e]
    n = n_blk_ref[e]

    n_x = xbuf.shape[0]
    ahead = n_x - 1

    def x_copy(blk, slot):
        return pltpu.make_async_copy(_token_rows(xs_hbm, blk * bm, bm), xbuf.at[slot], in_sems.at[slot])

    def y_copy(blk, slot):
        return pltpu.make_async_copy(ybuf.at[slot], _token_rows(y_hbm, blk * bm, bm), out_sems.at[slot])

    def fetch_head(first_blk, count):
        for k in range(ahead):
            @pl.when(k < count)
            def _(k=k):
                x_copy(first_blk + k, k).start(BLOCK_DMA_PRIORITY)

    @pl.when(e == 0)
    def _():
        fetch_head(first, n)

    w1b[...] = w1_ref[0, 0].astype(BF16)
    w3b[...] = w3_ref[0, 0].astype(BF16)
    w2b[...] = w2_ref[0, 0].astype(BF16)

    def block(j, carry):
        slot = lax.rem(j, 2)
        x_slot = lax.rem(j, n_x)

        @pl.when(j + ahead < n)
        def _():
            x_copy(first + j + ahead, lax.rem(j + ahead, n_x)).start(BLOCK_DMA_PRIORITY)

        x_copy(first + j, x_slot).wait()

        @pl.when(j >= 2)
        def _():
            y_copy(first + j - 2, slot).wait()

        xb = _load_token_major(xbuf.at[x_slot], bm).astype(BF16)
        h1 = jnp.dot(xb, w1b[...], preferred_element_type=F32)
        h3 = jnp.dot(xb, w3b[...], preferred_element_type=F32)
        hb = (h1 * jax.nn.sigmoid(h1) * h3).astype(BF16)
        _store_token_major(ybuf.at[slot], jnp.dot(hb, w2b[...], preferred_element_type=F32))
        y_copy(first + j, slot).start(BLOCK_DMA_PRIORITY)
        return carry

    lax.fori_loop(0, n, block, 0)

    nxt = jnp.minimum(e + 1, n_exp - 1)

    @pl.when(e + 1 < n_exp)
    def _():
        fetch_head(first_blk_ref[nxt], n_blk_ref[nxt])

    for back in (2, 1):
        @pl.when(n >= back)
        def _(back=back):
            y_copy(first + n - back, lax.rem(n - back, 2)).wait()

    @pl.when(e == n_exp - 1)
    def _():
        ybuf[0] = jnp.zeros(ybuf.shape[1:], F32)
        n_all = y_hbm.shape[0] // (bm * SUBLANES)
        for blk in range(n_all - n_exp, n_all):
            @pl.when(blk >= n_used_ref[0])
            def _(blk=blk):
                y_copy(blk, 0).start(BLOCK_DMA_PRIORITY)
                y_copy(blk, 0).wait()


def _experts(xs, first_blk, n_blk, n_used, w1, w3, w2, layer, *, bm):
    n_exp, d, d_e = w1.shape[-3:]
    assert d == SUBLANES * LANES
    w_blk = lambda e, *_: (layer, e, 0, 0)
    grid_spec = pltpu.PrefetchScalarGridSpec(
        num_scalar_prefetch=3,
        grid=(n_exp,),
        in_specs=[
            pl.BlockSpec(memory_space=pl.ANY),
            pl.BlockSpec((1, 1, d, d_e), w_blk),
            pl.BlockSpec((1, 1, d, d_e), w_blk),
            pl.BlockSpec((1, 1, d_e, d), w_blk),
        ],
        out_specs=pl.BlockSpec(memory_space=pl.ANY),
        scratch_shapes=[
            pltpu.VMEM((d, d_e), BF16), pltpu.VMEM((d, d_e), BF16), pltpu.VMEM((d_e, d), BF16),
            pltpu.VMEM((EXPERT_IN_SLOTS, bm * SUBLANES, LANES), F32), pltpu.VMEM((2, bm * SUBLANES, LANES), F32),
            pltpu.SemaphoreType.DMA((EXPERT_IN_SLOTS,)), pltpu.SemaphoreType.DMA((2,)),
        ],
    )
    return pl.pallas_call(
        functools.partial(_experts_kernel, bm=bm, n_exp=n_exp),
        grid_spec=grid_spec,
        out_shape=jax.ShapeDtypeStruct(xs.shape, F32),
        compiler_params=pltpu.CompilerParams(
            dimension_semantics=("arbitrary",), vmem_limit_bytes=VMEM_LIMIT),
        name="experts",
    )(first_blk, n_blk, n_used, xs, w1, w3, w2)


def _combine_tiles(step, n_tiles, seg_src_ref, seg_len_ref, load_x, slot_ref, gate_ref, lng_ref, lnb_ref, y_hbm,
                   store, grouped, sems, *, alpha, tm, n_exp, group):
    def fetch(tile, b, live=True):
        start = 0
        for e in range(n_exp):
            length = jnp.where(live, seg_len_ref[tile * n_exp + e], 0)
            src = seg_src_ref[tile * n_exp + e]
            _segment_copies(lambda lo, n, src=src: _token_rows(y_hbm, src + lo, n),
                            lambda lo, n, start=start: _token_rows(grouped.at[b], start + lo, n),
                            length, tm, sems.at[b], priority=BLOCK_DMA_PRIORITY)
            start = start + length

    n_buf = grouped.shape[0]
    ahead = n_buf - 1

    @pl.when(step == 0)
    def _():
        for k0 in range(ahead):
            @pl.when(k0 < n_tiles)
            def _(k0=k0):
                fetch(k0, k0)

    def tile(h, carry):
        k = step * group + h
        buf = lax.rem(k, n_buf)

        pltpu.make_async_copy(_token_rows(y_hbm, 0, 2 * tm), grouped.at[buf], sems.at[buf]).wait()

        slots = slot_ref[h]
        gates = gate_ref[h]
        row = lax.broadcasted_iota(I32, (2 * tm, tm), 0)
        g = (jnp.where(row == slots[0:1, :], gates[0:1, :], 0.0)
             + jnp.where(row == slots[1:2, :], gates[1:2, :], 0.0))
        yb = _load_token_major(grouped.at[buf], 2 * tm).astype(BF16)
        f = lax.dot_general(g.astype(BF16), yb, (((0,), (0,)), ((), ())), preferred_element_type=F32)

        nxt = k + ahead
        fetch(jnp.minimum(nxt, n_tiles - 1), lax.rem(nxt, n_buf), live=nxt < n_tiles)

        rows = pl.ds(pl.multiple_of(h * tm, tm), tm)
        store(rows, _layer_norm(alpha * load_x(rows) + f, lng_ref[...], lnb_ref[...]))
        return carry

    lax.fori_loop(0, group, tile, 0, unroll=group <= 2)


def _combine_kernel(seg_src_ref, seg_len_ref, x_ref, slot_ref, gate_ref, lng_ref, lnb_ref, y_hbm, o_ref,
                    grouped, sems, *, alpha, tm, n_exp, group):
    def store(rows, v):
        o_ref[rows, :] = v

    _combine_tiles(pl.program_id(0), pl.num_programs(0) * group, seg_src_ref, seg_len_ref,
                   lambda rows: x_ref[rows, :], slot_ref, gate_ref, lng_ref, lnb_ref, y_hbm, store, grouped, sems,
                   alpha=alpha, tm=tm, n_exp=n_exp, group=group)


def _combine(xf, y, slots, gates, seg_src, seg_len, ln_g, ln_b, *, alpha, tm, n_exp, group):
    t, d = xf.shape
    assert t % (tm * group) == 0
    grid_spec = pltpu.PrefetchScalarGridSpec(
        num_scalar_prefetch=2,
        grid=(t // (tm * group),),
        in_specs=[
            pl.BlockSpec((group * tm, d), lambda i, *_: (i, 0)),
            pl.BlockSpec((group, 2, tm), lambda i, *_: (i, 0, 0)),
            pl.BlockSpec((group, 2, tm), lambda i, *_: (i, 0, 0)),
            pl.BlockSpec((1, d), lambda i, *_: (0, 0)),
            pl.BlockSpec((1, d), lambda i, *_: (0, 0)),
            pl.BlockSpec(memory_space=pl.ANY),
        ],
        out_specs=pl.BlockSpec((group * tm, d), lambda i, *_: (i, 0)),
        scratch_shapes=[pltpu.VMEM((COMBINE_SLOTS, 2 * tm * SUBLANES, LANES), F32),
                        pltpu.SemaphoreType.DMA((COMBINE_SLOTS,))],
    )
    return pl.pallas_call(
        functools.partial(_combine_kernel, alpha=alpha, tm=tm, n_exp=n_exp, group=group),
        grid_spec=grid_spec,
        out_shape=jax.ShapeDtypeStruct((t, d), F32),
        compiler_params=pltpu.CompilerParams(
            dimension_semantics=("arbitrary",), vmem_limit_bytes=VMEM_LIMIT),
        name="combine",
    )(seg_src, seg_len, xf, slots, gates, _row(ln_g), _row(ln_b), y)


def _moe_rows(x, slots, tab, w1, w3, w2, layer, *, bm=EXPERT_BLOCK):
    bsz, seq, d = x.shape
    t = bsz * seq
    n_exp = tab.shape[1]
    tm = t // tab.shape[0]
    xf = x.reshape(t, d)
    before, in_tile = tab[:, :, 0], tab[:, :, 1]
    counts = before[-1] + in_tile[-1]
    padded = (counts + bm - 1) // bm * bm
    pend = jnp.cumsum(padded)
    pstart = pend - padded
    n_blk = -(-(2 * t) // bm) + n_exp
    n_used = (pend[-1] // bm).reshape(1).astype(I32)
    seg_pos = (pstart[None, :] + before).reshape(-1).astype(I32)
    seg_len = in_tile.reshape(-1).astype(I32)
    group = math.gcd(TILES_PER_STEP, t // tm)
    xs = _dispatch(xf, slots, seg_pos, seg_len, (pstart + counts).astype(I32), (padded - counts).astype(I32),
                   n_used, n_blk * bm, tm=tm, bm=bm, group=group)
    y = _experts(xs, (pstart // bm).astype(I32), (padded // bm).astype(I32), n_used, w1, w3, w2, layer, bm=bm)
    return y, seg_pos, seg_len


def _moe_ln(x, y, slots, gates, seg_pos, seg_len, ln_g, ln_b, *, alpha):
    bsz, seq, d = x.shape
    t = bsz * seq
    n_tiles, _, tm = slots.shape
    group = math.gcd(TILES_PER_STEP, n_tiles)
    out = _combine(x.reshape(t, d), y, slots, gates, seg_pos, seg_len, ln_g, ln_b,
                   alpha=alpha, tm=tm, n_exp=seg_len.shape[0] // n_tiles, group=group)
    return out.reshape(bsz, seq, d)


def kernel(x, ln1_g, ln1_b, ln2_g, ln2_b, even_w_in, even_a_dw, even_a_dw_b, even_a_ln_g, even_a_ln_b, even_b_dw, even_w_out, odd_w_in, odd_c_dw, odd_c_dw_b, odd_w_gate_a, odd_b_gate_a, odd_w_gate_x, odd_b_gate_x, odd_lam, odd_w_out, w_router, b_router, moe_w1, moe_w3, moe_w2):
    depth = ln1_g.shape[0]
    alpha = (2.0 * depth) ** 0.25
    moe = None
    for layer in range(depth):
        j = layer // 2
        if layer % 2 == 0:
            if moe is not None:
                x = _moe_ln(x, *moe, alpha=alpha)
            x, slots, gates, tab, xb = _mix0(
                x, even_w_in[j], even_a_dw[j], even_a_dw_b[j], even_a_ln_g[j], even_a_ln_b[j],
                even_b_dw[j], even_w_out[j], ln1_g[layer], ln1_b[layer], w_router, b_router, alpha=alpha)
        else:
            x, slots, gates, tab, xb = _mix1(
                x, odd_w_in[j], odd_c_dw[j], odd_c_dw_b[j], odd_w_gate_a[j], odd_b_gate_a[j],
                odd_w_gate_x[j], odd_b_gate_x[j], odd_lam[j], odd_w_out[j], ln1_g[layer], ln1_b[layer],
                w_router, b_router, alpha=alpha, moe=moe)
        y, seg_pos, seg_len = _moe_rows(xb, slots, tab, moe_w1, moe_w3, moe_w2, layer)
        moe = (y, slots, gates, seg_pos, seg_len, ln2_g[layer], ln2_b[layer])
    return x if moe is None else _moe_ln(x, *moe, alpha=alpha)
```
